```python
import math
import jax, jax.numpy as jnp
from jax import lax
import numpy as np

D_MODEL = 1024
BATCH = 4
SEQ = 8192
DEPTH = 2

CHUNK = 64
Q_BLOCK = 128
HEAD_DIM = 64
A_HEADS = 4
A_VDIM = 2 * HEAD_DIM
B_HEADS = 8
B_LEFT_CHUNKS = 8
REL_CLIP = 128
C_HEADS = 4
C_QK_DIM = 64
C_V_DIM = 128
D_CHUNK = 128
D_GROUPS = 4
D_WIDTH = 512
D_FF = -(-(8 * D_MODEL) // (3 * 256)) * 256

A_W = A_HEADS * A_VDIM
B_W = B_HEADS * HEAD_DIM
C_W = C_HEADS * C_V_DIM
C_QK_W = C_HEADS * C_QK_DIM
IN_EVEN = 3 * A_W + 3 * B_W
IN_ODD = 2 * C_QK_W + 2 * C_W + 2 * D_WIDTH
MIX_EVEN = A_W + B_W
MIX_ODD = C_W + D_WIDTH
N_EVEN = (DEPTH + 1) // 2
N_ODD = DEPTH // 2
ALPHA = (2.0 * DEPTH) ** 0.25
BETA = (8.0 * DEPTH) ** -0.25
LN_EPS = 1e-5
NEG_INF = -1e30

kernel_name = 'hybrid_streaming_diffattn_band_retnet_sgu'


def layer_norm(x, g, b):
    xf = x.astype(jnp.float32)
    mu = jnp.mean(xf, -1, keepdims=True)
    var = jnp.mean(jnp.square(xf - mu), -1, keepdims=True)
    return ((xf - mu) * lax.rsqrt(var + LN_EPS)).astype(x.dtype) * g + b


def rms_norm(x, g):
    xf = x.astype(jnp.float32)
    return (xf * lax.rsqrt(jnp.mean(xf * xf, -1, keepdims=True) + LN_EPS)).astype(x.dtype) * g


def group_norm_heads(x, g, b):
    xf = x.astype(jnp.float32)
    mu = jnp.mean(xf, -1, keepdims=True)
    var = jnp.mean(jnp.square(xf - mu), -1, keepdims=True)
    y = ((xf - mu) * lax.rsqrt(var + LN_EPS)).astype(x.dtype)
    return y.reshape(x.shape[0], x.shape[1], -1) * g + b


def alibi_slopes(n_heads):
    return 2.0 ** (-8.0 * jnp.arange(1, n_heads + 1, dtype=jnp.float32) / n_heads)


def diff_attention(q, k, v, lam, lam_init, subln_g):
    bsz, seq, h, _, d = q.shape
    nb = seq // Q_BLOCK
    kpos = jnp.arange(seq)
    slopes = alibi_slopes(h)
    scale = d ** -0.5

    def one_block(args):
        qb, i = args
        tq = i * Q_BLOCK + jnp.arange(Q_BLOCK)
        s = jnp.einsum('bqhmd,bkhmd->bhmqk', qb, k).astype(jnp.float32) * scale
        dist = jnp.abs(tq[:, None] - kpos[None, :]).astype(jnp.float32)
        allowed = (kpos[None, :] // CHUNK) <= (tq[:, None] // CHUNK)
        bias = jnp.where(allowed[None], -slopes[:, None, None] * dist[None], NEG_INF)
        p = jax.nn.softmax(s + bias[None, :, None], axis=-1)
        w = (p[:, :, 0] - lam * p[:, :, 1]).astype(v.dtype)
        return jnp.einsum('bhqk,bkhe->bqhe', w, v)

    qbs = q.reshape(bsz, nb, Q_BLOCK, h, 2, d).swapaxes(0, 1)
    o = lax.map(one_block, (qbs, jnp.arange(nb)))
    o = o.swapaxes(0, 1).reshape(bsz, seq, h, 2 * d)
    o = rms_norm(o, subln_g) * (1.0 - lam_init)
    return o.reshape(bsz, seq, h * 2 * d)


def band_attention(q, k, v, rel_bias):
    bsz, seq, h, d = q.shape
    nc = seq // CHUNK
    nband = B_LEFT_CHUNKS + 1
    pad = B_LEFT_CHUNKS * CHUNK
    band = nband * CHUNK
    rel = pad + jnp.arange(CHUNK)[:, None] - jnp.arange(band)[None, :]
    bias = rel_bias.astype(jnp.float32)[:, jnp.clip(rel, -REL_CLIP, REL_CLIP) + REL_CLIP]
    key_pos = (jnp.arange(nc)[:, None] - B_LEFT_CHUNKS) * CHUNK + jnp.arange(band)[None, :]
    valid = key_pos >= 0
    scale = d ** -0.5

    def one_seq(args):
        qs, ks, vs = args
        qc = qs.reshape(nc, CHUNK, h, d)
        kp = jnp.pad(ks, ((pad, 0), (0, 0), (0, 0))).reshape(nc + B_LEFT_CHUNKS, CHUNK, h, d)
        vp = jnp.pad(vs, ((pad, 0), (0, 0), (0, 0))).reshape(nc + B_LEFT_CHUNKS, CHUNK, h, d)
        kb = jnp.concatenate([kp[j:j + nc] for j in range(nband)], axis=1)
        vb = jnp.concatenate([vp[j:j + nc] for j in range(nband)], axis=1)
        s = jnp.einsum('cqhd,ckhd->hcqk', qc, kb).astype(jnp.float32) * scale + bias[:, None]
        s = jnp.where(valid[None, :, None, :], s, NEG_INF)
        p = jax.nn.softmax(s, axis=-1).astype(vs.dtype)
        return jnp.einsum('hcqk,ckhd->cqhd', p, vb).reshape(seq, h * d)

    return lax.map(one_seq, (q, k, v))


def retention(q, k, v):
    bsz, seq, h, dk = q.shape
    dv = v.shape[-1]
    nc = seq // CHUNK
    log_gamma = jnp.log(1.0 - 2.0 ** (-5.0 - jnp.arange(h, dtype=jnp.float32)))
    idx = jnp.arange(CHUNK, dtype=jnp.float32)
    diff = idx[:, None] - idx[None, :]
    decay_mask = jnp.where(diff >= 0, jnp.exp(log_gamma[:, None, None] * jnp.maximum(diff, 0.0)), 0.0)
    q_decay = jnp.exp(log_gamma[:, None] * (idx + 1.0))
    k_decay = jnp.exp(log_gamma[:, None] * (CHUNK - 1.0 - idx))
    chunk_decay = jnp.exp(log_gamma * CHUNK)

    def to_chunks(t):
        return t.astype(jnp.float32).reshape(bsz, nc, CHUNK, h, t.shape[-1]).transpose(1, 0, 3, 2, 4)

    qc, kc, vc = to_chunks(q), to_chunks(k * dk ** -0.5), to_chunks(v)
    scores = jnp.einsum('cbhld,cbhmd->cbhlm', qc, kc) * decay_mask
    inner = jnp.einsum('cbhlm,cbhme->cbhle', scores, vc)
    kv = jnp.einsum('cbhmd,hm,cbhme->cbhde', kc, k_decay, vc)

    def step(state, kv_c):
        return chunk_decay[None, :, None, None] * state + kv_c, state

    _, prev = lax.scan(step, jnp.zeros((bsz, h, dk, dv), jnp.float32), kv)
    cross = jnp.einsum('cbhld,cbhde->cbhle', qc, prev) * q_decay[:, :, None]
    o = (inner + cross).transpose(1, 0, 3, 2, 4).reshape(bsz, seq, h, dv)
    return o.astype(q.dtype)


def spatial_gating(z, w_s, b_s, ln_g, ln_b):
    bsz, seq, _ = z.shape
    u, v = jnp.split(z, 2, axis=-1)
    v = layer_norm(v, ln_g, ln_b)
    gw = D_WIDTH // D_GROUPS
    vc = v.reshape(bsz, seq // D_CHUNK, D_CHUNK, D_GROUPS, gw)
    causal = jnp.tril(jnp.ones((D_CHUNK, D_CHUNK), dtype=bool))
    ws = jnp.where(causal[None], w_s, 0)
    mixed = jnp.einsum('gts,bnsgc->bntgc', ws, vc) + b_s.T[:, :, None]
    return u * mixed.reshape(bsz, seq, D_WIDTH)


def swiglu(x, w_gate, w_up, w_down):
    return (jax.nn.silu(x @ w_gate) * (x @ w_up)) @ w_down


def setup_inputs(seed: int = 0) -> dict:
    key = jax.random.key(seed)
    ks = jax.random.split(key, 26)
    f32 = jnp.float32

    def nrm(k, shape, std):
        return jax.random.normal(k, shape, f32) * std

    ne, no = N_EVEN, N_ODD
    return {
        'x': nrm(ks[0], (BATCH, SEQ, D_MODEL), 1.0),
        'w_in_even': nrm(ks[1], (ne, D_MODEL, IN_EVEN), D_MODEL ** -0.5),
        'lam_q1': nrm(ks[2], (ne, HEAD_DIM), 0.1),
        'lam_k1': nrm(ks[3], (ne, HEAD_DIM), 0.1),
        'lam_q2': nrm(ks[4], (ne, HEAD_DIM), 0.1),
        'lam_k2': nrm(ks[5], (ne, HEAD_DIM), 0.1),
        'diff_subln_g': 1.0 + nrm(ks[6], (ne, A_VDIM), 0.02),
        'rel_bias': nrm(ks[7], (ne, B_HEADS, 2 * REL_CLIP + 1), 0.1),
        'w_out_even': nrm(ks[8], (ne, MIX_EVEN, D_MODEL), BETA * MIX_EVEN ** -0.5),
        'w_in_odd': nrm(ks[9], (no, D_MODEL, IN_ODD), D_MODEL ** -0.5),
        'ret_gn_g': 1.0 + nrm(ks[10], (no, C_W), 0.02),
        'ret_gn_b': nrm(ks[11], (no, C_W), 0.02),
        'sgu_ln_g': 1.0 + nrm(ks[12], (no, D_WIDTH), 0.02),
        'sgu_ln_b': nrm(ks[13], (no, D_WIDTH), 0.02),
        'sgu_w': nrm(ks[14], (no, D_GROUPS, D_CHUNK, D_CHUNK), D_CHUNK ** -0.5),
        'sgu_b': 1.0 + nrm(ks[15], (no, D_GROUPS, D_CHUNK), 0.02),
        'w_out_odd': nrm(ks[16], (no, MIX_ODD, D_MODEL), BETA * MIX_ODD ** -0.5),
        'ln_mix_g': 1.0 + nrm(ks[17], (DEPTH, D_MODEL), 0.02),
        'ln_mix_b': nrm(ks[18], (DEPTH, D_MODEL), 0.02),
        'ffn_w_gate': nrm(ks[19], (DEPTH, D_MODEL, D_FF), D_MODEL ** -0.5),
        'ffn_w_up': nrm(ks[20], (DEPTH, D_MODEL, D_FF), D_MODEL ** -0.5),
        'ffn_w_down': nrm(ks[21], (DEPTH, D_FF, D_MODEL), BETA * D_FF ** -0.5),
        'ln_ffn_g': 1.0 + nrm(ks[22], (DEPTH, D_MODEL), 0.02),
        'ln_ffn_b': nrm(ks[23], (DEPTH, D_MODEL), 0.02),
    }


def reference(x, w_in_even, lam_q1, lam_k1, lam_q2, lam_k2, diff_subln_g, rel_bias, w_out_even,
              w_in_odd, ret_gn_g, ret_gn_b, sgu_ln_g, sgu_ln_b, sgu_w, sgu_b, w_out_odd,
              ln_mix_g, ln_mix_b, ffn_w_gate, ffn_w_up, ffn_w_down, ln_ffn_g, ln_ffn_b):
    bsz, seq, _ = x.shape
    for l in range(DEPTH):
        j = l // 2
        if l % 2 == 0:
            hproj = x @ w_in_even[j]
            qa, ka, va, qb, kb, vb = jnp.split(
                hproj, [A_W, 2 * A_W, 3 * A_W, 3 * A_W + B_W, 3 * A_W + 2 * B_W], axis=-1)
            lam_init = 0.8 - 0.6 * math.exp(-0.3 * l)
            lam = (jnp.exp(jnp.sum(lam_q1[j].astype(jnp.float32) * lam_k1[j].astype(jnp.float32)))
                   - jnp.exp(jnp.sum(lam_q2[j].astype(jnp.float32) * lam_k2[j].astype(jnp.float32)))
                   + lam_init)
            o_a = diff_attention(qa.reshape(bsz, seq, A_HEADS, 2, HEAD_DIM),
                                 ka.reshape(bsz, seq, A_HEADS, 2, HEAD_DIM),
                                 va.reshape(bsz, seq, A_HEADS, A_VDIM),
                                 lam, lam_init, diff_subln_g[j])
            o_b = band_attention(qb.reshape(bsz, seq, B_HEADS, HEAD_DIM),
                                 kb.reshape(bsz, seq, B_HEADS, HEAD_DIM),
                                 vb.reshape(bsz, seq, B_HEADS, HEAD_DIM),
                                 rel_bias[j])
            mix = jnp.concatenate([o_a, o_b], axis=-1) @ w_out_even[j]
        else:
            hproj = x @ w_in_odd[j]
            qc, kc, vc, gc, zd = jnp.split(
                hproj, [C_QK_W, 2 * C_QK_W, 2 * C_QK_W + C_W, 2 * C_QK_W + 2 * C_W], axis=-1)
            o_c = retention(qc.reshape(bsz, seq, C_HEADS, C_QK_DIM),
                            kc.reshape(bsz, seq, C_HEADS, C_QK_DIM),
                            vc.reshape(bsz, seq, C_HEADS, C_V_DIM))
            o_c = group_norm_heads(o_c, ret_gn_g[j], ret_gn_b[j]) * jax.nn.silu(gc)
            o_d = spatial_gating(jax.nn.gelu(zd), sgu_w[j], sgu_b[j], sgu_ln_g[j], sgu_ln_b[j])
            mix = jnp.concatenate([o_c, o_d], axis=-1) @ w_out_odd[j]
        x = layer_norm(ALPHA * x + mix, ln_mix_g[l], ln_mix_b[l])
        x = layer_norm(ALPHA * x + swiglu(x, ffn_w_gate[l], ffn_w_up[l], ffn_w_down[l]),
                       ln_ffn_g[l], ln_ffn_b[l])
    return x
```

```python
import functools
import math

import jax
import jax.numpy as jnp
from jax import lax
from jax.experimental import pallas as pl
from jax.experimental.pallas import tpu as pltpu

F32 = jnp.float32
BF16 = jnp.bfloat16

D_MODEL = 1024
HEAD_DIM = 64
CHUNK = 64
A_HEADS = 4
A_W = 512
B_HEADS = 8
B_W = 512
B_LEFT_CHUNKS = 8
REL_CLIP = 128
C_HEADS = 4
C_QK_W = 256
C_W = 512
D_CHUNK = 128
D_GROUPS = 4
D_WIDTH = 512
DEPTH = 2
ALPHA = (2.0 * DEPTH) ** 0.25
LN_EPS = 1e-5
NEG_INF = -1e30

VMEM_LIMIT_V7X = 56 * 1024 * 1024

ROW_BLOCK = 512
ATT_BLOCK = 256
BAND_BLOCK = 256
BAND_WINDOW = BAND_BLOCK + B_LEFT_CHUNKS * CHUNK
RET_BLOCK = 256
SGU_ROWS = 512


def _params(semantics):
    return pltpu.CompilerParams(dimension_semantics=semantics, vmem_limit_bytes=VMEM_LIMIT_V7X)


def _const_spec(shape):
    nd = len(shape)
    return pl.BlockSpec(shape, lambda *_: (0,) * nd, pipeline_mode=pl.Buffered(1))


def _layer_norm(y, g, b):
    mu = jnp.mean(y, axis=-1, keepdims=True)
    d = y - mu
    var = jnp.mean(d * d, axis=-1, keepdims=True)
    return d * lax.rsqrt(var + LN_EPS) * g + b


def _proj_kernel(x_ref, w_ref, *out_refs):
    x = x_ref[...].astype(BF16)
    off = 0
    for o_ref in out_refs:
        n = o_ref.shape[-1]
        for c in range(0, n, 512):
            acc = jnp.dot(x, w_ref[:, off + c:off + c + 512], preferred_element_type=F32)
            o_ref[:, c:c + 512] = acc.astype(o_ref.dtype)
        off += n


def _proj(x2d, w_bf16, out_widths, out_dtypes):
    t, k = x2d.shape
    n = w_bf16.shape[1]
    assert sum(out_widths) == n and all(wd % 512 == 0 for wd in out_widths)
    return pl.pallas_call(
        _proj_kernel,
        grid=(t // ROW_BLOCK,),
        in_specs=[pl.BlockSpec((ROW_BLOCK, k), lambda i: (i, 0)), _const_spec((k, n))],
        out_specs=[pl.BlockSpec((ROW_BLOCK, wd), lambda i: (i, 0)) for wd in out_widths],
        out_shape=[jax.ShapeDtypeStruct((t, wd), dt) for wd, dt in zip(out_widths, out_dtypes)],
        compiler_params=_params(("parallel",)),
        name="in_proj",
    )(x2d, w_bf16)


def _outproj_ln_kernel(a_ref, b_ref, x_ref, w_ref, g_ref, beta_ref, o_ref):
    ka = a_ref.shape[-1]
    mix = jnp.dot(a_ref[...], w_ref[0:ka, :], preferred_element_type=F32)
    mix = mix + jnp.dot(b_ref[...], w_ref[ka:, :], preferred_element_type=F32)
    y = ALPHA * x_ref[...] + mix
    o_ref[...] = _layer_norm(y, g_ref[...], beta_ref[...])


def _outproj_ln(a, b, x2d, w_bf16, g, beta):
    t, d = x2d.shape
    ka, kb = a.shape[1], b.shape[1]
    return pl.pallas_call(
        _outproj_ln_kernel,
        grid=(t // ROW_BLOCK,),
        in_specs=[pl.BlockSpec((ROW_BLOCK, ka), lambda i: (i, 0)),
                  pl.BlockSpec((ROW_BLOCK, kb), lambda i: (i, 0)),
                  pl.BlockSpec((ROW_BLOCK, d), lambda i: (i, 0)),
                  _const_spec((ka + kb, d)), _const_spec((1, d)), _const_spec((1, d))],
        out_specs=pl.BlockSpec((ROW_BLOCK, d), lambda i: (i, 0)),
        out_shape=jax.ShapeDtypeStruct((t, d), F32),
        compiler_params=_params(("parallel",)),
        name="out_proj_ln",
    )(a, b, x2d, w_bf16, g.reshape(1, d), beta.reshape(1, d))


def _ffn_ln_kernel(x_ref, wg_ref, wu_ref, wd_ref, g_ref, beta_ref, o_ref):
    x = x_ref[...]
    xb = x.astype(BF16)
    gate = jnp.dot(xb, wg_ref[...], preferred_element_type=F32)
    up = jnp.dot(xb, wu_ref[...], preferred_element_type=F32)
    h = (gate * jax.nn.sigmoid(gate) * up).astype(BF16)
    y = ALPHA * x + jnp.dot(h, wd_ref[...], preferred_element_type=F32)
    o_ref[...] = _layer_norm(y, g_ref[...], beta_ref[...])


def _ffn_ln(x2d, wg, wu, wd, g, beta):
    t, d = x2d.shape
    f = wg.shape[1]
    return pl.pallas_call(
        _ffn_ln_kernel,
        grid=(t // ROW_BLOCK,),
        in_specs=[pl.BlockSpec((ROW_BLOCK, d), lambda i: (i, 0)),
                  _const_spec((d, f)), _const_spec((d, f)), _const_spec((f, d)),
                  _const_spec((1, d)), _const_spec((1, d))],
        out_specs=pl.BlockSpec((ROW_BLOCK, d), lambda i: (i, 0)),
        out_shape=jax.ShapeDtypeStruct((t, d), F32),
        compiler_params=_params(("parallel",)),
        name="ffn_ln",
    )(x2d, wg, wu, wd, g.reshape(1, d), beta.reshape(1, d))


def _diff_attn_kernel(q_ref, k_ref, vt_ref, past_ref, diag_ref, step_ref,
                      lq1_ref, lk1_ref, lq2_ref, lk2_ref, g_ref, o_ref,
                      m_ref, l_ref, acc_ref, *, lam_init):
    t = ATT_BLOCK
    i = pl.program_id(2)
    scale = HEAD_DIM ** -0.5
    qt = q_ref[0].astype(F32).T * scale
    row = lax.broadcasted_iota(jnp.int32, qt.shape, 0)
    qts = (jnp.where(row < HEAD_DIM, qt, 0.0).astype(BF16),
           jnp.where(row >= HEAD_DIM, qt, 0.0).astype(BF16))
    step = step_ref[0]

    m_ref[...] = jnp.full(m_ref.shape, NEG_INF, F32)
    l_ref[...] = jnp.zeros(l_ref.shape, F32)
    acc_ref[...] = jnp.zeros(acc_ref.shape, F32)

    def block(j, tile):
        start = pl.multiple_of(j * t, t)
        k_blk = k_ref[0, pl.ds(start, t), :]
        vt_blk = vt_ref[0, 0, j]
        for mp in range(2):
            st = jnp.dot(k_blk, qts[mp], preferred_element_type=F32) + tile
            m_prev = m_ref[mp] - step
            m_new = jnp.maximum(m_prev, jnp.max(st, axis=0, keepdims=True))
            alpha = jnp.exp(m_prev - m_new)
            p = jnp.exp(st - m_new)
            l_ref[mp] = alpha * l_ref[mp] + jnp.sum(p, axis=0, keepdims=True)
            acc_ref[mp] = alpha * acc_ref[mp] + jnp.dot(vt_blk, p.astype(BF16),
                                                        preferred_element_type=F32)
            m_ref[mp] = m_new

    def past_block(j, carry):
        block(j, past_ref[0])
        return carry

    lax.fori_loop(0, i, past_block, 0)
    block(i, diag_ref[0])

    lam = (jnp.exp(jnp.sum(lq1_ref[...] * lk1_ref[...], axis=-1, keepdims=True))
           - jnp.exp(jnp.sum(lq2_ref[...] * lk2_ref[...], axis=-1, keepdims=True)) + lam_init)
    ot = acc_ref[0] / l_ref[0] - lam * (acc_ref[1] / l_ref[1])
    ms = jnp.mean(ot * ot, axis=0, keepdims=True)
    y = (ot * lax.rsqrt(ms + LN_EPS)).T * g_ref[...] * (1.0 - lam_init)
    o_ref[0] = y.astype(o_ref.dtype)


def _diff_attention(hproj, lam_q1, lam_k1, lam_q2, lam_k2, subln_g, lam_init):
    bsz, seq, _ = hproj.shape
    t = ATT_BLOCK
    nb = seq // t
    vd = 2 * HEAD_DIM
    vt = hproj[:, :, 2 * A_W:3 * A_W].reshape(bsz, nb, t, A_HEADS, vd).transpose(0, 3, 1, 4, 2)
    slopes = 2.0 ** (-8.0 * jnp.arange(1, A_HEADS + 1, dtype=F32) / A_HEADS)
    kk = jnp.arange(t)[:, None]
    qq = jnp.arange(t)[None, :]
    rel = (qq - kk).astype(F32)
    past = -slopes[:, None, None] * rel[None]
    allowed = (kk // CHUNK) <= (qq // CHUNK)
    diag = jnp.where(allowed[None], -slopes[:, None, None] * jnp.abs(rel)[None], NEG_INF)
    step = jnp.broadcast_to((slopes * t)[:, None, None], (A_HEADS, 1, t))
    vec = lambda v: v.reshape(1, HEAD_DIM).astype(F32)
    small = lambda shape: pl.BlockSpec(shape, lambda b, h, i: (0,) * len(shape))
    return pl.pallas_call(
        functools.partial(_diff_attn_kernel, lam_init=lam_init),
        grid=(bsz, A_HEADS, nb),
        in_specs=[pl.BlockSpec((1, t, vd), lambda b, h, i: (b, i, h)),
                  pl.BlockSpec((1, seq, vd), lambda b, h, i: (b, 0, A_HEADS + h)),
                  pl.BlockSpec((1, 1, nb, vd, t), lambda b, h, i: (b, h, 0, 0, 0)),
                  pl.BlockSpec((1, t, t), lambda b, h, i: (h, 0, 0)),
                  pl.BlockSpec((1, t, t), lambda b, h, i: (h, 0, 0)),
                  pl.BlockSpec((1, 1, t), lambda b, h, i: (h, 0, 0)),
                  small((1, HEAD_DIM)), small((1, HEAD_DIM)), small((1, HEAD_DIM)), small((1, HEAD_DIM)),
                  small((1, vd))],
        out_specs=pl.BlockSpec((1, t, vd), lambda b, h, i: (b, i, h)),
        out_shape=jax.ShapeDtypeStruct((bsz, seq, A_W), BF16),
        scratch_shapes=[pltpu.VMEM((2, 1, t), F32), pltpu.VMEM((2, 1, t), F32),
                        pltpu.VMEM((2, vd, t), F32)],
        compiler_params=_params(("parallel", "parallel", "arbitrary")),
        name="diff_attn",
    )(hproj, hproj, vt, past, diag, step, vec(lam_q1), vec(lam_k1), vec(lam_q2), vec(lam_k2),
      subln_g.reshape(1, vd).astype(F32))


def _band_attn_kernel(q_ref, k0_ref, k1_ref, k2_ref, v0_ref, v1_ref, v2_ref, bias_ref, o_ref):
    i = pl.program_id(2)
    q = q_ref[0]
    k = jnp.concatenate([k0_ref[0], k1_ref[0], k2_ref[0]], axis=0)
    v = jnp.concatenate([v0_ref[0], v1_ref[0], v2_ref[0]], axis=0)
    lane = lax.broadcasted_iota(jnp.int32, q.shape, 1)
    key_pos = (i - 2) * BAND_BLOCK + lax.broadcasted_iota(jnp.int32, (1, BAND_WINDOW), 1)
    valid = key_pos >= 0
    scale = HEAD_DIM ** -0.5
    outs = []
    for hh in range(2):
        in_head = (lane < HEAD_DIM) if hh == 0 else (lane >= HEAD_DIM)
        qh = jnp.where(in_head, q, jnp.zeros_like(q))
        s = lax.dot_general(qh, k, (((1,), (1,)), ((), ())), preferred_element_type=F32)
        s = jnp.where(valid, s * scale + bias_ref[hh], NEG_INF)
        m = jnp.max(s, axis=-1, keepdims=True)
        p = jnp.exp(s - m)
        l = jnp.sum(p, axis=-1, keepdims=True)
        outs.append(jnp.dot(p.astype(BF16), v, preferred_element_type=F32) / l)
    o_ref[0] = jnp.where(lane < HEAD_DIM, outs[0], outs[1]).astype(o_ref.dtype)


def _band_attention(hproj, rel_bias):
    bsz, seq, _ = hproj.shape
    t = BAND_BLOCK
    r = jnp.arange(t)[:, None]
    c = jnp.arange(BAND_WINDOW)[None, :]
    rel = r + B_LEFT_CHUNKS * CHUNK - c
    in_band = ((c // CHUNK) >= (r // CHUNK)) & ((c // CHUNK) <= (r // CHUNK) + B_LEFT_CHUNKS)
    bias = rel_bias.astype(F32)[:, jnp.clip(rel, -REL_CLIP, REL_CLIP) + REL_CLIP]
    bias = jnp.where(in_band[None], bias, NEG_INF)
    qcol, kcol, vcol = (3 * A_W) // 128, (3 * A_W + B_W) // 128, (3 * A_W + 2 * B_W) // 128

    def kv_spec(col, back):
        return pl.BlockSpec((1, t, 128), lambda b, hp, i: (b, jnp.maximum(i - back, 0), col + hp))

    return pl.pallas_call(
        _band_attn_kernel,
        grid=(bsz, B_HEADS // 2, seq // t),
        in_specs=[pl.BlockSpec((1, t, 128), lambda b, hp, i: (b, i, qcol + hp)),
                  kv_spec(kcol, 2), kv_spec(kcol, 1), kv_spec(kcol, 0),
                  kv_spec(vcol, 2), kv_spec(vcol, 1), kv_spec(vcol, 0),
                  pl.BlockSpec((2, t, BAND_WINDOW), lambda b, hp, i: (hp, 0, 0))],
        out_specs=pl.BlockSpec((1, t, 128), lambda b, hp, i: (b, i, hp)),
        out_shape=jax.ShapeDtypeStruct((bsz, seq, B_W), BF16),
        compiler_params=_params(("parallel", "parallel", "arbitrary")),
        name="band_attn",
    )(hproj, hproj, hproj, hproj, hproj, hproj, hproj, bias)


def _retention_kernel(q_ref, k_ref, v_ref, gate_ref, dmask_ref, qdec_ref, kdec_ref, cdec_ref,
                      g_ref, beta_ref, o_ref, state_ref):
    @pl.when(pl.program_id(2) == 0)
    def _():
        state_ref[...] = jnp.zeros(state_ref.shape, F32)

    q = q_ref[0]
    k = k_ref[0]
    lane = lax.broadcasted_iota(jnp.int32, q.shape, 1)
    scale = HEAD_DIM ** -0.5
    vd = 2 * HEAD_DIM
    for hh in range(2):
        in_head = (lane < HEAD_DIM) if hh == 0 else (lane >= HEAD_DIM)
        qh = jnp.where(in_head, q, jnp.zeros_like(q))
        kh = jnp.where(in_head, k.astype(F32) * scale, 0.0)
        v = v_ref[0, :, hh * vd:(hh + 1) * vd]
        scores = lax.dot_general(qh, kh.astype(BF16), (((1,), (1,)), ((), ())),
                                 preferred_element_type=F32) * dmask_ref[hh]
        inner = jnp.dot(scores.astype(BF16), v, preferred_element_type=F32)
        state = state_ref[hh]
        cross = jnp.dot(qh, state.astype(BF16), preferred_element_type=F32) * qdec_ref[hh]
        kdt = (kh * kdec_ref[hh]).T.astype(BF16)
        kv = jnp.dot(kdt, v, preferred_element_type=F32)
        state_ref[hh] = cdec_ref[hh] * state + kv
        o = inner + cross
        mu = jnp.mean(o, axis=-1, keepdims=True)
        d = o - mu
        var = jnp.mean(d * d, axis=-1, keepdims=True)
        sl = slice(hh * vd, (hh + 1) * vd)
        y = d * lax.rsqrt(var + LN_EPS) * g_ref[:, sl] + beta_ref[:, sl]
        gate = gate_ref[0, :, sl]
        o_ref[0, :, sl] = (y * (gate * jax.nn.sigmoid(gate))).astype(o_ref.dtype)


def _retention(qkv, gz, gn_g, gn_b):
    bsz, seq, _ = qkv.shape
    ln = RET_BLOCK
    vd = 2 * HEAD_DIM
    log_gamma = jnp.log(1.0 - 2.0 ** (-5.0 - jnp.arange(C_HEADS, dtype=F32)))
    idx = jnp.arange(ln, dtype=F32)
    diff = idx[:, None] - idx[None, :]
    dmask = jnp.where(diff >= 0, jnp.exp(log_gamma[:, None, None] * jnp.maximum(diff, 0.0)), 0.0)
    qdec = jnp.broadcast_to(jnp.exp(log_gamma[:, None] * (idx + 1.0))[:, :, None], (C_HEADS, ln, vd))
    kdec = jnp.broadcast_to(jnp.exp(log_gamma[:, None] * (ln - 1.0 - idx))[:, :, None], (C_HEADS, ln, vd))
    cdec = jnp.broadcast_to(jnp.exp(log_gamma * ln)[:, None, None], (C_HEADS, 1, vd))
    pair = lambda shape: pl.BlockSpec(shape, lambda b, hp, c: (hp,) + (0,) * (len(shape) - 1))
    return pl.pallas_call(
        _retention_kernel,
        grid=(bsz, C_HEADS // 2, seq // ln),
        in_specs=[pl.BlockSpec((1, ln, 128), lambda b, hp, c: (b, c, hp)),
                  pl.BlockSpec((1, ln, 128), lambda b, hp, c: (b, c, C_QK_W // 128 + hp)),
                  pl.BlockSpec((1, ln, 2 * vd), lambda b, hp, c: (b, c, (2 * C_QK_W) // (2 * vd) + hp)),
                  pl.BlockSpec((1, ln, 2 * vd), lambda b, hp, c: (b, c, hp)),
                  pair((2, ln, ln)), pair((2, ln, vd)), pair((2, ln, vd)), pair((2, 1, vd)),
                  pl.BlockSpec((1, 2 * vd), lambda b, hp, c: (0, hp)),
                  pl.BlockSpec((1, 2 * vd), lambda b, hp, c: (0, hp))],
        out_specs=pl.BlockSpec((1, ln, 2 * vd), lambda b, hp, c: (b, c, hp)),
        out_shape=jax.ShapeDtypeStruct((bsz, seq, C_W), BF16),
        scratch_shapes=[pltpu.VMEM((2, vd, vd), F32)],
        compiler_params=_params(("parallel", "parallel", "arbitrary")),
        name="retention",
    )(qkv, qkv, qkv, gz, dmask, qdec, kdec, cdec, gn_g.reshape(1, C_W), gn_b.reshape(1, C_W))


def _sgu_kernel(u_ref, v_ref, w_ref, bt_ref, g_ref, beta_ref, o_ref):
    u = jax.nn.gelu(u_ref[0])
    v = _layer_norm(jax.nn.gelu(v_ref[0]), g_ref[...], beta_ref[...])
    gw = D_WIDTH // D_GROUPS
    r = lax.broadcasted_iota(jnp.int32, (D_CHUNK, D_CHUNK), 0)
    c = lax.broadcasted_iota(jnp.int32, (D_CHUNK, D_CHUNK), 1)
    for g in range(D_GROUPS):
        w = jnp.where(r >= c, w_ref[g], 0.0).astype(BF16)
        bias = bt_ref[:, g:g + 1]
        cols = slice(g * gw, (g + 1) * gw)
        vg = v[:, cols].astype(BF16)
        for n in range(SGU_ROWS // D_CHUNK):
            rows = slice(n * D_CHUNK, (n + 1) * D_CHUNK)
            mixed = jnp.dot(w, vg[rows], preferred_element_type=F32) + bias
            o_ref[0, rows, cols] = (u[rows, cols] * mixed).astype(o_ref.dtype)


def _spatial_gating(gz, w_s, b_s, ln_g, ln_b):
    bsz, seq, _ = gz.shape
    rws = SGU_ROWS
    small = lambda shape: pl.BlockSpec(shape, lambda b, i: (0,) * len(shape))
    return pl.pallas_call(
        _sgu_kernel,
        grid=(bsz, seq // rws),
        in_specs=[pl.BlockSpec((1, rws, D_WIDTH), lambda b, i: (b, i, 1)),
                  pl.BlockSpec((1, rws, D_WIDTH), lambda b, i: (b, i, 2)),
                  small((D_GROUPS, D_CHUNK, D_CHUNK)), small((D_CHUNK, D_GROUPS)),
                  small((1, D_WIDTH)), small((1, D_WIDTH))],
        out_specs=pl.BlockSpec((1, rws, D_WIDTH), lambda b, i: (b, i, 0)),
        out_shape=jax.ShapeDtypeStruct((bsz, seq, D_WIDTH), BF16),
        compiler_params=_params(("parallel", "parallel")),
        name="spatial_gating",
    )(gz, gz, w_s.astype(F32), b_s.T.astype(F32), ln_g.reshape(1, D_WIDTH), ln_b.reshape(1, D_WIDTH))


def kernel(x, w_in_even, lam_q1, lam_k1, lam_q2, lam_k2, diff_subln_g, rel_bias, w_out_even,
           w_in_odd, ret_gn_g, ret_gn_b, sgu_ln_g, sgu_ln_b, sgu_w, sgu_b, w_out_odd,
           ln_mix_g, ln_mix_b, ffn_w_gate, ffn_w_up, ffn_w_down, ln_ffn_g, ln_ffn_b):
    bsz, seq, d = x.shape
    t = bsz * seq
    x2d = x.reshape(t, d)
    for l in range(DEPTH):
        j = l // 2
        if l % 2 == 0:
            (hproj,) = _proj(x2d, w_in_even[j].astype(BF16), [w_in_even.shape[-1]], [BF16])
            hproj = hproj.reshape(bsz, seq, -1)
            lam_init = 0.8 - 0.6 * math.exp(-0.3 * l)
            o_a = _diff_attention(hproj, lam_q1[j], lam_k1[j], lam_q2[j], lam_k2[j],
                                  diff_subln_g[j], lam_init)
            o_b = _band_attention(hproj, rel_bias[j])
            x2d = _outproj_ln(o_a.reshape(t, -1), o_b.reshape(t, -1), x2d,
                              w_out_even[j].astype(BF16), ln_mix_g[l], ln_mix_b[l])
        else:
            qkv, gz = _proj(x2d, w_in_odd[j].astype(BF16), [2 * C_QK_W + C_W, C_W + 2 * D_WIDTH],
                            [BF16, F32])
            qkv = qkv.reshape(bsz, seq, -1)
            gz = gz.reshape(bsz, seq, -1)
            o_c = _retention(qkv, gz, ret_gn_g[j], ret_gn_b[j])
            o_d = _spatial_gating(gz, sgu_w[j], sgu_b[j], sgu_ln_g[j], sgu_ln_b[j])
            x2d = _outproj_ln(o_c.reshape(t, -1), o_d.reshape(t, -1), x2d,
                              w_out_odd[j].astype(BF16), ln_mix_g[l], ln_mix_b[l])
        x2d = _ffn_ln(x2d, ffn_w_gate[l].astype(BF16), ffn_w_up[l].astype(BF16),
                      ffn_w_down[l].astype(BF16), ln_ffn_g[l], ln_ffn_b[l])
    return x2d.reshape(bsz, seq, d)
```

```python
import functools
import math

import jax
import jax.numpy as jnp
from jax import lax
from jax.experimental import pallas as pl
from jax.experimental.pallas import tpu as pltpu

F32 = jnp.float32
BF16 = jnp.bfloat16

D_MODEL = 1024
HEAD_DIM = 64
CHUNK = 64
A_HEADS = 4
A_W = 512
B_HEADS = 8
B_W = 512
B_LEFT_CHUNKS = 8
REL_CLIP = 128
C_HEADS = 4
C_QK_W = 256
C_W = 512
D_CHUNK = 128
D_GROUPS = 4
D_WIDTH = 512
DEPTH = 2
ALPHA = (2.0 * DEPTH) ** 0.25
LN_EPS = 1e-5
NEG_INF = -1e30

VMEM_LIMIT_V7X = 56 * 1024 * 1024

ROW_BLOCK = 512
ATT_BLOCK = 512
BAND_BLOCK = 256
BAND_WINDOW = BAND_BLOCK + B_LEFT_CHUNKS * CHUNK
RET_BLOCK = 256
SGU_ROWS = 512


def _params(semantics):
    return pltpu.CompilerParams(dimension_semantics=semantics, vmem_limit_bytes=VMEM_LIMIT_V7X)


def _const_spec(shape):
    nd = len(shape)
    return pl.BlockSpec(shape, lambda *_: (0,) * nd, pipeline_mode=pl.Buffered(1))


def _layer_norm(y, g, b):
    mu = jnp.mean(y, axis=-1, keepdims=True)
    d = y - mu
    var = jnp.mean(d * d, axis=-1, keepdims=True)
    return d * lax.rsqrt(var + LN_EPS) * g + b


def _proj_kernel(x_ref, w_ref, *out_refs):
    x = x_ref[...].astype(BF16)
    off = 0
    for o_ref in out_refs:
        n = o_ref.shape[-1]
        for c in range(0, n, 512):
            acc = jnp.dot(x, w_ref[:, off + c:off + c + 512], preferred_element_type=F32)
            o_ref[:, c:c + 512] = acc.astype(o_ref.dtype)
        off += n


def _proj(x2d, w_bf16, out_widths, out_dtypes):
    t, k = x2d.shape
    n = w_bf16.shape[1]
    assert sum(out_widths) == n and all(wd % 512 == 0 for wd in out_widths)
    return pl.pallas_call(
        _proj_kernel,
        grid=(t // ROW_BLOCK,),
        in_specs=[pl.BlockSpec((ROW_BLOCK, k), lambda i: (i, 0)), _const_spec((k, n))],
        out_specs=[pl.BlockSpec((ROW_BLOCK, wd), lambda i: (i, 0)) for wd in out_widths],
        out_shape=[jax.ShapeDtypeStruct((t, wd), dt) for wd, dt in zip(out_widths, out_dtypes)],
        compiler_params=_params(("parallel",)),
        name="in_proj",
    )(x2d, w_bf16)


def _outproj_ln_kernel(a_ref, b_ref, x_ref, w_ref, g_ref, beta_ref, o_ref):
    ka = a_ref.shape[-1]
    mix = jnp.dot(a_ref[...], w_ref[0:ka, :], preferred_element_type=F32)
    mix = mix + jnp.dot(b_ref[...], w_ref[ka:, :], preferred_element_type=F32)
    y = ALPHA * x_ref[...] + mix
    o_ref[...] = _layer_norm(y, g_ref[...], beta_ref[...])


def _outproj_ln(a, b, x2d, w_bf16, g, beta):
    t, d = x2d.shape
    ka, kb = a.shape[1], b.shape[1]
    return pl.pallas_call(
        _outproj_ln_kernel,
        grid=(t // ROW_BLOCK,),
        in_specs=[pl.BlockSpec((ROW_BLOCK, ka), lambda i: (i, 0)),
                  pl.BlockSpec((ROW_BLOCK, kb), lambda i: (i, 0)),
                  pl.BlockSpec((ROW_BLOCK, d), lambda i: (i, 0)),
                  _const_spec((ka + kb, d)), _const_spec((1, d)), _const_spec((1, d))],
        out_specs=pl.BlockSpec((ROW_BLOCK, d), lambda i: (i, 0)),
        out_shape=jax.ShapeDtypeStruct((t, d), F32),
        compiler_params=_params(("parallel",)),
        name="out_proj_ln",
    )(a, b, x2d, w_bf16, g.reshape(1, d), beta.reshape(1, d))


def _ffn_ln_kernel(x_ref, wg_ref, wu_ref, wd_ref, g_ref, beta_ref, o_ref):
    x = x_ref[...]
    xb = x.astype(BF16)
    gate = jnp.dot(xb, wg_ref[...], preferred_element_type=F32)
    up = jnp.dot(xb, wu_ref[...], preferred_element_type=F32)
    h = (gate * jax.nn.sigmoid(gate) * up).astype(BF16)
    y = ALPHA * x + jnp.dot(h, wd_ref[...], preferred_element_type=F32)
    o_ref[...] = _layer_norm(y, g_ref[...], beta_ref[...])


def _ffn_ln(x2d, wg, wu, wd, g, beta):
    t, d = x2d.shape
    f = wg.shape[1]
    return pl.pallas_call(
        _ffn_ln_kernel,
        grid=(t // ROW_BLOCK,),
        in_specs=[pl.BlockSpec((ROW_BLOCK, d), lambda i: (i, 0)),
                  _const_spec((d, f)), _const_spec((d, f)), _const_spec((f, d)),
                  _const_spec((1, d)), _const_spec((1, d))],
        out_specs=pl.BlockSpec((ROW_BLOCK, d), lambda i: (i, 0)),
        out_shape=jax.ShapeDtypeStruct((t, d), F32),
        compiler_params=_params(("parallel",)),
        name="ffn_ln",
    )(x2d, wg, wu, wd, g.reshape(1, d), beta.reshape(1, d))


def _diff_attn_kernel(q_ref, k_ref, vt_ref, past_ref, diag_ref, step_ref,
                      lq1_ref, lk1_ref, lq2_ref, lk2_ref, g_ref, o_ref,
                      qt_ref, sa_ref, sb_ref, m_ref, l_ref, acc_ref, *, lam_init):
    t = ATT_BLOCK
    i = pl.program_id(2)
    scale = HEAD_DIM ** -0.5
    qt = q_ref[0].astype(F32).T * scale
    row = lax.broadcasted_iota(jnp.int32, qt.shape, 0)
    qt_ref[0] = jnp.where(row < HEAD_DIM, qt, 0.0).astype(BF16)
    qt_ref[1] = jnp.where(row >= HEAD_DIM, qt, 0.0).astype(BF16)

    m_ref[...] = jnp.full(m_ref.shape, NEG_INF, F32)
    l_ref[...] = jnp.zeros(l_ref.shape, F32)
    acc_ref[...] = jnp.zeros(acc_ref.shape, F32)

    def scores(j, s_ref):
        start = pl.multiple_of(j * t, t)
        k_blk = k_ref[0, pl.ds(start, t), :]
        for mp in range(2):
            s_ref[mp] = jnp.dot(k_blk, qt_ref[mp], preferred_element_type=F32)

    def softmax_pv(j, s_ref, tile_ref):
        vt_blk = vt_ref[0, 0, j]
        step = step_ref[0]
        for mp in range(2):
            st = s_ref[mp] + tile_ref[0]
            m_prev = m_ref[mp] - step
            m_new = jnp.maximum(m_prev, jnp.max(st, axis=0, keepdims=True))
            alpha = jnp.exp(m_prev - m_new)
            p = jnp.exp(st - m_new)
            l_ref[mp] = alpha * l_ref[mp] + jnp.sum(p, axis=0, keepdims=True)
            acc_ref[mp] = alpha * acc_ref[mp] + jnp.dot(vt_blk, p.astype(BF16),
                                                        preferred_element_type=F32)
            m_ref[mp] = m_new

    scores(0, sa_ref)

    def past_pair(jj, carry):
        j0 = 2 * jj
        scores(j0 + 1, sb_ref)
        softmax_pv(j0, sa_ref, past_ref)
        scores(j0 + 2, sa_ref)
        softmax_pv(j0 + 1, sb_ref, past_ref)
        return carry

    lax.fori_loop(0, lax.shift_right_logical(i, 1), past_pair, 0)

    @pl.when((i & 1) == 0)
    def _():
        softmax_pv(i, sa_ref, diag_ref)

    @pl.when((i & 1) == 1)
    def _():
        scores(i, sb_ref)
        softmax_pv(i - 1, sa_ref, past_ref)
        softmax_pv(i, sb_ref, diag_ref)

    lam = (jnp.exp(jnp.sum(lq1_ref[...] * lk1_ref[...], axis=-1, keepdims=True))
           - jnp.exp(jnp.sum(lq2_ref[...] * lk2_ref[...], axis=-1, keepdims=True)) + lam_init)
    ot = acc_ref[0] / l_ref[0] - lam * (acc_ref[1] / l_ref[1])
    ms = jnp.mean(ot * ot, axis=0, keepdims=True)
    y = (ot * lax.rsqrt(ms + LN_EPS)).T * g_ref[...] * (1.0 - lam_init)
    o_ref[0] = y.astype(o_ref.dtype)


def _diff_attention(hproj, lam_q1, lam_k1, lam_q2, lam_k2, subln_g, lam_init):
    bsz, seq, _ = hproj.shape
    t = ATT_BLOCK
    nb = seq // t
    vd = 2 * HEAD_DIM
    vt = hproj[:, :, 2 * A_W:3 * A_W].reshape(bsz, nb, t, A_HEADS, vd).transpose(0, 3, 1, 4, 2)
    slopes = 2.0 ** (-8.0 * jnp.arange(1, A_HEADS + 1, dtype=F32) / A_HEADS)
    kk = jnp.arange(t)[:, None]
    qq = jnp.arange(t)[None, :]
    rel = (qq - kk).astype(F32)
    past = -slopes[:, None, None] * rel[None]
    allowed = (kk // CHUNK) <= (qq // CHUNK)
    diag = jnp.where(allowed[None], -slopes[:, None, None] * jnp.abs(rel)[None], NEG_INF)
    step = jnp.broadcast_to((slopes * t)[:, None, None], (A_HEADS, 1, t))
    vec = lambda v: v.reshape(1, HEAD_DIM).astype(F32)
    small = lambda shape: pl.BlockSpec(shape, lambda b, h, i: (0,) * len(shape))
    return pl.pallas_call(
        functools.partial(_diff_attn_kernel, lam_init=lam_init),
        grid=(bsz, A_HEADS, nb),
        in_specs=[pl.BlockSpec((1, t, vd), lambda b, h, i: (b, i, h)),
                  pl.BlockSpec((1, seq, vd), lambda b, h, i: (b, 0, A_HEADS + h)),
                  pl.BlockSpec((1, 1, nb, vd, t), lambda b, h, i: (b, h, 0, 0, 0)),
                  pl.BlockSpec((1, t, t), lambda b, h, i: (h, 0, 0)),
                  pl.BlockSpec((1, t, t), lambda b, h, i: (h, 0, 0)),
                  pl.BlockSpec((1, 1, t), lambda b, h, i: (h, 0, 0)),
                  small((1, HEAD_DIM)), small((1, HEAD_DIM)), small((1, HEAD_DIM)), small((1, HEAD_DIM)),
                  small((1, vd))],
        out_specs=pl.BlockSpec((1, t, vd), lambda b, h, i: (b, i, h)),
        out_shape=jax.ShapeDtypeStruct((bsz, seq, A_W), BF16),
        scratch_shapes=[pltpu.VMEM((2, vd, t), BF16),
                        pltpu.VMEM((2, t, t), F32), pltpu.VMEM((2, t, t), F32),
                        pltpu.VMEM((2, 1, t), F32), pltpu.VMEM((2, 1, t), F32),
                        pltpu.VMEM((2, vd, t), F32)],
        compiler_params=_params(("parallel", "parallel", "arbitrary")),
        name="diff_attn",
    )(hproj, hproj, vt, past, diag, step, vec(lam_q1), vec(lam_k1), vec(lam_q2), vec(lam_k2),
      subln_g.reshape(1, vd).astype(F32))


def _band_attn_kernel(q_ref, k0_ref, k1_ref, k2_ref, v0_ref, v1_ref, v2_ref, bias_ref, o_ref):
    i = pl.program_id(2)
    q = q_ref[0]
    k = jnp.concatenate([k0_ref[0], k1_ref[0], k2_ref[0]], axis=0)
    v = jnp.concatenate([v0_ref[0], v1_ref[0], v2_ref[0]], axis=0)
    lane = lax.broadcasted_iota(jnp.int32, q.shape, 1)
    key_pos = (i - 2) * BAND_BLOCK + lax.broadcasted_iota(jnp.int32, (1, BAND_WINDOW), 1)
    valid = key_pos >= 0
    scale = HEAD_DIM ** -0.5
    outs = []
    for hh in range(2):
        in_head = (lane < HEAD_DIM) if hh == 0 else (lane >= HEAD_DIM)
        qh = jnp.where(in_head, q, jnp.zeros_like(q))
        s = lax.dot_general(qh, k, (((1,), (1,)), ((), ())), preferred_element_type=F32)
        s = jnp.where(valid, s * scale + bias_ref[hh], NEG_INF)
        m = jnp.max(s, axis=-1, keepdims=True)
        p = jnp.exp(s - m)
        l = jnp.sum(p, axis=-1, keepdims=True)
        outs.append(jnp.dot(p.astype(BF16), v, preferred_element_type=F32) / l)
    o_ref[0] = jnp.where(lane < HEAD_DIM, outs[0], outs[1]).astype(o_ref.dtype)


def _band_attention(hproj, rel_bias):
    bsz, seq, _ = hproj.shape
    t = BAND_BLOCK
    r = jnp.arange(t)[:, None]
    c = jnp.arange(BAND_WINDOW)[None, :]
    in_band = ((c // CHUNK) >= (r // CHUNK)) & ((c // CHUNK) <= (r // CHUNK) + B_LEFT_CHUNKS)
    rb = rel_bias.astype(F32)
    n_diag = t + BAND_WINDOW - 1
    lo = jnp.broadcast_to(rb[:, :1], (B_HEADS, t - 1 - REL_CLIP))
    hi = jnp.broadcast_to(rb[:, -1:], (B_HEADS, n_diag - (t - 1 - REL_CLIP) - (2 * REL_CLIP + 1)))
    vals = jnp.concatenate([lo, rb, hi], axis=1)
    sheared = jnp.tile(vals, (1, t + 1))[:, :t * (n_diag + 1)].reshape(B_HEADS, t, n_diag + 1)
    bias = sheared[:, :, :BAND_WINDOW][:, :, ::-1]
    bias = jnp.where(in_band[None], bias, NEG_INF)
    qcol, kcol, vcol = (3 * A_W) // 128, (3 * A_W + B_W) // 128, (3 * A_W + 2 * B_W) // 128

    def kv_spec(col, back):
        return pl.BlockSpec((1, t, 128), lambda b, hp, i: (b, jnp.maximum(i - back, 0), col + hp))

    return pl.pallas_call(
        _band_attn_kernel,
        grid=(bsz, B_HEADS // 2, seq // t),
        in_specs=[pl.BlockSpec((1, t, 128), lambda b, hp, i: (b, i, qcol + hp)),
                  kv_spec(kcol, 2), kv_spec(kcol, 1), kv_spec(kcol, 0),
                  kv_spec(vcol, 2), kv_spec(vcol, 1), kv_spec(vcol, 0),
                  pl.BlockSpec((2, t, BAND_WINDOW), lambda b, hp, i: (hp, 0, 0))],
        out_specs=pl.BlockSpec((1, t, 128), lambda b, hp, i: (b, i, hp)),
        out_shape=jax.ShapeDtypeStruct((bsz, seq, B_W), BF16),
        compiler_params=_params(("parallel", "parallel", "arbitrary")),
        name="band_attn",
    )(hproj, hproj, hproj, hproj, hproj, hproj, hproj, bias)


def _retention_kernel(q_ref, k_ref, v_ref, gate_ref, dmask_ref, qdec_ref, kdec_ref, cdec_ref,
                      g_ref, beta_ref, o_ref, state_ref):
    @pl.when(pl.program_id(2) == 0)
    def _():
        state_ref[...] = jnp.zeros(state_ref.shape, F32)

    q = q_ref[0]
    k = k_ref[0]
    lane = lax.broadcasted_iota(jnp.int32, q.shape, 1)
    scale = HEAD_DIM ** -0.5
    vd = 2 * HEAD_DIM
    for hh in range(2):
        in_head = (lane < HEAD_DIM) if hh == 0 else (lane >= HEAD_DIM)
        qh = jnp.where(in_head, q, jnp.zeros_like(q))
        kh = jnp.where(in_head, k.astype(F32) * scale, 0.0)
        v = v_ref[0, :, hh * vd:(hh + 1) * vd]
        scores = lax.dot_general(qh, kh.astype(BF16), (((1,), (1,)), ((), ())),
                                 preferred_element_type=F32) * dmask_ref[hh]
        inner = jnp.dot(scores.astype(BF16), v, preferred_element_type=F32)
        state = state_ref[hh]
        cross = jnp.dot(qh, state.astype(BF16), preferred_element_type=F32) * qdec_ref[hh]
        kdt = (kh * kdec_ref[hh]).T.astype(BF16)
        kv = jnp.dot(kdt, v, preferred_element_type=F32)
        state_ref[hh] = cdec_ref[hh] * state + kv
        o = inner + cross
        mu = jnp.mean(o, axis=-1, keepdims=True)
        d = o - mu
        var = jnp.mean(d * d, axis=-1, keepdims=True)
        sl = slice(hh * vd, (hh + 1) * vd)
        y = d * lax.rsqrt(var + LN_EPS) * g_ref[:, sl] + beta_ref[:, sl]
        gate = gate_ref[0, :, sl]
        o_ref[0, :, sl] = (y * (gate * jax.nn.sigmoid(gate))).astype(o_ref.dtype)


def _retention(qkv, gz, gn_g, gn_b):
    bsz, seq, _ = qkv.shape
    ln = RET_BLOCK
    vd = 2 * HEAD_DIM
    log_gamma = jnp.log(1.0 - 2.0 ** (-5.0 - jnp.arange(C_HEADS, dtype=F32)))
    idx = jnp.arange(ln, dtype=F32)
    diff = idx[:, None] - idx[None, :]
    dmask = jnp.where(diff >= 0, jnp.exp(log_gamma[:, None, None] * jnp.maximum(diff, 0.0)), 0.0)
    qdec = jnp.broadcast_to(jnp.exp(log_gamma[:, None] * (idx + 1.0))[:, :, None], (C_HEADS, ln, vd))
    kdec = jnp.broadcast_to(jnp.exp(log_gamma[:, None] * (ln - 1.0 - idx))[:, :, None], (C_HEADS, ln, vd))
    cdec = jnp.broadcast_to(jnp.exp(log_gamma * ln)[:, None, None], (C_HEADS, 1, vd))
    pair = lambda shape: pl.BlockSpec(shape, lambda b, hp, c: (hp,) + (0,) * (len(shape) - 1))
    return pl.pallas_call(
        _retention_kernel,
        grid=(bsz, C_HEADS // 2, seq // ln),
        in_specs=[pl.BlockSpec((1, ln, 128), lambda b, hp, c: (b, c, hp)),
                  pl.BlockSpec((1, ln, 128), lambda b, hp, c: (b, c, C_QK_W // 128 + hp)),
                  pl.BlockSpec((1, ln, 2 * vd), lambda b, hp, c: (b, c, (2 * C_QK_W) // (2 * vd) + hp)),
                  pl.BlockSpec((1, ln, 2 * vd), lambda b, hp, c: (b, c, hp)),
                  pair((2, ln, ln)), pair((2, ln, vd)), pair((2, ln, vd)), pair((2, 1, vd)),
                  pl.BlockSpec((1, 2 * vd), lambda b, hp, c: (0, hp)),
                  pl.BlockSpec((1, 2 * vd), lambda b, hp, c: (0, hp))],
        out_specs=pl.BlockSpec((1, ln, 2 * vd), lambda b, hp, c: (b, c, hp)),
        out_shape=jax.ShapeDtypeStruct((bsz, seq, C_W), BF16),
        scratch_shapes=[pltpu.VMEM((2, vd, vd), F32)],
        compiler_params=_params(("parallel", "parallel", "arbitrary")),
        name="retention",
    )(qkv, qkv, qkv, gz, dmask, qdec, kdec, cdec, gn_g.reshape(1, C_W), gn_b.reshape(1, C_W))


def _sgu_kernel(u_ref, v_ref, w_ref, bt_ref, g_ref, beta_ref, o_ref):
    u = jax.nn.gelu(u_ref[0])
    v = _layer_norm(jax.nn.gelu(v_ref[0]), g_ref[...], beta_ref[...])
    gw = D_WIDTH // D_GROUPS
    r = lax.broadcasted_iota(jnp.int32, (D_CHUNK, D_CHUNK), 0)
    c = lax.broadcasted_iota(jnp.int32, (D_CHUNK, D_CHUNK), 1)
    for g in range(D_GROUPS):
        w = jnp.where(r >= c, w_ref[g], 0.0).astype(BF16)
        bias = bt_ref[:, g:g + 1]
        cols = slice(g * gw, (g + 1) * gw)
        vg = v[:, cols].astype(BF16)
        for n in range(SGU_ROWS // D_CHUNK):
            rows = slice(n * D_CHUNK, (n + 1) * D_CHUNK)
            mixed = jnp.dot(w, vg[rows], preferred_element_type=F32) + bias
            o_ref[0, rows, cols] = (u[rows, cols] * mixed).astype(o_ref.dtype)


def _spatial_gating(gz, w_s, b_s, ln_g, ln_b):
    bsz, seq, _ = gz.shape
    rws = SGU_ROWS
    small = lambda shape: pl.BlockSpec(shape, lambda b, i: (0,) * len(shape))
    return pl.pallas_call(
        _sgu_kernel,
        grid=(bsz, seq // rws),
        in_specs=[pl.BlockSpec((1, rws, D_WIDTH), lambda b, i: (b, i, 1)),
                  pl.BlockSpec((1, rws, D_WIDTH), lambda b, i: (b, i, 2)),
                  small((D_GROUPS, D_CHUNK, D_CHUNK)), small((D_CHUNK, D_GROUPS)),
                  small((1, D_WIDTH)), small((1, D_WIDTH))],
        out_specs=pl.BlockSpec((1, rws, D_WIDTH), lambda b, i: (b, i, 0)),
        out_shape=jax.ShapeDtypeStruct((bsz, seq, D_WIDTH), BF16),
        compiler_params=_params(("parallel", "parallel")),
        name="spatial_gating",
    )(gz, gz, w_s.astype(F32), b_s.T.astype(F32), ln_g.reshape(1, D_WIDTH), ln_b.reshape(1, D_WIDTH))


def kernel(x, w_in_even, lam_q1, lam_k1, lam_q2, lam_k2, diff_subln_g, rel_bias, w_out_even,
           w_in_odd, ret_gn_g, ret_gn_b, sgu_ln_g, sgu_ln_b, sgu_w, sgu_b, w_out_odd,
           ln_mix_g, ln_mix_b, ffn_w_gate, ffn_w_up, ffn_w_down, ln_ffn_g, ln_ffn_b):
    bsz, seq, d = x.shape
    t = bsz * seq
    x2d = x.reshape(t, d)
    for l in range(DEPTH):
        j = l // 2
        if l % 2 == 0:
            (hproj,) = _proj(x2d, w_in_even[j].astype(BF16), [w_in_even.shape[-1]], [BF16])
            hproj = hproj.reshape(bsz, seq, -1)
            lam_init = 0.8 - 0.6 * math.exp(-0.3 * l)
            o_a = _diff_attention(hproj, lam_q1[j], lam_k1[j], lam_q2[j], lam_k2[j],
                                  diff_subln_g[j], lam_init)
            o_b = _band_attention(hproj, rel_bias[j])
            x2d = _outproj_ln(o_a.reshape(t, -1), o_b.reshape(t, -1), x2d,
                              w_out_even[j].astype(BF16), ln_mix_g[l], ln_mix_b[l])
        else:
            qkv, gz = _proj(x2d, w_in_odd[j].astype(BF16), [2 * C_QK_W + C_W, C_W + 2 * D_WIDTH],
                            [BF16, F32])
            qkv = qkv.reshape(bsz, seq, -1)
            gz = gz.reshape(bsz, seq, -1)
            o_c = _retention(qkv, gz, ret_gn_g[j], ret_gn_b[j])
            o_d = _spatial_gating(gz, sgu_w[j], sgu_b[j], sgu_ln_g[j], sgu_ln_b[j])
            x2d = _outproj_ln(o_c.reshape(t, -1), o_d.reshape(t, -1), x2d,
                              w_out_odd[j].astype(BF16), ln_mix_g[l], ln_mix_b[l])
        x2d = _ffn_ln(x2d, ffn_w_gate[l].astype(BF16), ffn_w_up[l].astype(BF16),
                      ffn_w_down[l].astype(BF16), ln_ffn_g[l], ln_ffn_b[l])
    return x2d.reshape(bsz, seq, d)
```

```python
import functools
import math

import jax
import jax.numpy as jnp
from jax import lax
from jax.experimental import pallas as pl
from jax.experimental.pallas import tpu as pltpu

F32 = jnp.float32
BF16 = jnp.bfloat16

D_MODEL = 1024
HEAD_DIM = 64
CHUNK = 64
A_HEADS = 4
A_W = 512
B_HEADS = 8
B_W = 512
B_LEFT_CHUNKS = 8
REL_CLIP = 128
C_HEADS = 4
C_QK_W = 256
C_W = 512
D_CHUNK = 128
D_GROUPS = 4
D_WIDTH = 512
DEPTH = 2
ALPHA = (2.0 * DEPTH) ** 0.25
LN_EPS = 1e-5
NEG_INF = -1e30
LOG2E = 1.4426950408889634

VMEM_LIMIT_V7X = 56 * 1024 * 1024
BF16_ROWS = 16

ROW_BLOCK = 512
ATT_BLOCK = 512
BAND_BLOCK = 256
BAND_WINDOW = BAND_BLOCK + B_LEFT_CHUNKS * CHUNK
RET_BLOCK = 256
SGU_ROWS = 512


def _params(semantics):
    return pltpu.CompilerParams(dimension_semantics=semantics, vmem_limit_bytes=VMEM_LIMIT_V7X)


def _const_spec(shape):
    nd = len(shape)
    return pl.BlockSpec(shape, lambda *_: (0,) * nd, pipeline_mode=pl.Buffered(1))


def _layer_norm(y, g, b):
    mu = jnp.mean(y, axis=-1, keepdims=True)
    d = y - mu
    var = jnp.mean(d * d, axis=-1, keepdims=True)
    return d * lax.rsqrt(var + LN_EPS) * g + b


def _proj_kernel(x_ref, w_ref, *out_refs):
    x = x_ref[...].astype(BF16)
    off = 0
    for o_ref in out_refs:
        n = o_ref.shape[-1]
        for c in range(0, n, 512):
            acc = jnp.dot(x, w_ref[:, off + c:off + c + 512], preferred_element_type=F32)
            o_ref[:, c:c + 512] = acc.astype(o_ref.dtype)
        off += n


def _proj(x2d, w_bf16, out_widths, out_dtypes):
    t, k = x2d.shape
    n = w_bf16.shape[1]
    assert sum(out_widths) == n and all(wd % 512 == 0 for wd in out_widths)
    return pl.pallas_call(
        _proj_kernel,
        grid=(t // ROW_BLOCK,),
        in_specs=[pl.BlockSpec((ROW_BLOCK, k), lambda i: (i, 0)), _const_spec((k, n))],
        out_specs=[pl.BlockSpec((ROW_BLOCK, wd), lambda i: (i, 0)) for wd in out_widths],
        out_shape=[jax.ShapeDtypeStruct((t, wd), dt) for wd, dt in zip(out_widths, out_dtypes)],
        compiler_params=_params(("parallel",)),
        name="in_proj",
    )(x2d, w_bf16)


def _outproj_ln_kernel(a_ref, b_ref, x_ref, w_ref, g_ref, beta_ref, o_ref):
    ka = a_ref.shape[-1]
    mix = jnp.dot(a_ref[...], w_ref[0:ka, :], preferred_element_type=F32)
    mix = mix + jnp.dot(b_ref[...], w_ref[ka:, :], preferred_element_type=F32)
    y = ALPHA * x_ref[...] + mix
    o_ref[...] = _layer_norm(y, g_ref[...], beta_ref[...])


def _outproj_ln(a, b, x2d, w_bf16, g, beta):
    t, d = x2d.shape
    ka, kb = a.shape[1], b.shape[1]
    return pl.pallas_call(
        _outproj_ln_kernel,
        grid=(t // ROW_BLOCK,),
        in_specs=[pl.BlockSpec((ROW_BLOCK, ka), lambda i: (i, 0)),
                  pl.BlockSpec((ROW_BLOCK, kb), lambda i: (i, 0)),
                  pl.BlockSpec((ROW_BLOCK, d), lambda i: (i, 0)),
                  _const_spec((ka + kb, d)), _const_spec((1, d)), _const_spec((1, d))],
        out_specs=pl.BlockSpec((ROW_BLOCK, d), lambda i: (i, 0)),
        out_shape=jax.ShapeDtypeStruct((t, d), F32),
        compiler_params=_params(("parallel",)),
        name="out_proj_ln",
    )(a, b, x2d, w_bf16, g.reshape(1, d), beta.reshape(1, d))


def _ffn_ln_kernel(x_ref, wg_ref, wu_ref, wd_ref, g_ref, beta_ref, o_ref):
    x = x_ref[...]
    xb = x.astype(BF16)
    gate = jnp.dot(xb, wg_ref[...], preferred_element_type=F32)
    up = jnp.dot(xb, wu_ref[...], preferred_element_type=F32)
    h = (gate * jax.nn.sigmoid(gate) * up).astype(BF16)
    y = ALPHA * x + jnp.dot(h, wd_ref[...], preferred_element_type=F32)
    o_ref[...] = _layer_norm(y, g_ref[...], beta_ref[...])


def _ffn_ln(x2d, wg, wu, wd, g, beta):
    t, d = x2d.shape
    f = wg.shape[1]
    return pl.pallas_call(
        _ffn_ln_kernel,
        grid=(t // ROW_BLOCK,),
        in_specs=[pl.BlockSpec((ROW_BLOCK, d), lambda i: (i, 0)),
                  _const_spec((d, f)), _const_spec((d, f)), _const_spec((f, d)),
                  _const_spec((1, d)), _const_spec((1, d))],
        out_specs=pl.BlockSpec((ROW_BLOCK, d), lambda i: (i, 0)),
        out_shape=jax.ShapeDtypeStruct((t, d), F32),
        compiler_params=_params(("parallel",)),
        name="ffn_ln",
    )(x2d, wg, wu, wd, g.reshape(1, d), beta.reshape(1, d))


def _diff_attn_kernel(q_ref, k_ref, vt_ref, past_ref, diag_ref, step_ref,
                      lq1_ref, lk1_ref, lq2_ref, lk2_ref, g_ref, o_ref,
                      qt_ref, sa_ref, sb_ref, m_ref, acc_ref, *, lam_init):
    t = ATT_BLOCK
    i = pl.program_id(2)
    scale = HEAD_DIM ** -0.5
    qt = q_ref[0].astype(F32).T * (scale * LOG2E)
    row = lax.broadcasted_iota(jnp.int32, qt.shape, 0)
    qt_ref[0] = jnp.where(row < HEAD_DIM, qt, 0.0).astype(BF16)
    qt_ref[1] = jnp.where(row >= HEAD_DIM, qt, 0.0).astype(BF16)

    m_ref[...] = jnp.full(m_ref.shape, NEG_INF, F32)
    acc_ref[...] = jnp.zeros(acc_ref.shape, F32)

    def scores(j, s_ref):
        start = pl.multiple_of(j * t, t)
        k_blk = k_ref[0, pl.ds(start, t), :]
        for mp in range(2):
            s_ref[mp] = jnp.dot(k_blk, qt_ref[mp], preferred_element_type=F32)

    def softmax_pv(j, s_ref, tile_ref):
        vt_blk = vt_ref[0, 0, j]
        step = step_ref[0]
        for mp in range(2):
            st = s_ref[mp] + tile_ref[0]
            m_prev = m_ref[mp] - step
            m_new = jnp.maximum(m_prev, jnp.max(st, axis=0, keepdims=True))
            alpha = jnp.exp2(m_prev - m_new)
            p = jnp.exp2(st - m_new)
            acc_ref[mp] = alpha * acc_ref[mp] + jnp.dot(vt_blk, p.astype(BF16),
                                                        preferred_element_type=F32)
            m_ref[mp] = m_new

    scores(0, sa_ref)

    def past_pair(jj, carry):
        j0 = 2 * jj
        scores(j0 + 1, sb_ref)
        softmax_pv(j0, sa_ref, past_ref)
        scores(j0 + 2, sa_ref)
        softmax_pv(j0 + 1, sb_ref, past_ref)
        return carry

    lax.fori_loop(0, lax.shift_right_logical(i, 1), past_pair, 0)

    @pl.when((i & 1) == 0)
    def _():
        softmax_pv(i, sa_ref, diag_ref)

    @pl.when((i & 1) == 1)
    def _():
        scores(i, sb_ref)
        softmax_pv(i - 1, sa_ref, past_ref)
        softmax_pv(i, sb_ref, diag_ref)

    lam = (jnp.exp(jnp.sum(lq1_ref[...] * lk1_ref[...], axis=-1, keepdims=True))
           - jnp.exp(jnp.sum(lq2_ref[...] * lk2_ref[...], axis=-1, keepdims=True)) + lam_init)
    vd = 2 * HEAD_DIM
    ot = (acc_ref[0, 0:vd] / acc_ref[0, vd:vd + 1]
          - lam * (acc_ref[1, 0:vd] / acc_ref[1, vd:vd + 1]))
    ms = jnp.mean(ot * ot, axis=0, keepdims=True)
    y = (ot * lax.rsqrt(ms + LN_EPS)).T * g_ref[...] * (1.0 - lam_init)
    o_ref[0] = y.astype(o_ref.dtype)


def _diff_attention(hproj, lam_q1, lam_k1, lam_q2, lam_k2, subln_g, lam_init):
    bsz, seq, _ = hproj.shape
    t = ATT_BLOCK
    nb = seq // t
    vd = 2 * HEAD_DIM
    vt = hproj[:, :, 2 * A_W:3 * A_W].reshape(bsz, nb, t, A_HEADS, vd).transpose(0, 3, 1, 4, 2)
    ones_tile = (jnp.arange(BF16_ROWS) == 0).astype(BF16)[:, None]
    vt = jnp.concatenate([vt, jnp.broadcast_to(ones_tile, vt.shape[:3] + (BF16_ROWS, t))], axis=3)
    slopes = 2.0 ** (-8.0 * jnp.arange(1, A_HEADS + 1, dtype=F32) / A_HEADS)
    kk = jnp.arange(t)[:, None]
    qq = jnp.arange(t)[None, :]
    rel = (qq - kk).astype(F32)
    past = -(slopes * LOG2E)[:, None, None] * rel[None]
    allowed = (kk // CHUNK) <= (qq // CHUNK)
    diag = jnp.where(allowed[None], -(slopes * LOG2E)[:, None, None] * jnp.abs(rel)[None], NEG_INF)
    step = jnp.broadcast_to((slopes * (LOG2E * t))[:, None, None], (A_HEADS, 1, t))
    vec = lambda v: v.reshape(1, HEAD_DIM).astype(F32)
    small = lambda shape: pl.BlockSpec(shape, lambda b, h, i: (0,) * len(shape))
    return pl.pallas_call(
        functools.partial(_diff_attn_kernel, lam_init=lam_init),
        grid=(bsz, A_HEADS, nb),
        in_specs=[pl.BlockSpec((1, t, vd), lambda b, h, i: (b, i, h)),
                  pl.BlockSpec((1, seq, vd), lambda b, h, i: (b, 0, A_HEADS + h)),
                  pl.BlockSpec((1, 1, nb, vd + BF16_ROWS, t), lambda b, h, i: (b, h, 0, 0, 0)),
                  pl.BlockSpec((1, t, t), lambda b, h, i: (h, 0, 0)),
                  pl.BlockSpec((1, t, t), lambda b, h, i: (h, 0, 0)),
                  pl.BlockSpec((1, 1, t), lambda b, h, i: (h, 0, 0)),
                  small((1, HEAD_DIM)), small((1, HEAD_DIM)), small((1, HEAD_DIM)), small((1, HEAD_DIM)),
                  small((1, vd))],
        out_specs=pl.BlockSpec((1, t, vd), lambda b, h, i: (b, i, h)),
        out_shape=jax.ShapeDtypeStruct((bsz, seq, A_W), BF16),
        scratch_shapes=[pltpu.VMEM((2, vd, t), BF16),
                        pltpu.VMEM((2, t, t), F32), pltpu.VMEM((2, t, t), F32),
                        pltpu.VMEM((2, 1, t), F32),
                        pltpu.VMEM((2, vd + BF16_ROWS, t), F32)],
        compiler_params=_params(("parallel", "parallel", "arbitrary")),
        name="diff_attn",
    )(hproj, hproj, vt, past, diag, step, vec(lam_q1), vec(lam_k1), vec(lam_q2), vec(lam_k2),
      subln_g.reshape(1, vd).astype(F32))


def _band_attn_kernel(q_ref, k_ref, v_ref, bias_ref, o_ref, vt_ref, sa_ref, sb_ref):
    t = BAND_BLOCK
    nb = q_ref.shape[1] // t
    n_win = BAND_WINDOW // t
    vd = 2 * HEAD_DIM
    scale = HEAD_DIM ** -0.5
    row = lax.broadcasted_iota(jnp.int32, (vd, t), 0)
    ones_tile = (lax.broadcasted_iota(jnp.int32, (BF16_ROWS, t), 0) == 0).astype(BF16)

    def rows(n):
        return pl.ds(pl.multiple_of(n * t, t), t)

    def transpose_values(n, carry):
        vt_ref[n, 0:vd] = v_ref[0, rows(n), :].astype(F32).T.astype(BF16)
        vt_ref[n, vd:vd + BF16_ROWS] = ones_tile
        return carry

    lax.fori_loop(0, nb, transpose_values, 0)

    def key_block(n, w):
        return jnp.maximum(n - (n_win - 1) + w, 0)

    def scores(n, s_ref):
        qt = q_ref[0, rows(n), :].astype(F32).T * (scale * LOG2E)
        for hh in range(2):
            in_head = (row < HEAD_DIM) if hh == 0 else (row >= HEAD_DIM)
            qh = jnp.where(in_head, qt, 0.0).astype(BF16)
            for w in range(n_win):
                s_ref[hh, w * t:(w + 1) * t] = jnp.dot(k_ref[0, rows(key_block(n, w)), :], qh,
                                                       preferred_element_type=F32)

    def softmax_pv(n, s_ref, mask_keys):
        outs = []
        for hh in range(2):
            st = s_ref[hh] + bias_ref[hh]
            if mask_keys:
                key_pos = (n - (n_win - 1)) * t + lax.broadcasted_iota(jnp.int32, (BAND_WINDOW, 1), 0)
                st = jnp.where(key_pos >= 0, st, NEG_INF)
            p = jnp.exp2(st - jnp.max(st, axis=0, keepdims=True)).astype(BF16)
            ot = jnp.dot(vt_ref[key_block(n, 0)], p[0:t], preferred_element_type=F32)
            for w in range(1, n_win):
                ot = ot + jnp.dot(vt_ref[key_block(n, w)], p[w * t:(w + 1) * t],
                                  preferred_element_type=F32)
            outs.append(ot[0:vd] / ot[vd:vd + 1])
        o_ref[0, rows(n), :] = jnp.where(row < HEAD_DIM, outs[0], outs[1]).T.astype(o_ref.dtype)

    for n in range(n_win - 1):
        scores(n, sa_ref)
        softmax_pv(n, sa_ref, True)

    first = n_win - 1
    scores(first, sa_ref)

    def pair(jj, carry):
        n0 = first + 2 * jj
        scores(n0 + 1, sb_ref)
        softmax_pv(n0, sa_ref, False)
        scores(n0 + 2, sa_ref)
        softmax_pv(n0 + 1, sb_ref, False)
        return carry

    n_pairs = (nb - first - 2) // 2
    lax.fori_loop(0, n_pairs, pair, 0)
    for n in range(first + 2 * n_pairs, nb):
        if n + 1 < nb:
            scores(n + 1, sb_ref)
        softmax_pv(n, sa_ref, False)
        sa_ref, sb_ref = sb_ref, sa_ref


def _band_attention(hproj, rel_bias):
    bsz, seq, _ = hproj.shape
    t = BAND_BLOCK
    c = jnp.arange(BAND_WINDOW)[:, None]
    r = jnp.arange(t)[None, :]
    in_band = ((c // CHUNK) >= (r // CHUNK)) & ((c // CHUNK) <= (r // CHUNK) + B_LEFT_CHUNKS)
    rb = rel_bias.astype(F32) * LOG2E
    period = t + BAND_WINDOW - 1
    off = B_LEFT_CHUNKS * CHUNK
    w_pos = jnp.broadcast_to(rb[:, -1:], (B_HEADS, t))
    n_lo = BAND_WINDOW - 1 - off - REL_CLIP
    w_neg = jnp.concatenate([jnp.broadcast_to(rb[:, :1], (B_HEADS, n_lo)), rb,
                             jnp.broadcast_to(rb[:, -1:], (B_HEADS, BAND_WINDOW - 1 - n_lo - rb.shape[1]))],
                            axis=1)
    w = jnp.concatenate([w_pos, w_neg], axis=1)
    sheared = jnp.tile(w, (1, BAND_WINDOW))[:, :BAND_WINDOW * (period - 1)]
    bias = sheared.reshape(B_HEADS, BAND_WINDOW, period - 1)[:, :, :t]
    bias = jnp.where(in_band[None], bias, NEG_INF)
    qcol, kcol, vcol = (3 * A_W) // 128, (3 * A_W + B_W) // 128, (3 * A_W + 2 * B_W) // 128

    vd = 2 * HEAD_DIM
    seq_spec = lambda col: pl.BlockSpec((1, seq, vd), lambda b, hp: (b, 0, col + hp))
    return pl.pallas_call(
        _band_attn_kernel,
        grid=(bsz, B_HEADS // 2),
        in_specs=[seq_spec(qcol), seq_spec(kcol), seq_spec(vcol),
                  pl.BlockSpec((2, BAND_WINDOW, t), lambda b, hp: (hp, 0, 0))],
        out_specs=pl.BlockSpec((1, seq, vd), lambda b, hp: (b, 0, hp)),
        out_shape=jax.ShapeDtypeStruct((bsz, seq, B_W), BF16),
        scratch_shapes=[pltpu.VMEM((seq // t, vd + BF16_ROWS, t), BF16),
                        pltpu.VMEM((2, BAND_WINDOW, t), F32), pltpu.VMEM((2, BAND_WINDOW, t), F32)],
        compiler_params=_params(("parallel", "parallel")),
        name="band_attn",
    )(hproj, hproj, hproj, bias)


def _retention_kernel(q_ref, k_ref, v_ref, gate_ref, dmask_ref, qdec_ref, kdec_ref, cdec_ref,
                      g_ref, beta_ref, o_ref, state_ref):
    @pl.when(pl.program_id(2) == 0)
    def _():
        state_ref[...] = jnp.zeros(state_ref.shape, F32)

    q = q_ref[0]
    k = k_ref[0]
    lane = lax.broadcasted_iota(jnp.int32, q.shape, 1)
    scale = HEAD_DIM ** -0.5
    vd = 2 * HEAD_DIM
    for hh in range(2):
        in_head = (lane < HEAD_DIM) if hh == 0 else (lane >= HEAD_DIM)
        qh = jnp.where(in_head, q, jnp.zeros_like(q))
        kh = jnp.where(in_head, k.astype(F32) * scale, 0.0)
        v = v_ref[0, :, hh * vd:(hh + 1) * vd]
        scores = lax.dot_general(qh, kh.astype(BF16), (((1,), (1,)), ((), ())),
                                 preferred_element_type=F32) * dmask_ref[hh]
        inner = jnp.dot(scores.astype(BF16), v, preferred_element_type=F32)
        state = state_ref[hh]
        cross = jnp.dot(qh, state.astype(BF16), preferred_element_type=F32) * qdec_ref[hh]
        kdt = (kh * kdec_ref[hh]).T.astype(BF16)
        kv = jnp.dot(kdt, v, preferred_element_type=F32)
        state_ref[hh] = cdec_ref[hh] * state + kv
        o = inner + cross
        mu = jnp.mean(o, axis=-1, keepdims=True)
        d = o - mu
        var = jnp.mean(d * d, axis=-1, keepdims=True)
        sl = slice(hh * vd, (hh + 1) * vd)
        y = d * lax.rsqrt(var + LN_EPS) * g_ref[:, sl] + beta_ref[:, sl]
        gate = gate_ref[0, :, sl]
        o_ref[0, :, sl] = (y * (gate * jax.nn.sigmoid(gate))).astype(o_ref.dtype)


def _retention(qkv, gz, gn_g, gn_b):
    bsz, seq, _ = qkv.shape
    ln = RET_BLOCK
    vd = 2 * HEAD_DIM
    log_gamma = jnp.log(1.0 - 2.0 ** (-5.0 - jnp.arange(C_HEADS, dtype=F32)))
    idx = jnp.arange(ln, dtype=F32)
    diff = idx[:, None] - idx[None, :]
    dmask = jnp.where(diff >= 0, jnp.exp(log_gamma[:, None, None] * jnp.maximum(diff, 0.0)), 0.0)
    qdec = jnp.broadcast_to(jnp.exp(log_gamma[:, None] * (idx + 1.0))[:, :, None], (C_HEADS, ln, vd))
    kdec = jnp.broadcast_to(jnp.exp(log_gamma[:, None] * (ln - 1.0 - idx))[:, :, None], (C_HEADS, ln, vd))
    cdec = jnp.broadcast_to(jnp.exp(log_gamma * ln)[:, None, None], (C_HEADS, 1, vd))
    pair = lambda shape: pl.BlockSpec(shape, lambda b, hp, c: (hp,) + (0,) * (len(shape) - 1))
    return pl.pallas_call(
        _retention_kernel,
        grid=(bsz, C_HEADS // 2, seq // ln),
        in_specs=[pl.BlockSpec((1, ln, 128), lambda b, hp, c: (b, c, hp)),
                  pl.BlockSpec((1, ln, 128), lambda b, hp, c: (b, c, C_QK_W // 128 + hp)),
                  pl.BlockSpec((1, ln, 2 * vd), lambda b, hp, c: (b, c, (2 * C_QK_W) // (2 * vd) + hp)),
                  pl.BlockSpec((1, ln, 2 * vd), lambda b, hp, c: (b, c, hp)),
                  pair((2, ln, ln)), pair((2, ln, vd)), pair((2, ln, vd)), pair((2, 1, vd)),
                  pl.BlockSpec((1, 2 * vd), lambda b, hp, c: (0, hp)),
                  pl.BlockSpec((1, 2 * vd), lambda b, hp, c: (0, hp))],
        out_specs=pl.BlockSpec((1, ln, 2 * vd), lambda b, hp, c: (b, c, hp)),
        out_shape=jax.ShapeDtypeStruct((bsz, seq, C_W), BF16),
        scratch_shapes=[pltpu.VMEM((2, vd, vd), F32)],
        compiler_params=_params(("parallel", "parallel", "arbitrary")),
        name="retention",
    )(qkv, qkv, qkv, gz, dmask, qdec, kdec, cdec, gn_g.reshape(1, C_W), gn_b.reshape(1, C_W))


def _sgu_kernel(u_ref, v_ref, w_ref, bt_ref, g_ref, beta_ref, o_ref):
    u = jax.nn.gelu(u_ref[0])
    v = _layer_norm(jax.nn.gelu(v_ref[0]), g_ref[...], beta_ref[...])
    gw = D_WIDTH // D_GROUPS
    r = lax.broadcasted_iota(jnp.int32, (D_CHUNK, D_CHUNK), 0)
    c = lax.broadcasted_iota(jnp.int32, (D_CHUNK, D_CHUNK), 1)
    for g in range(D_GROUPS):
        w = jnp.where(r >= c, w_ref[g], 0.0).astype(BF16)
        bias = bt_ref[:, g:g + 1]
        cols = slice(g * gw, (g + 1) * gw)
        vg = v[:, cols].astype(BF16)
        for n in range(SGU_ROWS // D_CHUNK):
            rows = slice(n * D_CHUNK, (n + 1) * D_CHUNK)
            mixed = jnp.dot(w, vg[rows], preferred_element_type=F32) + bias
            o_ref[0, rows, cols] = (u[rows, cols] * mixed).astype(o_ref.dtype)


def _spatial_gating(gz, w_s, b_s, ln_g, ln_b):
    bsz, seq, _ = gz.shape
    rws = SGU_ROWS
    small = lambda shape: pl.BlockSpec(shape, lambda b, i: (0,) * len(shape))
    return pl.pallas_call(
        _sgu_kernel,
        grid=(bsz, seq // rws),
        in_specs=[pl.BlockSpec((1, rws, D_WIDTH), lambda b, i: (b, i, 1)),
                  pl.BlockSpec((1, rws, D_WIDTH), lambda b, i: (b, i, 2)),
                  small((D_GROUPS, D_CHUNK, D_CHUNK)), small((D_CHUNK, D_GROUPS)),
                  small((1, D_WIDTH)), small((1, D_WIDTH))],
        out_specs=pl.BlockSpec((1, rws, D_WIDTH), lambda b, i: (b, i, 0)),
        out_shape=jax.ShapeDtypeStruct((bsz, seq, D_WIDTH), BF16),
        compiler_params=_params(("parallel", "parallel")),
        name="spatial_gating",
    )(gz, gz, w_s.astype(F32), b_s.T.astype(F32), ln_g.reshape(1, D_WIDTH), ln_b.reshape(1, D_WIDTH))


def kernel(x, w_in_even, lam_q1, lam_k1, lam_q2, lam_k2, diff_subln_g, rel_bias, w_out_even,
           w_in_odd, ret_gn_g, ret_gn_b, sgu_ln_g, sgu_ln_b, sgu_w, sgu_b, w_out_odd,
           ln_mix_g, ln_mix_b, ffn_w_gate, ffn_w_up, ffn_w_down, ln_ffn_g, ln_ffn_b):
    bsz, seq, d = x.shape
    t = bsz * seq
    x2d = x.reshape(t, d)
    for l in range(DEPTH):
        j = l // 2
        if l % 2 == 0:
            (hproj,) = _proj(x2d, w_in_even[j].astype(BF16), [w_in_even.shape[-1]], [BF16])
            hproj = hproj.reshape(bsz, seq, -1)
            lam_init = 0.8 - 0.6 * math.exp(-0.3 * l)
            o_a = _diff_attention(hproj, lam_q1[j], lam_k1[j], lam_q2[j], lam_k2[j],
                                  diff_subln_g[j], lam_init)
            o_b = _band_attention(hproj, rel_bias[j])
            x2d = _outproj_ln(o_a.reshape(t, -1), o_b.reshape(t, -1), x2d,
                              w_out_even[j].astype(BF16), ln_mix_g[l], ln_mix_b[l])
        else:
            qkv, gz = _proj(x2d, w_in_odd[j].astype(BF16), [2 * C_QK_W + C_W, C_W + 2 * D_WIDTH],
                            [BF16, F32])
            qkv = qkv.reshape(bsz, seq, -1)
            gz = gz.reshape(bsz, seq, -1)
            o_c = _retention(qkv, gz, ret_gn_g[j], ret_gn_b[j])
            o_d = _spatial_gating(gz, sgu_w[j], sgu_b[j], sgu_ln_g[j], sgu_ln_b[j])
            x2d = _outproj_ln(o_c.reshape(t, -1), o_d.reshape(t, -1), x2d,
                              w_out_odd[j].astype(BF16), ln_mix_g[l], ln_mix_b[l])
        x2d = _ffn_ln(x2d, ffn_w_gate[l].astype(BF16), ffn_w_up[l].astype(BF16),
                      ffn_w_down[l].astype(BF16), ln_ffn_g[l], ln_ffn_b[l])
    return x2d.reshape(bsz, seq, d)
```

```python
import functools
import math

import jax
import jax.numpy as jnp
from jax import lax
from jax.experimental import pallas as pl
from jax.experimental.pallas import tpu as pltpu

F32 = jnp.float32
BF16 = jnp.bfloat16

D_MODEL = 1024
HEAD_DIM = 64
CHUNK = 64
A_HEADS = 4
A_W = 512
B_HEADS = 8
B_W = 512
B_LEFT_CHUNKS = 8
REL_CLIP = 128
C_HEADS = 4
C_QK_W = 256
C_W = 512
D_CHUNK = 128
D_GROUPS = 4
D_WIDTH = 512
DEPTH = 2
ALPHA = (2.0 * DEPTH) ** 0.25
LN_EPS = 1e-5
NEG_INF = -1e30
LOG2E = 1.4426950408889634

VMEM_LIMIT_V7X = 56 * 1024 * 1024
BF16_ROWS = 16

ROW_BLOCK = 512
ATT_BLOCK = 512
BAND_BLOCK = 256
BAND_WINDOW = BAND_BLOCK + B_LEFT_CHUNKS * CHUNK
RET_BLOCK = 256
SGU_ROWS = 512


def _params(semantics):
    return pltpu.CompilerParams(dimension_semantics=semantics, vmem_limit_bytes=VMEM_LIMIT_V7X)


def _const_spec(shape):
    nd = len(shape)
    return pl.BlockSpec(shape, lambda *_: (0,) * nd, pipeline_mode=pl.Buffered(1))


def _layer_norm(y, g, b):
    mu = jnp.mean(y, axis=-1, keepdims=True)
    d = y - mu
    var = jnp.mean(d * d, axis=-1, keepdims=True)
    return d * lax.rsqrt(var + LN_EPS) * g + b


def _proj_kernel(x_ref, w_ref, *out_refs):
    x = x_ref[...].astype(BF16)
    off = 0
    for o_ref in out_refs:
        n = o_ref.shape[-1]
        for c in range(0, n, 512):
            acc = jnp.dot(x, w_ref[:, off + c:off + c + 512], preferred_element_type=F32)
            o_ref[:, c:c + 512] = acc.astype(o_ref.dtype)
        off += n


def _proj(x2d, w_bf16, out_widths, out_dtypes):
    t, k = x2d.shape
    n = w_bf16.shape[1]
    assert sum(out_widths) == n and all(wd % 512 == 0 for wd in out_widths)
    return pl.pallas_call(
        _proj_kernel,
        grid=(t // ROW_BLOCK,),
        in_specs=[pl.BlockSpec((ROW_BLOCK, k), lambda i: (i, 0)), _const_spec((k, n))],
        out_specs=[pl.BlockSpec((ROW_BLOCK, wd), lambda i: (i, 0)) for wd in out_widths],
        out_shape=[jax.ShapeDtypeStruct((t, wd), dt) for wd, dt in zip(out_widths, out_dtypes)],
        compiler_params=_params(("parallel",)),
        name="in_proj",
    )(x2d, w_bf16)


def _transposed_value_tiles(vals, block):
    ones_tile = (lax.broadcasted_iota(jnp.int32, (BF16_ROWS, block), 0) == 0).astype(BF16)
    return [jnp.concatenate([vals[r:r + block].T.astype(BF16), ones_tile], axis=0)
            for r in range(0, vals.shape[0], block)]


def _proj_even_kernel(x_ref, w_ref, h_ref, vta_ref, vtb_ref):
    x = x_ref[...].astype(BF16)
    vd = 2 * HEAD_DIM
    dot = lambda c: jnp.dot(x, w_ref[:, c:c + 512], preferred_element_type=F32)
    h_ref[:, 0:512] = dot(0).astype(BF16)
    h_ref[:, 512:1024] = dot(A_W).astype(BF16)
    h_ref[:, 1024:1536] = dot(3 * A_W).astype(BF16)
    h_ref[:, 1536:2048] = dot(3 * A_W + B_W).astype(BF16)
    va = dot(2 * A_W)
    for h in range(A_HEADS):
        vta_ref[0, h, 0] = _transposed_value_tiles(va[:, h * vd:(h + 1) * vd], ATT_BLOCK)[0]
    vb = dot(3 * A_W + 2 * B_W)
    for hp in range(B_HEADS // 2):
        tiles = _transposed_value_tiles(vb[:, hp * vd:(hp + 1) * vd], BAND_BLOCK)
        for n, tile in enumerate(tiles):
            vtb_ref[0, hp, n] = tile


def _proj_even(x2d, w_bf16, bsz, seq):
    t, k = x2d.shape
    n = w_bf16.shape[1]
    assert ROW_BLOCK == ATT_BLOCK and seq % ROW_BLOCK == 0
    per_seq = seq // ROW_BLOCK
    vrows = 2 * HEAD_DIM + BF16_ROWS
    return pl.pallas_call(
        _proj_even_kernel,
        grid=(t // ROW_BLOCK,),
        in_specs=[pl.BlockSpec((ROW_BLOCK, k), lambda i: (i, 0)), _const_spec((k, n))],
        out_specs=[pl.BlockSpec((ROW_BLOCK, 2 * A_W + 2 * B_W), lambda i: (i, 0)),
                   pl.BlockSpec((1, A_HEADS, 1, vrows, ATT_BLOCK),
                                lambda i: (i // per_seq, 0, i % per_seq, 0, 0)),
                   pl.BlockSpec((1, B_HEADS // 2, ROW_BLOCK // BAND_BLOCK, vrows, BAND_BLOCK),
                                lambda i: (i // per_seq, 0, i % per_seq, 0, 0))],
        out_shape=[jax.ShapeDtypeStruct((t, 2 * A_W + 2 * B_W), BF16),
                   jax.ShapeDtypeStruct((bsz, A_HEADS, seq // ATT_BLOCK, vrows, ATT_BLOCK), BF16),
                   jax.ShapeDtypeStruct((bsz, B_HEADS // 2, seq // BAND_BLOCK, vrows, BAND_BLOCK), BF16)],
        compiler_params=_params(("parallel",)),
        name="in_proj_even",
    )(x2d, w_bf16)


def _mix_ffn_kernel(a_ref, b_ref, x_ref, wo_ref, g1_ref, b1_ref, wg_ref, wu_ref, wd_ref,
                    g2_ref, b2_ref, o_ref):
    ka = a_ref.shape[-1]
    mix = jnp.dot(a_ref[...], wo_ref[0:ka, :], preferred_element_type=F32)
    mix = mix + jnp.dot(b_ref[...], wo_ref[ka:, :], preferred_element_type=F32)
    x1 = _layer_norm(ALPHA * x_ref[...] + mix, g1_ref[...], b1_ref[...])
    xb = x1.astype(BF16)
    gate = jnp.dot(xb, wg_ref[...], preferred_element_type=F32)
    up = jnp.dot(xb, wu_ref[...], preferred_element_type=F32)
    h = (gate * jax.nn.sigmoid(gate) * up).astype(BF16)
    y = ALPHA * x1 + jnp.dot(h, wd_ref[...], preferred_element_type=F32)
    o_ref[...] = _layer_norm(y, g2_ref[...], b2_ref[...])


def _mix_ffn(a, b, x2d, wo, g1, b1, wg, wu, wd, g2, b2):
    t, d = x2d.shape
    ka, kb = a.shape[1], b.shape[1]
    f = wg.shape[1]
    vec = lambda v: v.reshape(1, d)
    return pl.pallas_call(
        _mix_ffn_kernel,
        grid=(t // ROW_BLOCK,),
        in_specs=[pl.BlockSpec((ROW_BLOCK, ka), lambda i: (i, 0)),
                  pl.BlockSpec((ROW_BLOCK, kb), lambda i: (i, 0)),
                  pl.BlockSpec((ROW_BLOCK, d), lambda i: (i, 0)),
                  _const_spec((ka + kb, d)), _const_spec((1, d)), _const_spec((1, d)),
                  _const_spec((d, f)), _const_spec((d, f)), _const_spec((f, d)),
                  _const_spec((1, d)), _const_spec((1, d))],
        out_specs=pl.BlockSpec((ROW_BLOCK, d), lambda i: (i, 0)),
        out_shape=jax.ShapeDtypeStruct((t, d), F32),
        compiler_params=_params(("parallel",)),
        name="mix_ffn",
    )(a, b, x2d, wo, vec(g1), vec(b1), wg, wu, wd, vec(g2), vec(b2))


def _diff_attn_kernel(q_ref, k_ref, vt_ref, past_ref, diag_ref, past_max_ref, diag_max_ref, step_ref,
                      lq1_ref, lk1_ref, lq2_ref, lk2_ref, g_ref, o_ref,
                      qt_ref, sa_ref, sb_ref, ra_ref, rb_ref, m_ref, acc_ref, *, lam_init):
    t = ATT_BLOCK
    n_chunks = t // CHUNK
    i = pl.program_id(2)
    scale = HEAD_DIM ** -0.5
    qt = q_ref[0].astype(F32).T * (scale * LOG2E)
    row = lax.broadcasted_iota(jnp.int32, qt.shape, 0)
    qt_ref[0] = jnp.where(row < HEAD_DIM, qt, 0.0).astype(BF16)
    qt_ref[1] = jnp.where(row >= HEAD_DIM, qt, 0.0).astype(BF16)

    m_ref[...] = jnp.full(m_ref.shape, NEG_INF, F32)
    acc_ref[...] = jnp.zeros(acc_ref.shape, F32)

    def scores(j, s_ref, r_ref):
        start = pl.multiple_of(j * t, t)
        k_blk = k_ref[0, pl.ds(start, t), :]
        for mp in range(2):
            s = jnp.dot(k_blk, qt_ref[mp], preferred_element_type=F32)
            s_ref[mp] = s
            r_ref[mp] = jnp.max(s.reshape(n_chunks, CHUNK // 8, 8, t), axis=1)

    def softmax_pv(j, s_ref, r_ref, tile_ref, tile_max_ref):
        vt_blk = vt_ref[0, 0, j]
        step = step_ref[0]
        for mp in range(2):
            m_prev = m_ref[mp] - step
            bound = jnp.max(r_ref[mp] + tile_max_ref[0][:, None, :], axis=0)
            bound = jnp.max(bound, axis=0, keepdims=True)
            m_new = jnp.maximum(m_prev, bound)
            alpha = jnp.exp2(m_prev - m_new)
            p = jnp.exp2(s_ref[mp] + tile_ref[0] - m_new)
            acc_ref[mp] = alpha * acc_ref[mp] + jnp.dot(vt_blk, p.astype(BF16),
                                                        preferred_element_type=F32)
            m_ref[mp] = m_new

    scores(0, sa_ref, ra_ref)

    def past_pair(jj, carry):
        j0 = 2 * jj
        scores(j0 + 1, sb_ref, rb_ref)
        softmax_pv(j0, sa_ref, ra_ref, past_ref, past_max_ref)
        scores(j0 + 2, sa_ref, ra_ref)
        softmax_pv(j0 + 1, sb_ref, rb_ref, past_ref, past_max_ref)
        return carry

    lax.fori_loop(0, lax.shift_right_logical(i, 1), past_pair, 0)

    @pl.when((i & 1) == 0)
    def _():
        softmax_pv(i, sa_ref, ra_ref, diag_ref, diag_max_ref)

    @pl.when((i & 1) == 1)
    def _():
        scores(i, sb_ref, rb_ref)
        softmax_pv(i - 1, sa_ref, ra_ref, past_ref, past_max_ref)
        softmax_pv(i, sb_ref, rb_ref, diag_ref, diag_max_ref)

    lam = (jnp.exp(jnp.sum(lq1_ref[...] * lk1_ref[...], axis=-1, keepdims=True))
           - jnp.exp(jnp.sum(lq2_ref[...] * lk2_ref[...], axis=-1, keepdims=True)) + lam_init)
    vd = 2 * HEAD_DIM
    ot = (acc_ref[0, 0:vd] / acc_ref[0, vd:vd + 1]
          - lam * (acc_ref[1, 0:vd] / acc_ref[1, vd:vd + 1]))
    ms = jnp.mean(ot * ot, axis=0, keepdims=True)
    y = (ot * lax.rsqrt(ms + LN_EPS)).T * g_ref[...] * (1.0 - lam_init)
    o_ref[0] = y.astype(o_ref.dtype)


def _diff_attention(hproj, vt, lam_q1, lam_k1, lam_q2, lam_k2, subln_g, lam_init):
    bsz, seq, _ = hproj.shape
    t = ATT_BLOCK
    nb = seq // t
    n_chunks = t // CHUNK
    vd = 2 * HEAD_DIM
    slopes = 2.0 ** (-8.0 * jnp.arange(1, A_HEADS + 1, dtype=F32) / A_HEADS)
    kk = jnp.arange(t)[:, None]
    qq = jnp.arange(t)[None, :]
    rel = (qq - kk).astype(F32)
    past = -(slopes * LOG2E)[:, None, None] * rel[None]
    allowed = (kk // CHUNK) <= (qq // CHUNK)
    diag = jnp.where(allowed[None], -(slopes * LOG2E)[:, None, None] * jnp.abs(rel)[None], NEG_INF)
    chunk_max = lambda tile: jnp.max(tile.reshape(A_HEADS, n_chunks, CHUNK, t), axis=2)
    step = jnp.broadcast_to((slopes * (LOG2E * t))[:, None, None], (A_HEADS, 1, t))
    vec = lambda v: v.reshape(1, HEAD_DIM).astype(F32)
    small = lambda shape: pl.BlockSpec(shape, lambda b, h, i: (0,) * len(shape))
    per_head = lambda shape: pl.BlockSpec(shape, lambda b, h, i: (h,) + (0,) * (len(shape) - 1))
    return pl.pallas_call(
        functools.partial(_diff_attn_kernel, lam_init=lam_init),
        grid=(bsz, A_HEADS, nb),
        in_specs=[pl.BlockSpec((1, t, vd), lambda b, h, i: (b, i, h)),
                  pl.BlockSpec((1, seq, vd), lambda b, h, i: (b, 0, A_HEADS + h)),
                  pl.BlockSpec((1, 1, nb, vd + BF16_ROWS, t), lambda b, h, i: (b, h, 0, 0, 0)),
                  per_head((1, t, t)), per_head((1, t, t)),
                  per_head((1, n_chunks, t)), per_head((1, n_chunks, t)), per_head((1, 1, t)),
                  small((1, HEAD_DIM)), small((1, HEAD_DIM)), small((1, HEAD_DIM)), small((1, HEAD_DIM)),
                  small((1, vd))],
        out_specs=pl.BlockSpec((1, t, vd), lambda b, h, i: (b, i, h)),
        out_shape=jax.ShapeDtypeStruct((bsz, seq, A_W), BF16),
        scratch_shapes=[pltpu.VMEM((2, vd, t), BF16),
                        pltpu.VMEM((2, t, t), F32), pltpu.VMEM((2, t, t), F32),
                        pltpu.VMEM((2, n_chunks, 8, t), F32), pltpu.VMEM((2, n_chunks, 8, t), F32),
                        pltpu.VMEM((2, 1, t), F32),
                        pltpu.VMEM((2, vd + BF16_ROWS, t), F32)],
        compiler_params=_params(("parallel", "parallel", "arbitrary")),
        name="diff_attn",
    )(hproj, hproj, vt, past, diag, chunk_max(past), chunk_max(diag), step,
      vec(lam_q1), vec(lam_k1), vec(lam_q2), vec(lam_k2), subln_g.reshape(1, vd).astype(F32))


def _band_attn_kernel(q_ref, k_ref, vt_ref, bias_ref, o_ref, sa_ref, sb_ref):
    t = BAND_BLOCK
    nb = q_ref.shape[1] // t
    n_win = BAND_WINDOW // t
    vd = 2 * HEAD_DIM
    scale = HEAD_DIM ** -0.5
    row = lax.broadcasted_iota(jnp.int32, (vd, t), 0)

    def rows(n):
        return pl.ds(pl.multiple_of(n * t, t), t)

    def key_block(n, w):
        return jnp.maximum(n - (n_win - 1) + w, 0)

    def scores(n, s_ref):
        qt = q_ref[0, rows(n), :].astype(F32).T * (scale * LOG2E)
        for hh in range(2):
            in_head = (row < HEAD_DIM) if hh == 0 else (row >= HEAD_DIM)
            qh = jnp.where(in_head, qt, 0.0).astype(BF16)
            for w in range(n_win):
                s_ref[hh, w * t:(w + 1) * t] = jnp.dot(k_ref[0, rows(key_block(n, w)), :], qh,
                                                       preferred_element_type=F32)

    def softmax_pv(n, s_ref, mask_keys):
        outs = []
        for hh in range(2):
            st = s_ref[hh] + bias_ref[hh]
            if mask_keys:
                key_pos = (n - (n_win - 1)) * t + lax.broadcasted_iota(jnp.int32, (BAND_WINDOW, 1), 0)
                st = jnp.where(key_pos >= 0, st, NEG_INF)
            p = jnp.exp2(st - jnp.max(st, axis=0, keepdims=True)).astype(BF16)
            ot = jnp.dot(vt_ref[0, 0, key_block(n, 0)], p[0:t], preferred_element_type=F32)
            for w in range(1, n_win):
                ot = ot + jnp.dot(vt_ref[0, 0, key_block(n, w)], p[w * t:(w + 1) * t],
                                  preferred_element_type=F32)
            outs.append(ot[0:vd] / ot[vd:vd + 1])
        o_ref[0, rows(n), :] = jnp.where(row < HEAD_DIM, outs[0], outs[1]).T.astype(o_ref.dtype)

    for n in range(n_win - 1):
        scores(n, sa_ref)
        softmax_pv(n, sa_ref, True)

    first = n_win - 1
    scores(first, sa_ref)

    def pair(jj, carry):
        n0 = first + 2 * jj
        scores(n0 + 1, sb_ref)
        softmax_pv(n0, sa_ref, False)
        scores(n0 + 2, sa_ref)
        softmax_pv(n0 + 1, sb_ref, False)
        return carry

    n_pairs = (nb - first - 2) // 2
    lax.fori_loop(0, n_pairs, pair, 0)
    for n in range(first + 2 * n_pairs, nb):
        if n + 1 < nb:
            scores(n + 1, sb_ref)
        softmax_pv(n, sa_ref, False)
        sa_ref, sb_ref = sb_ref, sa_ref


def _band_attention(hproj, vt, rel_bias):
    bsz, seq, _ = hproj.shape
    t = BAND_BLOCK
    vd = 2 * HEAD_DIM
    c = jnp.arange(BAND_WINDOW)[:, None]
    r = jnp.arange(t)[None, :]
    in_band = ((c // CHUNK) >= (r // CHUNK)) & ((c // CHUNK) <= (r // CHUNK) + B_LEFT_CHUNKS)
    rb = rel_bias.astype(F32) * LOG2E
    period = t + BAND_WINDOW - 1
    off = B_LEFT_CHUNKS * CHUNK
    w_pos = jnp.broadcast_to(rb[:, -1:], (B_HEADS, t))
    n_lo = BAND_WINDOW - 1 - off - REL_CLIP
    w_neg = jnp.concatenate([jnp.broadcast_to(rb[:, :1], (B_HEADS, n_lo)), rb,
                             jnp.broadcast_to(rb[:, -1:], (B_HEADS, BAND_WINDOW - 1 - n_lo - rb.shape[1]))],
                            axis=1)
    w = jnp.concatenate([w_pos, w_neg], axis=1)
    repeated = jnp.broadcast_to(w[:, None, :], (B_HEADS, BAND_WINDOW, period)).reshape(B_HEADS, -1)
    sheared = repeated[:, :BAND_WINDOW * (period - 1)].reshape(B_HEADS, BAND_WINDOW, period - 1)
    bias = sheared[:, :, :t]
    bias = jnp.where(in_band[None], bias, NEG_INF)
    qcol, kcol = (2 * A_W) // vd, (2 * A_W + B_W) // vd
    seq_spec = lambda col: pl.BlockSpec((1, seq, vd), lambda b, hp: (b, 0, col + hp))
    return pl.pallas_call(
        _band_attn_kernel,
        grid=(bsz, B_HEADS // 2),
        in_specs=[seq_spec(qcol), seq_spec(kcol),
                  pl.BlockSpec((1, 1, seq // t, vd + BF16_ROWS, t), lambda b, hp: (b, hp, 0, 0, 0)),
                  pl.BlockSpec((2, BAND_WINDOW, t), lambda b, hp: (hp, 0, 0))],
        out_specs=pl.BlockSpec((1, seq, vd), lambda b, hp: (b, 0, hp)),
        out_shape=jax.ShapeDtypeStruct((bsz, seq, B_W), BF16),
        scratch_shapes=[pltpu.VMEM((2, BAND_WINDOW, t), F32), pltpu.VMEM((2, BAND_WINDOW, t), F32)],
        compiler_params=_params(("parallel", "parallel")),
        name="band_attn",
    )(hproj, hproj, vt, bias)


def _retention_kernel(q_ref, k_ref, v_ref, gate_ref, dmask_ref, qdec_ref, kdec_ref, cdec_ref,
                      g_ref, beta_ref, o_ref, state_ref):
    @pl.when(pl.program_id(2) == 0)
    def _():
        state_ref[...] = jnp.zeros(state_ref.shape, F32)

    q = q_ref[0]
    k = k_ref[0]
    lane = lax.broadcasted_iota(jnp.int32, q.shape, 1)
    scale = HEAD_DIM ** -0.5
    vd = 2 * HEAD_DIM
    for hh in range(2):
        in_head = (lane < HEAD_DIM) if hh == 0 else (lane >= HEAD_DIM)
        qh = jnp.where(in_head, q, jnp.zeros_like(q))
        kh = jnp.where(in_head, k.astype(F32) * scale, 0.0)
        v = v_ref[0, :, hh * vd:(hh + 1) * vd]
        scores = lax.dot_general(qh, kh.astype(BF16), (((1,), (1,)), ((), ())),
                                 preferred_element_type=F32) * dmask_ref[hh]
        inner = jnp.dot(scores.astype(BF16), v, preferred_element_type=F32)
        state = state_ref[hh]
        cross = jnp.dot(qh, state.astype(BF16), preferred_element_type=F32) * qdec_ref[hh]
        kdt = (kh * kdec_ref[hh]).T.astype(BF16)
        kv = jnp.dot(kdt, v, preferred_element_type=F32)
        state_ref[hh] = cdec_ref[hh] * state + kv
        o = inner + cross
        mu = jnp.mean(o, axis=-1, keepdims=True)
        d = o - mu
        var = jnp.mean(d * d, axis=-1, keepdims=True)
        sl = slice(hh * vd, (hh + 1) * vd)
        y = d * lax.rsqrt(var + LN_EPS) * g_ref[:, sl] + beta_ref[:, sl]
        gate = gate_ref[0, :, sl]
        o_ref[0, :, sl] = (y * (gate * jax.nn.sigmoid(gate))).astype(o_ref.dtype)


def _retention(qkv, gz, gn_g, gn_b):
    bsz, seq, _ = qkv.shape
    ln = RET_BLOCK
    vd = 2 * HEAD_DIM
    log_gamma = jnp.log(1.0 - 2.0 ** (-5.0 - jnp.arange(C_HEADS, dtype=F32)))
    idx = jnp.arange(ln, dtype=F32)
    diff = idx[:, None] - idx[None, :]
    dmask = jnp.where(diff >= 0, jnp.exp(log_gamma[:, None, None] * jnp.maximum(diff, 0.0)), 0.0)
    qdec = jnp.broadcast_to(jnp.exp(log_gamma[:, None] * (idx + 1.0))[:, :, None], (C_HEADS, ln, vd))
    kdec = jnp.broadcast_to(jnp.exp(log_gamma[:, None] * (ln - 1.0 - idx))[:, :, None], (C_HEADS, ln, vd))
    cdec = jnp.broadcast_to(jnp.exp(log_gamma * ln)[:, None, None], (C_HEADS, 1, vd))
    pair = lambda shape: pl.BlockSpec(shape, lambda b, hp, c: (hp,) + (0,) * (len(shape) - 1))
    return pl.pallas_call(
        _retention_kernel,
        grid=(bsz, C_HEADS // 2, seq // ln),
        in_specs=[pl.BlockSpec((1, ln, 128), lambda b, hp, c: (b, c, hp)),
                  pl.BlockSpec((1, ln, 128), lambda b, hp, c: (b, c, C_QK_W // 128 + hp)),
                  pl.BlockSpec((1, ln, 2 * vd), lambda b, hp, c: (b, c, (2 * C_QK_W) // (2 * vd) + hp)),
                  pl.BlockSpec((1, ln, 2 * vd), lambda b, hp, c: (b, c, hp)),
                  pair((2, ln, ln)), pair((2, ln, vd)), pair((2, ln, vd)), pair((2, 1, vd)),
                  pl.BlockSpec((1, 2 * vd), lambda b, hp, c: (0, hp)),
                  pl.BlockSpec((1, 2 * vd), lambda b, hp, c: (0, hp))],
        out_specs=pl.BlockSpec((1, ln, 2 * vd), lambda b, hp, c: (b, c, hp)),
        out_shape=jax.ShapeDtypeStruct((bsz, seq, C_W), BF16),
        scratch_shapes=[pltpu.VMEM((2, vd, vd), F32)],
        compiler_params=_params(("parallel", "parallel", "arbitrary")),
        name="retention",
    )(qkv, qkv, qkv, gz, dmask, qdec, kdec, cdec, gn_g.reshape(1, C_W), gn_b.reshape(1, C_W))


def _sgu_kernel(u_ref, v_ref, w_ref, bt_ref, g_ref, beta_ref, o_ref):
    u = jax.nn.gelu(u_ref[0])
    v = _layer_norm(jax.nn.gelu(v_ref[0]), g_ref[...], beta_ref[...])
    gw = D_WIDTH // D_GROUPS
    r = lax.broadcasted_iota(jnp.int32, (D_CHUNK, D_CHUNK), 0)
    c = lax.broadcasted_iota(jnp.int32, (D_CHUNK, D_CHUNK), 1)
    for g in range(D_GROUPS):
        w = jnp.where(r >= c, w_ref[g], 0.0).astype(BF16)
        bias = bt_ref[:, g:g + 1]
        cols = slice(g * gw, (g + 1) * gw)
        vg = v[:, cols].astype(BF16)
        for n in range(SGU_ROWS // D_CHUNK):
            rows = slice(n * D_CHUNK, (n + 1) * D_CHUNK)
            mixed = jnp.dot(w, vg[rows], preferred_element_type=F32) + bias
            o_ref[0, rows, cols] = (u[rows, cols] * mixed).astype(o_ref.dtype)


def _spatial_gating(gz, w_s, b_s, ln_g, ln_b):
    bsz, seq, _ = gz.shape
    rws = SGU_ROWS
    small = lambda shape: pl.BlockSpec(shape, lambda b, i: (0,) * len(shape))
    return pl.pallas_call(
        _sgu_kernel,
        grid=(bsz, seq // rws),
        in_specs=[pl.BlockSpec((1, rws, D_WIDTH), lambda b, i: (b, i, 1)),
                  pl.BlockSpec((1, rws, D_WIDTH), lambda b, i: (b, i, 2)),
                  small((D_GROUPS, D_CHUNK, D_CHUNK)), small((D_CHUNK, D_GROUPS)),
                  small((1, D_WIDTH)), small((1, D_WIDTH))],
        out_specs=pl.BlockSpec((1, rws, D_WIDTH), lambda b, i: (b, i, 0)),
        out_shape=jax.ShapeDtypeStruct((bsz, seq, D_WIDTH), BF16),
        compiler_params=_params(("parallel", "parallel")),
        name="spatial_gating",
    )(gz, gz, w_s.astype(F32), b_s.T.astype(F32), ln_g.reshape(1, D_WIDTH), ln_b.reshape(1, D_WIDTH))


def kernel(x, w_in_even, lam_q1, lam_k1, lam_q2, lam_k2, diff_subln_g, rel_bias, w_out_even,
           w_in_odd, ret_gn_g, ret_gn_b, sgu_ln_g, sgu_ln_b, sgu_w, sgu_b, w_out_odd,
           ln_mix_g, ln_mix_b, ffn_w_gate, ffn_w_up, ffn_w_down, ln_ffn_g, ln_ffn_b):
    bsz, seq, d = x.shape
    t = bsz * seq
    x2d = x.reshape(t, d)
    for l in range(DEPTH):
        j = l // 2
        if l % 2 == 0:
            hproj, vta, vtb = _proj_even(x2d, w_in_even[j].astype(BF16), bsz, seq)
            hproj = hproj.reshape(bsz, seq, -1)
            lam_init = 0.8 - 0.6 * math.exp(-0.3 * l)
            mix_a = _diff_attention(hproj, vta, lam_q1[j], lam_k1[j], lam_q2[j], lam_k2[j],
                                    diff_subln_g[j], lam_init)
            mix_b = _band_attention(hproj, vtb, rel_bias[j])
            w_out = w_out_even[j]
        else:
            qkv, gz = _proj(x2d, w_in_odd[j].astype(BF16), [2 * C_QK_W + C_W, C_W + 2 * D_WIDTH],
                            [BF16, F32])
            qkv = qkv.reshape(bsz, seq, -1)
            gz = gz.reshape(bsz, seq, -1)
            mix_a = _retention(qkv, gz, ret_gn_g[j], ret_gn_b[j])
            mix_b = _spatial_gating(gz, sgu_w[j], sgu_b[j], sgu_ln_g[j], sgu_ln_b[j])
            w_out = w_out_odd[j]
        x2d = _mix_ffn(mix_a.reshape(t, -1), mix_b.reshape(t, -1), x2d, w_out.astype(BF16),
                       ln_mix_g[l], ln_mix_b[l], ffn_w_gate[l].astype(BF16), ffn_w_up[l].astype(BF16),
                       ffn_w_down[l].astype(BF16), ln_ffn_g[l], ln_ffn_b[l])
    return x2d.reshape(bsz, seq, d)
```

```python
import functools
import math

import jax
import jax.numpy as jnp
from jax import lax
from jax.experimental import pallas as pl
from jax.experimental.pallas import tpu as pltpu

F32 = jnp.float32
BF16 = jnp.bfloat16

D_MODEL = 1024
HEAD_DIM = 64
CHUNK = 64
A_HEADS = 4
A_W = 512
B_HEADS = 8
B_W = 512
B_LEFT_CHUNKS = 8
REL_CLIP = 128
C_HEADS = 4
C_QK_W = 256
C_W = 512
D_CHUNK = 128
D_GROUPS = 4
D_WIDTH = 512
DEPTH = 2
ALPHA = (2.0 * DEPTH) ** 0.25
LN_EPS = 1e-5
NEG_INF = -1e30
LOG2E = 1.4426950408889634

VMEM_LIMIT_V7X = 56 * 1024 * 1024
BF16_ROWS = 16

ROW_BLOCK = 512
ATT_BLOCK = 512
BAND_BLOCK = 256
BAND_WINDOW = BAND_BLOCK + B_LEFT_CHUNKS * CHUNK
RET_BLOCK = 256
SGU_ROWS = 512


def _params(semantics):
    return pltpu.CompilerParams(dimension_semantics=semantics, vmem_limit_bytes=VMEM_LIMIT_V7X)


def _const_spec(shape):
    nd = len(shape)
    return pl.BlockSpec(shape, lambda *_: (0,) * nd, pipeline_mode=pl.Buffered(1))


def _layer_norm(y, g, b):
    mu = jnp.mean(y, axis=-1, keepdims=True)
    d = y - mu
    var = jnp.mean(d * d, axis=-1, keepdims=True)
    return d * lax.rsqrt(var + LN_EPS) * g + b


def _proj_kernel(x_ref, w_ref, *out_refs):
    x = x_ref[...].astype(BF16)
    off = 0
    for o_ref in out_refs:
        n = o_ref.shape[-1]
        for c in range(0, n, 512):
            acc = jnp.dot(x, w_ref[:, off + c:off + c + 512], preferred_element_type=F32)
            o_ref[:, c:c + 512] = acc.astype(o_ref.dtype)
        off += n


def _proj(x2d, w_bf16, out_widths, out_dtypes):
    t, k = x2d.shape
    n = w_bf16.shape[1]
    assert sum(out_widths) == n and all(wd % 512 == 0 for wd in out_widths)
    return pl.pallas_call(
        _proj_kernel,
        grid=(t // ROW_BLOCK,),
        in_specs=[pl.BlockSpec((ROW_BLOCK, k), lambda i: (i, 0)), _const_spec((k, n))],
        out_specs=[pl.BlockSpec((ROW_BLOCK, wd), lambda i: (i, 0)) for wd in out_widths],
        out_shape=[jax.ShapeDtypeStruct((t, wd), dt) for wd, dt in zip(out_widths, out_dtypes)],
        compiler_params=_params(("parallel",)),
        name="in_proj",
    )(x2d, w_bf16)


def _transposed_value_tiles(vals, block):
    ones_tile = (lax.broadcasted_iota(jnp.int32, (BF16_ROWS, block), 0) == 0).astype(BF16)
    return [jnp.concatenate([vals[r:r + block].T.astype(BF16), ones_tile], axis=0)
            for r in range(0, vals.shape[0], block)]


def _proj_even_kernel(x_ref, w_ref, h_ref, vta_ref, vtb_ref):
    x = x_ref[...].astype(BF16)
    vd = 2 * HEAD_DIM
    dot = lambda c: jnp.dot(x, w_ref[:, c:c + 512], preferred_element_type=F32)
    h_ref[:, 0:512] = dot(0).astype(BF16)
    h_ref[:, 512:1024] = dot(A_W).astype(BF16)
    h_ref[:, 1024:1536] = dot(3 * A_W).astype(BF16)
    h_ref[:, 1536:2048] = dot(3 * A_W + B_W).astype(BF16)
    va = dot(2 * A_W)
    for h in range(A_HEADS):
        vta_ref[0, h, 0] = _transposed_value_tiles(va[:, h * vd:(h + 1) * vd], ATT_BLOCK)[0]
    vb = dot(3 * A_W + 2 * B_W)
    for hp in range(B_HEADS // 2):
        tiles = _transposed_value_tiles(vb[:, hp * vd:(hp + 1) * vd], BAND_BLOCK)
        for n, tile in enumerate(tiles):
            vtb_ref[0, hp, n] = tile


def _proj_even(x2d, w_bf16, bsz, seq):
    t, k = x2d.shape
    n = w_bf16.shape[1]
    assert ROW_BLOCK == ATT_BLOCK and seq % ROW_BLOCK == 0
    per_seq = seq // ROW_BLOCK
    vrows = 2 * HEAD_DIM + BF16_ROWS
    return pl.pallas_call(
        _proj_even_kernel,
        grid=(t // ROW_BLOCK,),
        in_specs=[pl.BlockSpec((ROW_BLOCK, k), lambda i: (i, 0)), _const_spec((k, n))],
        out_specs=[pl.BlockSpec((ROW_BLOCK, 2 * A_W + 2 * B_W), lambda i: (i, 0)),
                   pl.BlockSpec((1, A_HEADS, 1, vrows, ATT_BLOCK),
                                lambda i: (i // per_seq, 0, i % per_seq, 0, 0)),
                   pl.BlockSpec((1, B_HEADS // 2, ROW_BLOCK // BAND_BLOCK, vrows, BAND_BLOCK),
                                lambda i: (i // per_seq, 0, i % per_seq, 0, 0))],
        out_shape=[jax.ShapeDtypeStruct((t, 2 * A_W + 2 * B_W), BF16),
                   jax.ShapeDtypeStruct((bsz, A_HEADS, seq // ATT_BLOCK, vrows, ATT_BLOCK), BF16),
                   jax.ShapeDtypeStruct((bsz, B_HEADS // 2, seq // BAND_BLOCK, vrows, BAND_BLOCK), BF16)],
        compiler_params=_params(("parallel",)),
        name="in_proj_even",
    )(x2d, w_bf16)


def _mix_ffn_kernel(a_ref, b_ref, x_ref, wo_ref, g1_ref, b1_ref, wg_ref, wu_ref, wd_ref,
                    g2_ref, b2_ref, o_ref):
    ka = a_ref.shape[-1]
    mix = jnp.dot(a_ref[...], wo_ref[0:ka, :], preferred_element_type=F32)
    mix = mix + jnp.dot(b_ref[...], wo_ref[ka:, :], preferred_element_type=F32)
    x1 = _layer_norm(ALPHA * x_ref[...] + mix, g1_ref[...], b1_ref[...])
    xb = x1.astype(BF16)
    gate = jnp.dot(xb, wg_ref[...], preferred_element_type=F32)
    up = jnp.dot(xb, wu_ref[...], preferred_element_type=F32)
    h = (gate * jax.nn.sigmoid(gate) * up).astype(BF16)
    y = ALPHA * x1 + jnp.dot(h, wd_ref[...], preferred_element_type=F32)
    o_ref[...] = _layer_norm(y, g2_ref[...], b2_ref[...])


def _mix_ffn(a, b, x2d, wo, g1, b1, wg, wu, wd, g2, b2):
    t, d = x2d.shape
    ka, kb = a.shape[1], b.shape[1]
    f = wg.shape[1]
    vec = lambda v: v.reshape(1, d)
    return pl.pallas_call(
        _mix_ffn_kernel,
        grid=(t // ROW_BLOCK,),
        in_specs=[pl.BlockSpec((ROW_BLOCK, ka), lambda i: (i, 0)),
                  pl.BlockSpec((ROW_BLOCK, kb), lambda i: (i, 0)),
                  pl.BlockSpec((ROW_BLOCK, d), lambda i: (i, 0)),
                  _const_spec((ka + kb, d)), _const_spec((1, d)), _const_spec((1, d)),
                  _const_spec((d, f)), _const_spec((d, f)), _const_spec((f, d)),
                  _const_spec((1, d)), _const_spec((1, d))],
        out_specs=pl.BlockSpec((ROW_BLOCK, d), lambda i: (i, 0)),
        out_shape=jax.ShapeDtypeStruct((t, d), F32),
        compiler_params=_params(("parallel",)),
        name="mix_ffn",
    )(a, b, x2d, wo, vec(g1), vec(b1), wg, wu, wd, vec(g2), vec(b2))


def _diff_attn_kernel(q_ref, k_ref, vt_ref, past_ref, diag_ref, past_max_ref, diag_max_ref, step_ref,
                      lq1_ref, lk1_ref, lq2_ref, lk2_ref, g_ref, o_ref,
                      qt_ref, sa_ref, sb_ref, ra_ref, rb_ref, m_ref, acc_ref, *, lam_init):
    t = ATT_BLOCK
    n_chunks = t // CHUNK
    i = pl.program_id(2)
    scale = HEAD_DIM ** -0.5
    qt = q_ref[0].astype(F32).T * (scale * LOG2E)
    row = lax.broadcasted_iota(jnp.int32, qt.shape, 0)
    qt_ref[0] = jnp.where(row < HEAD_DIM, qt, 0.0).astype(BF16)
    qt_ref[1] = jnp.where(row >= HEAD_DIM, qt, 0.0).astype(BF16)

    m_ref[...] = jnp.full(m_ref.shape, NEG_INF, F32)
    acc_ref[...] = jnp.zeros(acc_ref.shape, F32)

    def scores(j, s_ref, r_ref):
        start = pl.multiple_of(j * t, t)
        k_blk = k_ref[0, pl.ds(start, t), :]
        for mp in range(2):
            s = jnp.dot(k_blk, qt_ref[mp], preferred_element_type=F32)
            s_ref[mp] = s
            r_ref[mp] = jnp.max(s.reshape(n_chunks, CHUNK // 8, 8, t), axis=1)

    def softmax_pv(j, s_ref, r_ref, tile_ref, tile_max_ref):
        vt_blk = vt_ref[0, 0, j]
        step = step_ref[0]
        for mp in range(2):
            m_prev = m_ref[mp] - step
            bound = jnp.max(r_ref[mp] + tile_max_ref[0][:, None, :], axis=0)
            bound = jnp.max(bound, axis=0, keepdims=True)
            m_new = jnp.maximum(m_prev, bound)
            alpha = jnp.exp2(m_prev - m_new)
            p = jnp.exp2(s_ref[mp] + tile_ref[0] - m_new)
            acc_ref[mp] = alpha * acc_ref[mp] + jnp.dot(vt_blk, p.astype(BF16),
                                                        preferred_element_type=F32)
            m_ref[mp] = m_new

    scores(0, sa_ref, ra_ref)

    def past_pair(jj, carry):
        j0 = 2 * jj
        scores(j0 + 1, sb_ref, rb_ref)
        softmax_pv(j0, sa_ref, ra_ref, past_ref, past_max_ref)
        scores(j0 + 2, sa_ref, ra_ref)
        softmax_pv(j0 + 1, sb_ref, rb_ref, past_ref, past_max_ref)
        return carry

    lax.fori_loop(0, lax.shift_right_logical(i, 1), past_pair, 0)

    @pl.when((i & 1) == 0)
    def _():
        softmax_pv(i, sa_ref, ra_ref, diag_ref, diag_max_ref)

    @pl.when((i & 1) == 1)
    def _():
        scores(i, sb_ref, rb_ref)
        softmax_pv(i - 1, sa_ref, ra_ref, past_ref, past_max_ref)
        softmax_pv(i, sb_ref, rb_ref, diag_ref, diag_max_ref)

    lam = (jnp.exp(jnp.sum(lq1_ref[...] * lk1_ref[...], axis=-1, keepdims=True))
           - jnp.exp(jnp.sum(lq2_ref[...] * lk2_ref[...], axis=-1, keepdims=True)) + lam_init)
    vd = 2 * HEAD_DIM
    ot = (acc_ref[0, 0:vd] / acc_ref[0, vd:vd + 1]
          - lam * (acc_ref[1, 0:vd] / acc_ref[1, vd:vd + 1]))
    ms = jnp.mean(ot * ot, axis=0, keepdims=True)
    y = (ot * lax.rsqrt(ms + LN_EPS)).T * g_ref[...] * (1.0 - lam_init)
    o_ref[0] = y.astype(o_ref.dtype)


def _diff_attention(hproj, vt, lam_q1, lam_k1, lam_q2, lam_k2, subln_g, lam_init):
    bsz, seq, _ = hproj.shape
    t = ATT_BLOCK
    nb = seq // t
    n_chunks = t // CHUNK
    vd = 2 * HEAD_DIM
    slopes = 2.0 ** (-8.0 * jnp.arange(1, A_HEADS + 1, dtype=F32) / A_HEADS)
    kk = jnp.arange(t)[:, None]
    qq = jnp.arange(t)[None, :]
    rel = (qq - kk).astype(F32)
    past = -(slopes * LOG2E)[:, None, None] * rel[None]
    allowed = (kk // CHUNK) <= (qq // CHUNK)
    diag = jnp.where(allowed[None], -(slopes * LOG2E)[:, None, None] * jnp.abs(rel)[None], NEG_INF)
    chunk_max = lambda tile: jnp.max(tile.reshape(A_HEADS, n_chunks, CHUNK, t), axis=2)
    step = jnp.broadcast_to((slopes * (LOG2E * t))[:, None, None], (A_HEADS, 1, t))
    vec = lambda v: v.reshape(1, HEAD_DIM).astype(F32)
    small = lambda shape: pl.BlockSpec(shape, lambda b, h, i: (0,) * len(shape))
    per_head = lambda shape: pl.BlockSpec(shape, lambda b, h, i: (h,) + (0,) * (len(shape) - 1))
    return pl.pallas_call(
        functools.partial(_diff_attn_kernel, lam_init=lam_init),
        grid=(bsz, A_HEADS, nb),
        in_specs=[pl.BlockSpec((1, t, vd), lambda b, h, i: (b, i, h)),
                  pl.BlockSpec((1, seq, vd), lambda b, h, i: (b, 0, A_HEADS + h)),
                  pl.BlockSpec((1, 1, nb, vd + BF16_ROWS, t), lambda b, h, i: (b, h, 0, 0, 0)),
                  per_head((1, t, t)), per_head((1, t, t)),
                  per_head((1, n_chunks, t)), per_head((1, n_chunks, t)), per_head((1, 1, t)),
                  small((1, HEAD_DIM)), small((1, HEAD_DIM)), small((1, HEAD_DIM)), small((1, HEAD_DIM)),
                  small((1, vd))],
        out_specs=pl.BlockSpec((1, t, vd), lambda b, h, i: (b, i, h)),
        out_shape=jax.ShapeDtypeStruct((bsz, seq, A_W), BF16),
        scratch_shapes=[pltpu.VMEM((2, vd, t), BF16),
                        pltpu.VMEM((2, t, t), F32), pltpu.VMEM((2, t, t), F32),
                        pltpu.VMEM((2, n_chunks, 8, t), F32), pltpu.VMEM((2, n_chunks, 8, t), F32),
                        pltpu.VMEM((2, 1, t), F32),
                        pltpu.VMEM((2, vd + BF16_ROWS, t), F32)],
        compiler_params=_params(("parallel", "parallel", "arbitrary")),
        name="diff_attn",
    )(hproj, hproj, vt, past, diag, chunk_max(past), chunk_max(diag), step,
      vec(lam_q1), vec(lam_k1), vec(lam_q2), vec(lam_k2), subln_g.reshape(1, vd).astype(F32))


def _band_attn_kernel(q_ref, k_ref, vt_ref, diag_vals_ref, band_ref, o_ref, bias_ref, sa_ref, sb_ref):
    t = BAND_BLOCK
    nb = q_ref.shape[1] // t
    n_win = BAND_WINDOW // t
    vd = 2 * HEAD_DIM
    scale = HEAD_DIM ** -0.5
    row = lax.broadcasted_iota(jnp.int32, (vd, t), 0)
    n_vals = diag_vals_ref.shape[-1]

    for hh in range(2):
        vals = jnp.broadcast_to(diag_vals_ref[hh], (8, n_vals))
        for c0 in range(0, BAND_WINDOW, 8):
            rolled = pltpu.roll(vals, (c0 - (BAND_WINDOW - 1)) % n_vals, axis=1, stride=1, stride_axis=0)
            bias_ref[hh, c0:c0 + 8, :] = jnp.where(band_ref[c0:c0 + 8, :] > 0.0, rolled[:, 0:t], NEG_INF)

    def rows(n):
        return pl.ds(pl.multiple_of(n * t, t), t)

    def key_block(n, w):
        return jnp.maximum(n - (n_win - 1) + w, 0)

    def scores(n, s_ref):
        qt = q_ref[0, rows(n), :].astype(F32).T * (scale * LOG2E)
        for hh in range(2):
            in_head = (row < HEAD_DIM) if hh == 0 else (row >= HEAD_DIM)
            qh = jnp.where(in_head, qt, 0.0).astype(BF16)
            for w in range(n_win):
                s_ref[hh, w * t:(w + 1) * t] = jnp.dot(k_ref[0, rows(key_block(n, w)), :], qh,
                                                       preferred_element_type=F32)

    def softmax_pv(n, s_ref, mask_keys):
        outs = []
        for hh in range(2):
            st = s_ref[hh] + bias_ref[hh]
            if mask_keys:
                key_pos = (n - (n_win - 1)) * t + lax.broadcasted_iota(jnp.int32, (BAND_WINDOW, 1), 0)
                st = jnp.where(key_pos >= 0, st, NEG_INF)
            p = jnp.exp2(st - jnp.max(st, axis=0, keepdims=True)).astype(BF16)
            ot = jnp.dot(vt_ref[0, 0, key_block(n, 0)], p[0:t], preferred_element_type=F32)
            for w in range(1, n_win):
                ot = ot + jnp.dot(vt_ref[0, 0, key_block(n, w)], p[w * t:(w + 1) * t],
                                  preferred_element_type=F32)
            outs.append(ot[0:vd] / ot[vd:vd + 1])
        o_ref[0, rows(n), :] = jnp.where(row < HEAD_DIM, outs[0], outs[1]).T.astype(o_ref.dtype)

    for n in range(n_win - 1):
        scores(n, sa_ref)
        softmax_pv(n, sa_ref, True)

    first = n_win - 1
    scores(first, sa_ref)

    def pair(jj, carry):
        n0 = first + 2 * jj
        scores(n0 + 1, sb_ref)
        softmax_pv(n0, sa_ref, False)
        scores(n0 + 2, sa_ref)
        softmax_pv(n0 + 1, sb_ref, False)
        return carry

    n_pairs = (nb - first - 2) // 2
    lax.fori_loop(0, n_pairs, pair, 0)
    for n in range(first + 2 * n_pairs, nb):
        if n + 1 < nb:
            scores(n + 1, sb_ref)
        softmax_pv(n, sa_ref, False)
        sa_ref, sb_ref = sb_ref, sa_ref


def _band_attention(hproj, vt, rel_bias):
    bsz, seq, _ = hproj.shape
    t = BAND_BLOCK
    vd = 2 * HEAD_DIM
    c = jnp.arange(BAND_WINDOW)[:, None]
    r = jnp.arange(t)[None, :]
    in_band = ((c // CHUNK) >= (r // CHUNK)) & ((c // CHUNK) <= (r // CHUNK) + B_LEFT_CHUNKS)
    rb = rel_bias.astype(F32) * LOG2E
    n_lo = BAND_WINDOW - 1 - B_LEFT_CHUNKS * CHUNK - REL_CLIP
    n_hi = t + BAND_WINDOW - n_lo - rb.shape[1]
    diag_vals = jnp.concatenate([jnp.broadcast_to(rb[:, :1], (B_HEADS, n_lo)), rb,
                                 jnp.broadcast_to(rb[:, -1:], (B_HEADS, n_hi))], axis=1)[:, None, :]
    qcol, kcol = (2 * A_W) // vd, (2 * A_W + B_W) // vd
    seq_spec = lambda col: pl.BlockSpec((1, seq, vd), lambda b, hp: (b, 0, col + hp))
    return pl.pallas_call(
        _band_attn_kernel,
        grid=(bsz, B_HEADS // 2),
        in_specs=[seq_spec(qcol), seq_spec(kcol),
                  pl.BlockSpec((1, 1, seq // t, vd + BF16_ROWS, t), lambda b, hp: (b, hp, 0, 0, 0)),
                  pl.BlockSpec((2, 1, t + BAND_WINDOW), lambda b, hp: (hp, 0, 0)),
                  pl.BlockSpec((BAND_WINDOW, t), lambda b, hp: (0, 0))],
        out_specs=pl.BlockSpec((1, seq, vd), lambda b, hp: (b, 0, hp)),
        out_shape=jax.ShapeDtypeStruct((bsz, seq, B_W), BF16),
        scratch_shapes=[pltpu.VMEM((2, BAND_WINDOW, t), F32),
                        pltpu.VMEM((2, BAND_WINDOW, t), F32), pltpu.VMEM((2, BAND_WINDOW, t), F32)],
        compiler_params=_params(("parallel", "parallel")),
        name="band_attn",
    )(hproj, hproj, vt, diag_vals, in_band.astype(F32))


def _retention_kernel(q_ref, k_ref, v_ref, gate_ref, dmask_ref, qdec_ref, kdec_ref, cdec_ref,
                      g_ref, beta_ref, o_ref, state_ref):
    @pl.when(pl.program_id(2) == 0)
    def _():
        state_ref[...] = jnp.zeros(state_ref.shape, F32)

    q = q_ref[0]
    k = k_ref[0]
    lane = lax.broadcasted_iota(jnp.int32, q.shape, 1)
    scale = HEAD_DIM ** -0.5
    vd = 2 * HEAD_DIM
    for hh in range(2):
        in_head = (lane < HEAD_DIM) if hh == 0 else (lane >= HEAD_DIM)
        qh = jnp.where(in_head, q, jnp.zeros_like(q))
        kh = jnp.where(in_head, k.astype(F32) * scale, 0.0)
        v = v_ref[0, :, hh * vd:(hh + 1) * vd]
        scores = lax.dot_general(qh, kh.astype(BF16), (((1,), (1,)), ((), ())),
                                 preferred_element_type=F32) * dmask_ref[hh]
        inner = jnp.dot(scores.astype(BF16), v, preferred_element_type=F32)
        state = state_ref[hh]
        cross = jnp.dot(qh, state.astype(BF16), preferred_element_type=F32) * qdec_ref[hh]
        kdt = (kh * kdec_ref[hh]).T.astype(BF16)
        kv = jnp.dot(kdt, v, preferred_element_type=F32)
        state_ref[hh] = cdec_ref[hh] * state + kv
        o = inner + cross
        mu = jnp.mean(o, axis=-1, keepdims=True)
        d = o - mu
        var = jnp.mean(d * d, axis=-1, keepdims=True)
        sl = slice(hh * vd, (hh + 1) * vd)
        y = d * lax.rsqrt(var + LN_EPS) * g_ref[:, sl] + beta_ref[:, sl]
        gate = gate_ref[0, :, sl]
        o_ref[0, :, sl] = (y * (gate * jax.nn.sigmoid(gate))).astype(o_ref.dtype)


def _retention(qkv, gz, gn_g, gn_b):
    bsz, seq, _ = qkv.shape
    ln = RET_BLOCK
    vd = 2 * HEAD_DIM
    log_gamma = jnp.log(1.0 - 2.0 ** (-5.0 - jnp.arange(C_HEADS, dtype=F32)))
    idx = jnp.arange(ln, dtype=F32)
    diff = idx[:, None] - idx[None, :]
    dmask = jnp.where(diff >= 0, jnp.exp(log_gamma[:, None, None] * jnp.maximum(diff, 0.0)), 0.0)
    qdec = jnp.broadcast_to(jnp.exp(log_gamma[:, None] * (idx + 1.0))[:, :, None], (C_HEADS, ln, vd))
    kdec = jnp.broadcast_to(jnp.exp(log_gamma[:, None] * (ln - 1.0 - idx))[:, :, None], (C_HEADS, ln, vd))
    cdec = jnp.broadcast_to(jnp.exp(log_gamma * ln)[:, None, None], (C_HEADS, 1, vd))
    pair = lambda shape: pl.BlockSpec(shape, lambda b, hp, c: (hp,) + (0,) * (len(shape) - 1))
    return pl.pallas_call(
        _retention_kernel,
        grid=(bsz, C_HEADS // 2, seq // ln),
        in_specs=[pl.BlockSpec((1, ln, 128), lambda b, hp, c: (b, c, hp)),
                  pl.BlockSpec((1, ln, 128), lambda b, hp, c: (b, c, C_QK_W // 128 + hp)),
                  pl.BlockSpec((1, ln, 2 * vd), lambda b, hp, c: (b, c, (2 * C_QK_W) // (2 * vd) + hp)),
                  pl.BlockSpec((1, ln, 2 * vd), lambda b, hp, c: (b, c, hp)),
                  pair((2, ln, ln)), pair((2, ln, vd)), pair((2, ln, vd)), pair((2, 1, vd)),
                  pl.BlockSpec((1, 2 * vd), lambda b, hp, c: (0, hp)),
                  pl.BlockSpec((1, 2 * vd), lambda b, hp, c: (0, hp))],
        out_specs=pl.BlockSpec((1, ln, 2 * vd), lambda b, hp, c: (b, c, hp)),
        out_shape=jax.ShapeDtypeStruct((bsz, seq, C_W), BF16),
        scratch_shapes=[pltpu.VMEM((2, vd, vd), F32)],
        compiler_params=_params(("parallel", "parallel", "arbitrary")),
        name="retention",
    )(qkv, qkv, qkv, gz, dmask, qdec, kdec, cdec, gn_g.reshape(1, C_W), gn_b.reshape(1, C_W))


def _sgu_kernel(u_ref, v_ref, w_ref, bt_ref, g_ref, beta_ref, o_ref):
    u = jax.nn.gelu(u_ref[0])
    v = _layer_norm(jax.nn.gelu(v_ref[0]), g_ref[...], beta_ref[...])
    gw = D_WIDTH // D_GROUPS
    r = lax.broadcasted_iota(jnp.int32, (D_CHUNK, D_CHUNK), 0)
    c = lax.broadcasted_iota(jnp.int32, (D_CHUNK, D_CHUNK), 1)
    for g in range(D_GROUPS):
        w = jnp.where(r >= c, w_ref[g], 0.0).astype(BF16)
        bias = bt_ref[:, g:g + 1]
        cols = slice(g * gw, (g + 1) * gw)
        vg = v[:, cols].astype(BF16)
        for n in range(SGU_ROWS // D_CHUNK):
            rows = slice(n * D_CHUNK, (n + 1) * D_CHUNK)
            mixed = jnp.dot(w, vg[rows], preferred_element_type=F32) + bias
            o_ref[0, rows, cols] = (u[rows, cols] * mixed).astype(o_ref.dtype)


def _spatial_gating(gz, w_s, b_s, ln_g, ln_b):
    bsz, seq, _ = gz.shape
    rws = SGU_ROWS
    small = lambda shape: pl.BlockSpec(shape, lambda b, i: (0,) * len(shape))
    return pl.pallas_call(
        _sgu_kernel,
        grid=(bsz, seq // rws),
        in_specs=[pl.BlockSpec((1, rws, D_WIDTH), lambda b, i: (b, i, 1)),
                  pl.BlockSpec((1, rws, D_WIDTH), lambda b, i: (b, i, 2)),
                  small((D_GROUPS, D_CHUNK, D_CHUNK)), small((D_CHUNK, D_GROUPS)),
                  small((1, D_WIDTH)), small((1, D_WIDTH))],
        out_specs=pl.BlockSpec((1, rws, D_WIDTH), lambda b, i: (b, i, 0)),
        out_shape=jax.ShapeDtypeStruct((bsz, seq, D_WIDTH), BF16),
        compiler_params=_params(("parallel", "parallel")),
        name="spatial_gating",
    )(gz, gz, w_s.astype(F32), b_s.T.astype(F32), ln_g.reshape(1, D_WIDTH), ln_b.reshape(1, D_WIDTH))


def kernel(x, w_in_even, lam_q1, lam_k1, lam_q2, lam_k2, diff_subln_g, rel_bias, w_out_even,
           w_in_odd, ret_gn_g, ret_gn_b, sgu_ln_g, sgu_ln_b, sgu_w, sgu_b, w_out_odd,
           ln_mix_g, ln_mix_b, ffn_w_gate, ffn_w_up, ffn_w_down, ln_ffn_g, ln_ffn_b):
    bsz, seq, d = x.shape
    t = bsz * seq
    x2d = x.reshape(t, d)
    for l in range(DEPTH):
        j = l // 2
        if l % 2 == 0:
            hproj, vta, vtb = _proj_even(x2d, w_in_even[j].astype(BF16), bsz, seq)
            hproj = hproj.reshape(bsz, seq, -1)
            lam_init = 0.8 - 0.6 * math.exp(-0.3 * l)
            mix_a = _diff_attention(hproj, vta, lam_q1[j], lam_k1[j], lam_q2[j], lam_k2[j],
                                    diff_subln_g[j], lam_init)
            mix_b = _band_attention(hproj, vtb, rel_bias[j])
            w_out = w_out_even[j]
        else:
            qkv, gz = _proj(x2d, w_in_odd[j].astype(BF16), [2 * C_QK_W + C_W, C_W + 2 * D_WIDTH],
                            [BF16, F32])
            qkv = qkv.reshape(bsz, seq, -1)
            gz = gz.reshape(bsz, seq, -1)
            mix_a = _retention(qkv, gz, ret_gn_g[j], ret_gn_b[j])
            mix_b = _spatial_gating(gz, sgu_w[j], sgu_b[j], sgu_ln_g[j], sgu_ln_b[j])
            w_out = w_out_odd[j]
        x2d = _mix_ffn(mix_a.reshape(t, -1), mix_b.reshape(t, -1), x2d, w_out.astype(BF16),
                       ln_mix_g[l], ln_mix_b[l], ffn_w_gate[l].astype(BF16), ffn_w_up[l].astype(BF16),
                       ffn_w_down[l].astype(BF16), ln_ffn_g[l], ln_ffn_b[l])
    return x2d.reshape(bsz, seq, d)
```

```python
import functools
import math

import jax
import jax.numpy as jnp
from jax import lax
from jax.experimental import pallas as pl
from jax.experimental.pallas import tpu as pltpu

F32 = jnp.float32
BF16 = jnp.bfloat16

D_MODEL = 1024
HEAD_DIM = 64
CHUNK = 64
A_HEADS = 4
A_W = 512
B_HEADS = 8
B_W = 512
B_LEFT_CHUNKS = 8
REL_CLIP = 128
C_HEADS = 4
C_QK_W = 256
C_W = 512
D_CHUNK = 128
D_GROUPS = 4
D_WIDTH = 512
DEPTH = 2
ALPHA = (2.0 * DEPTH) ** 0.25
LN_EPS = 1e-5
NEG_INF = -1e30
LOG2E = 1.4426950408889634
SKIP_LOG2 = 160.0
NORM_SLACK = 1.0 + 2.0 ** -8

VMEM_LIMIT_V7X = 56 * 1024 * 1024
BF16_ROWS = 16

ROW_BLOCK = 512
ATT_BLOCK = 512
BAND_BLOCK = 256
BAND_WINDOW = BAND_BLOCK + B_LEFT_CHUNKS * CHUNK
RET_BLOCK = 256
SGU_ROWS = 512


def _params(semantics):
    return pltpu.CompilerParams(dimension_semantics=semantics, vmem_limit_bytes=VMEM_LIMIT_V7X)


def _const_spec(shape):
    nd = len(shape)
    return pl.BlockSpec(shape, lambda *_: (0,) * nd, pipeline_mode=pl.Buffered(1))


def _layer_norm(y, g, b):
    mu = jnp.mean(y, axis=-1, keepdims=True)
    d = y - mu
    var = jnp.mean(d * d, axis=-1, keepdims=True)
    return d * lax.rsqrt(var + LN_EPS) * g + b


def _proj_kernel(x_ref, w_ref, *out_refs):
    x = x_ref[...].astype(BF16)
    off = 0
    for o_ref in out_refs:
        n = o_ref.shape[-1]
        for c in range(0, n, 512):
            acc = jnp.dot(x, w_ref[:, off + c:off + c + 512], preferred_element_type=F32)
            o_ref[:, c:c + 512] = acc.astype(o_ref.dtype)
        off += n


def _proj(x2d, w_bf16, out_widths, out_dtypes):
    t, k = x2d.shape
    n = w_bf16.shape[1]
    assert sum(out_widths) == n and all(wd % 512 == 0 for wd in out_widths)
    return pl.pallas_call(
        _proj_kernel,
        grid=(t // ROW_BLOCK,),
        in_specs=[pl.BlockSpec((ROW_BLOCK, k), lambda i: (i, 0)), _const_spec((k, n))],
        out_specs=[pl.BlockSpec((ROW_BLOCK, wd), lambda i: (i, 0)) for wd in out_widths],
        out_shape=[jax.ShapeDtypeStruct((t, wd), dt) for wd, dt in zip(out_widths, out_dtypes)],
        compiler_params=_params(("parallel",)),
        name="in_proj",
    )(x2d, w_bf16)


def _transposed_value_tiles(vals, block):
    ones_tile = (lax.broadcasted_iota(jnp.int32, (BF16_ROWS, block), 0) == 0).astype(BF16)
    return [jnp.concatenate([vals[r:r + block].T.astype(BF16), ones_tile], axis=0)
            for r in range(0, vals.shape[0], block)]


def _proj_even_kernel(x_ref, w_ref, h_ref, vta_ref, vtb_ref):
    x = x_ref[...].astype(BF16)
    vd = 2 * HEAD_DIM
    dot = lambda c: jnp.dot(x, w_ref[:, c:c + 512], preferred_element_type=F32)
    h_ref[:, 0:512] = dot(0).astype(BF16)
    h_ref[:, 512:1024] = dot(A_W).astype(BF16)
    h_ref[:, 1024:1536] = dot(3 * A_W).astype(BF16)
    h_ref[:, 1536:2048] = dot(3 * A_W + B_W).astype(BF16)
    va = dot(2 * A_W)
    for h in range(A_HEADS):
        vta_ref[0, h, 0] = _transposed_value_tiles(va[:, h * vd:(h + 1) * vd], ATT_BLOCK)[0]
    vb = dot(3 * A_W + 2 * B_W)
    for hp in range(B_HEADS // 2):
        tiles = _transposed_value_tiles(vb[:, hp * vd:(hp + 1) * vd], BAND_BLOCK)
        for n, tile in enumerate(tiles):
            vtb_ref[0, hp, n] = tile


def _proj_even(x2d, w_bf16, bsz, seq):
    t, k = x2d.shape
    n = w_bf16.shape[1]
    assert ROW_BLOCK == ATT_BLOCK and seq % ROW_BLOCK == 0
    per_seq = seq // ROW_BLOCK
    vrows = 2 * HEAD_DIM + BF16_ROWS
    return pl.pallas_call(
        _proj_even_kernel,
        grid=(t // ROW_BLOCK,),
        in_specs=[pl.BlockSpec((ROW_BLOCK, k), lambda i: (i, 0)), _const_spec((k, n))],
        out_specs=[pl.BlockSpec((ROW_BLOCK, 2 * A_W + 2 * B_W), lambda i: (i, 0)),
                   pl.BlockSpec((1, A_HEADS, 1, vrows, ATT_BLOCK),
                                lambda i: (i // per_seq, 0, i % per_seq, 0, 0)),
                   pl.BlockSpec((1, B_HEADS // 2, ROW_BLOCK // BAND_BLOCK, vrows, BAND_BLOCK),
                                lambda i: (i // per_seq, 0, i % per_seq, 0, 0))],
        out_shape=[jax.ShapeDtypeStruct((t, 2 * A_W + 2 * B_W), BF16),
                   jax.ShapeDtypeStruct((bsz, A_HEADS, seq // ATT_BLOCK, vrows, ATT_BLOCK), BF16),
                   jax.ShapeDtypeStruct((bsz, B_HEADS // 2, seq // BAND_BLOCK, vrows, BAND_BLOCK), BF16)],
        compiler_params=_params(("parallel",)),
        name="in_proj_even",
    )(x2d, w_bf16)


def _mix_ffn_kernel(a_ref, b_ref, x_ref, wo_ref, g1_ref, b1_ref, wg_ref, wu_ref, wd_ref,
                    g2_ref, b2_ref, o_ref):
    ka = a_ref.shape[-1]
    mix = jnp.dot(a_ref[...], wo_ref[0:ka, :], preferred_element_type=F32)
    mix = mix + jnp.dot(b_ref[...], wo_ref[ka:, :], preferred_element_type=F32)
    x1 = _layer_norm(ALPHA * x_ref[...] + mix, g1_ref[...], b1_ref[...])
    xb = x1.astype(BF16)
    gate = jnp.dot(xb, wg_ref[...], preferred_element_type=F32)
    up = jnp.dot(xb, wu_ref[...], preferred_element_type=F32)
    h = (gate * jax.nn.sigmoid(gate) * up).astype(BF16)
    y = ALPHA * x1 + jnp.dot(h, wd_ref[...], preferred_element_type=F32)
    o_ref[...] = _layer_norm(y, g2_ref[...], b2_ref[...])


def _mix_ffn(a, b, x2d, wo, g1, b1, wg, wu, wd, g2, b2):
    t, d = x2d.shape
    ka, kb = a.shape[1], b.shape[1]
    f = wg.shape[1]
    vec = lambda v: v.reshape(1, d)
    return pl.pallas_call(
        _mix_ffn_kernel,
        grid=(t // ROW_BLOCK,),
        in_specs=[pl.BlockSpec((ROW_BLOCK, ka), lambda i: (i, 0)),
                  pl.BlockSpec((ROW_BLOCK, kb), lambda i: (i, 0)),
                  pl.BlockSpec((ROW_BLOCK, d), lambda i: (i, 0)),
                  _const_spec((ka + kb, d)), _const_spec((1, d)), _const_spec((1, d)),
                  _const_spec((d, f)), _const_spec((d, f)), _const_spec((f, d)),
                  _const_spec((1, d)), _const_spec((1, d))],
        out_specs=pl.BlockSpec((ROW_BLOCK, d), lambda i: (i, 0)),
        out_shape=jax.ShapeDtypeStruct((t, d), F32),
        compiler_params=_params(("parallel",)),
        name="mix_ffn",
    )(a, b, x2d, wo, vec(g1), vec(b1), wg, wu, wd, vec(g2), vec(b2))


def _diff_attn_kernel(q_ref, k_ref, vt_ref, past_ref, diag_ref, past_max_ref, diag_max_ref, step_ref,
                      lq1_ref, lk1_ref, lq2_ref, lk2_ref, g_ref, o_ref,
                      qt_ref, sa_ref, sb_ref, ra_ref, rb_ref, m_ref, acc_ref, knorm_ref, *, lam_init):
    t = ATT_BLOCK
    n_chunks = t // CHUNK
    i = pl.program_id(2)
    scale = HEAD_DIM ** -0.5
    qt = q_ref[0].astype(F32).T * (scale * LOG2E)
    row = lax.broadcasted_iota(jnp.int32, qt.shape, 0)
    qt_ref[0] = jnp.where(row < HEAD_DIM, qt, 0.0).astype(BF16)
    qt_ref[1] = jnp.where(row >= HEAD_DIM, qt, 0.0).astype(BF16)

    m_ref[...] = jnp.full(m_ref.shape, NEG_INF, F32)
    acc_ref[...] = jnp.zeros(acc_ref.shape, F32)

    @pl.when(i == 0)
    def _():
        lane = lax.broadcasted_iota(jnp.int32, (t, 2 * HEAD_DIM), 1)

        def block_norms(j, carry):
            kb = k_ref[0, pl.ds(pl.multiple_of(j * t, t), t), :].astype(F32)
            sq = kb * kb
            n1 = jnp.sum(jnp.where(lane < HEAD_DIM, sq, 0.0), axis=1, keepdims=True)
            n2 = jnp.sum(jnp.where(lane >= HEAD_DIM, sq, 0.0), axis=1, keepdims=True)
            return (jnp.maximum(carry[0], jnp.max(n1, axis=0, keepdims=True)),
                    jnp.maximum(carry[1], jnp.max(n2, axis=0, keepdims=True)))

        zero = jnp.zeros((1, 1), F32)
        n1, n2 = lax.fori_loop(0, k_ref.shape[1] // t, block_norms, (zero, zero))
        knorm_ref[0] = jnp.broadcast_to(jnp.sqrt(n1), (1, t))
        knorm_ref[1] = jnp.broadcast_to(jnp.sqrt(n2), (1, t))

    def scores(j, s_ref, r_ref):
        start = pl.multiple_of(j * t, t)
        k_blk = k_ref[0, pl.ds(start, t), :]
        for mp in range(2):
            s = jnp.dot(k_blk, qt_ref[mp], preferred_element_type=F32)
            s_ref[mp] = s
            r_ref[mp] = jnp.max(s.reshape(n_chunks, CHUNK // 8, 8, t), axis=1)

    def softmax_pv(j, s_ref, r_ref, tile_ref, tile_max_ref):
        vt_blk = vt_ref[0, 0, j]
        step = step_ref[0]
        for mp in range(2):
            m_prev = m_ref[mp]
            bound = jnp.max(r_ref[mp] + tile_max_ref[0][:, None, :], axis=0)
            bound = jnp.max(bound, axis=0, keepdims=True)
            m_new = jnp.maximum(m_prev, bound)
            alpha = jnp.exp2(m_prev - m_new)
            p = jnp.exp2(s_ref[mp] + tile_ref[0] - m_new)
            acc_ref[mp] = alpha * acc_ref[mp] + jnp.dot(vt_blk, p.astype(BF16),
                                                        preferred_element_type=F32)
            m_ref[mp] = m_new + step

    def prev_block(r):
        return jnp.maximum(i - r, 0)

    scores(i, sa_ref, ra_ref)
    scores(prev_block(1), sb_ref, rb_ref)
    softmax_pv(i, sa_ref, ra_ref, diag_ref, diag_max_ref)

    tile_top = jnp.max(past_max_ref[0], axis=0, keepdims=True)
    inv_step = 1.0 / step_ref[0]
    reach = jnp.zeros((1, t), F32)
    for mp in range(2):
        qb = qt_ref[mp].astype(F32)
        qnorm = jnp.sqrt(jnp.sum(qb * qb, axis=0, keepdims=True))
        gap = qnorm * knorm_ref[mp] * NORM_SLACK + tile_top - m_ref[mp]
        reach = jnp.maximum(reach, (gap + SKIP_LOG2) * inv_step + 1.0)
    reach = jnp.floor(jnp.minimum(reach, float(k_ref.shape[1] // t))).astype(jnp.int32)
    n_keep = jnp.minimum(jnp.max(reach), i)

    def past_pair(jj, carry):
        r0 = 2 * jj + 1
        scores(prev_block(r0 + 1), sa_ref, ra_ref)
        softmax_pv(i - r0, sb_ref, rb_ref, past_ref, past_max_ref)
        scores(prev_block(r0 + 2), sb_ref, rb_ref)
        softmax_pv(i - r0 - 1, sa_ref, ra_ref, past_ref, past_max_ref)
        return carry

    lax.fori_loop(0, lax.shift_right_logical(n_keep, 1), past_pair, 0)

    @pl.when((n_keep & 1) == 1)
    def _():
        softmax_pv(i - n_keep, sb_ref, rb_ref, past_ref, past_max_ref)

    lam = (jnp.exp(jnp.sum(lq1_ref[...] * lk1_ref[...], axis=-1, keepdims=True))
           - jnp.exp(jnp.sum(lq2_ref[...] * lk2_ref[...], axis=-1, keepdims=True)) + lam_init)
    vd = 2 * HEAD_DIM
    ot = (acc_ref[0, 0:vd] / acc_ref[0, vd:vd + 1]
          - lam * (acc_ref[1, 0:vd] / acc_ref[1, vd:vd + 1]))
    ms = jnp.mean(ot * ot, axis=0, keepdims=True)
    y = (ot * lax.rsqrt(ms + LN_EPS)).T * g_ref[...] * (1.0 - lam_init)
    o_ref[0] = y.astype(o_ref.dtype)


def _diff_attention(hproj, vt, lam_q1, lam_k1, lam_q2, lam_k2, subln_g, lam_init):
    bsz, seq, _ = hproj.shape
    t = ATT_BLOCK
    nb = seq // t
    n_chunks = t // CHUNK
    vd = 2 * HEAD_DIM
    slopes = 2.0 ** (-8.0 * jnp.arange(1, A_HEADS + 1, dtype=F32) / A_HEADS)
    kk = jnp.arange(t)[:, None]
    qq = jnp.arange(t)[None, :]
    rel = (qq - kk).astype(F32)
    past = -(slopes * LOG2E)[:, None, None] * rel[None]
    allowed = (kk // CHUNK) <= (qq // CHUNK)
    diag = jnp.where(allowed[None], -(slopes * LOG2E)[:, None, None] * jnp.abs(rel)[None], NEG_INF)
    chunk_max = lambda tile: jnp.max(tile.reshape(A_HEADS, n_chunks, CHUNK, t), axis=2)
    step = jnp.broadcast_to((slopes * (LOG2E * t))[:, None, None], (A_HEADS, 1, t))
    vec = lambda v: v.reshape(1, HEAD_DIM).astype(F32)
    small = lambda shape: pl.BlockSpec(shape, lambda b, h, i: (0,) * len(shape))
    per_head = lambda shape: pl.BlockSpec(shape, lambda b, h, i: (h,) + (0,) * (len(shape) - 1))
    return pl.pallas_call(
        functools.partial(_diff_attn_kernel, lam_init=lam_init),
        grid=(bsz, A_HEADS, nb),
        in_specs=[pl.BlockSpec((1, t, vd), lambda b, h, i: (b, i, h)),
                  pl.BlockSpec((1, seq, vd), lambda b, h, i: (b, 0, A_HEADS + h)),
                  pl.BlockSpec((1, 1, nb, vd + BF16_ROWS, t), lambda b, h, i: (b, h, 0, 0, 0)),
                  per_head((1, t, t)), per_head((1, t, t)),
                  per_head((1, n_chunks, t)), per_head((1, n_chunks, t)), per_head((1, 1, t)),
                  small((1, HEAD_DIM)), small((1, HEAD_DIM)), small((1, HEAD_DIM)), small((1, HEAD_DIM)),
                  small((1, vd))],
        out_specs=pl.BlockSpec((1, t, vd), lambda b, h, i: (b, i, h)),
        out_shape=jax.ShapeDtypeStruct((bsz, seq, A_W), BF16),
        scratch_shapes=[pltpu.VMEM((2, vd, t), BF16),
                        pltpu.VMEM((2, t, t), F32), pltpu.VMEM((2, t, t), F32),
                        pltpu.VMEM((2, n_chunks, 8, t), F32), pltpu.VMEM((2, n_chunks, 8, t), F32),
                        pltpu.VMEM((2, 1, t), F32),
                        pltpu.VMEM((2, vd + BF16_ROWS, t), F32),
                        pltpu.VMEM((2, 1, t), F32)],
        compiler_params=_params(("parallel", "parallel", "arbitrary")),
        name="diff_attn",
    )(hproj, hproj, vt, past, diag, chunk_max(past), chunk_max(diag), step,
      vec(lam_q1), vec(lam_k1), vec(lam_q2), vec(lam_k2), subln_g.reshape(1, vd).astype(F32))


def _band_attn_kernel(q_ref, k_ref, vt_ref, diag_vals_ref, band_ref, o_ref, bias_ref, sa_ref, sb_ref):
    t = BAND_BLOCK
    nb = q_ref.shape[1] // t
    n_win = BAND_WINDOW // t
    vd = 2 * HEAD_DIM
    scale = HEAD_DIM ** -0.5
    row = lax.broadcasted_iota(jnp.int32, (vd, t), 0)
    n_vals = diag_vals_ref.shape[-1]

    for hh in range(2):
        vals = jnp.broadcast_to(diag_vals_ref[hh], (8, n_vals))
        for c0 in range(0, BAND_WINDOW, 8):
            rolled = pltpu.roll(vals, (c0 - (BAND_WINDOW - 1)) % n_vals, axis=1, stride=1, stride_axis=0)
            bias_ref[hh, c0:c0 + 8, :] = jnp.where(band_ref[c0:c0 + 8, :] > 0.0, rolled[:, 0:t], NEG_INF)

    def rows(n):
        return pl.ds(pl.multiple_of(n * t, t), t)

    def key_block(n, w):
        return jnp.maximum(n - (n_win - 1) + w, 0)

    def scores(n, s_ref):
        qt = q_ref[0, rows(n), :].astype(F32).T * (scale * LOG2E)
        for hh in range(2):
            in_head = (row < HEAD_DIM) if hh == 0 else (row >= HEAD_DIM)
            qh = jnp.where(in_head, qt, 0.0).astype(BF16)
            for w in range(n_win):
                s_ref[hh, w * t:(w + 1) * t] = jnp.dot(k_ref[0, rows(key_block(n, w)), :], qh,
                                                       preferred_element_type=F32)

    def softmax_pv(n, s_ref, mask_keys):
        outs = []
        for hh in range(2):
            st = s_ref[hh] + bias_ref[hh]
            if mask_keys:
                key_pos = (n - (n_win - 1)) * t + lax.broadcasted_iota(jnp.int32, (BAND_WINDOW, 1), 0)
                st = jnp.where(key_pos >= 0, st, NEG_INF)
            p = jnp.exp2(st - jnp.max(st, axis=0, keepdims=True)).astype(BF16)
            ot = jnp.dot(vt_ref[0, 0, key_block(n, 0)], p[0:t], preferred_element_type=F32)
            for w in range(1, n_win):
                ot = ot + jnp.dot(vt_ref[0, 0, key_block(n, w)], p[w * t:(w + 1) * t],
                                  preferred_element_type=F32)
            outs.append(ot[0:vd] / ot[vd:vd + 1])
        o_ref[0, rows(n), :] = jnp.where(row < HEAD_DIM, outs[0], outs[1]).T.astype(o_ref.dtype)

    for n in range(n_win - 1):
        scores(n, sa_ref)
        softmax_pv(n, sa_ref, True)

    first = n_win - 1
    scores(first, sa_ref)

    def pair(jj, carry):
        n0 = first + 2 * jj
        scores(n0 + 1, sb_ref)
        softmax_pv(n0, sa_ref, False)
        scores(n0 + 2, sa_ref)
        softmax_pv(n0 + 1, sb_ref, False)
        return carry

    n_pairs = (nb - first - 2) // 2
    lax.fori_loop(0, n_pairs, pair, 0)
    for n in range(first + 2 * n_pairs, nb):
        if n + 1 < nb:
            scores(n + 1, sb_ref)
        softmax_pv(n, sa_ref, False)
        sa_ref, sb_ref = sb_ref, sa_ref


def _band_attention(hproj, vt, rel_bias):
    bsz, seq, _ = hproj.shape
    t = BAND_BLOCK
    vd = 2 * HEAD_DIM
    c = jnp.arange(BAND_WINDOW)[:, None]
    r = jnp.arange(t)[None, :]
    in_band = ((c // CHUNK) >= (r // CHUNK)) & ((c // CHUNK) <= (r // CHUNK) + B_LEFT_CHUNKS)
    rb = rel_bias.astype(F32) * LOG2E
    n_lo = BAND_WINDOW - 1 - B_LEFT_CHUNKS * CHUNK - REL_CLIP
    n_hi = t + BAND_WINDOW - n_lo - rb.shape[1]
    diag_vals = jnp.concatenate([jnp.broadcast_to(rb[:, :1], (B_HEADS, n_lo)), rb,
                                 jnp.broadcast_to(rb[:, -1:], (B_HEADS, n_hi))], axis=1)[:, None, :]
    qcol, kcol = (2 * A_W) // vd, (2 * A_W + B_W) // vd
    seq_spec = lambda col: pl.BlockSpec((1, seq, vd), lambda b, hp: (b, 0, col + hp))
    return pl.pallas_call(
        _band_attn_kernel,
        grid=(bsz, B_HEADS // 2),
        in_specs=[seq_spec(qcol), seq_spec(kcol),
                  pl.BlockSpec((1, 1, seq // t, vd + BF16_ROWS, t), lambda b, hp: (b, hp, 0, 0, 0)),
                  pl.BlockSpec((2, 1, t + BAND_WINDOW), lambda b, hp: (hp, 0, 0)),
                  pl.BlockSpec((BAND_WINDOW, t), lambda b, hp: (0, 0))],
        out_specs=pl.BlockSpec((1, seq, vd), lambda b, hp: (b, 0, hp)),
        out_shape=jax.ShapeDtypeStruct((bsz, seq, B_W), BF16),
        scratch_shapes=[pltpu.VMEM((2, BAND_WINDOW, t), F32),
                        pltpu.VMEM((2, BAND_WINDOW, t), F32), pltpu.VMEM((2, BAND_WINDOW, t), F32)],
        compiler_params=_params(("parallel", "parallel")),
        name="band_attn",
    )(hproj, hproj, vt, diag_vals, in_band.astype(F32))


def _retention_kernel(q_ref, k_ref, v_ref, gate_ref, dmask_ref, qdec_ref, kdec_ref, cdec_ref,
                      g_ref, beta_ref, o_ref, state_ref):
    @pl.when(pl.program_id(2) == 0)
    def _():
        state_ref[...] = jnp.zeros(state_ref.shape, F32)

    q = q_ref[0]
    k = k_ref[0]
    lane = lax.broadcasted_iota(jnp.int32, q.shape, 1)
    scale = HEAD_DIM ** -0.5
    vd = 2 * HEAD_DIM
    for hh in range(2):
        in_head = (lane < HEAD_DIM) if hh == 0 else (lane >= HEAD_DIM)
        qh = jnp.where(in_head, q, jnp.zeros_like(q))
        kh = jnp.where(in_head, k.astype(F32) * scale, 0.0)
        v = v_ref[0, :, hh * vd:(hh + 1) * vd]
        scores = lax.dot_general(qh, kh.astype(BF16), (((1,), (1,)), ((), ())),
                                 preferred_element_type=F32) * dmask_ref[hh]
        inner = jnp.dot(scores.astype(BF16), v, preferred_element_type=F32)
        state = state_ref[hh]
        cross = jnp.dot(qh, state.astype(BF16), preferred_element_type=F32) * qdec_ref[hh]
        kdt = (kh * kdec_ref[hh]).T.astype(BF16)
        kv = jnp.dot(kdt, v, preferred_element_type=F32)
        state_ref[hh] = cdec_ref[hh] * state + kv
        o = inner + cross
        mu = jnp.mean(o, axis=-1, keepdims=True)
        d = o - mu
        var = jnp.mean(d * d, axis=-1, keepdims=True)
        sl = slice(hh * vd, (hh + 1) * vd)
        y = d * lax.rsqrt(var + LN_EPS) * g_ref[:, sl] + beta_ref[:, sl]
        gate = gate_ref[0, :, sl]
        o_ref[0, :, sl] = (y * (gate * jax.nn.sigmoid(gate))).astype(o_ref.dtype)


def _retention(qkv, gz, gn_g, gn_b):
    bsz, seq, _ = qkv.shape
    ln = RET_BLOCK
    vd = 2 * HEAD_DIM
    log_gamma = jnp.log(1.0 - 2.0 ** (-5.0 - jnp.arange(C_HEADS, dtype=F32)))
    idx = jnp.arange(ln, dtype=F32)
    diff = idx[:, None] - idx[None, :]
    dmask = jnp.where(diff >= 0, jnp.exp(log_gamma[:, None, None] * jnp.maximum(diff, 0.0)), 0.0)
    qdec = jnp.broadcast_to(jnp.exp(log_gamma[:, None] * (idx + 1.0))[:, :, None], (C_HEADS, ln, vd))
    kdec = jnp.broadcast_to(jnp.exp(log_gamma[:, None] * (ln - 1.0 - idx))[:, :, None], (C_HEADS, ln, vd))
    cdec = jnp.broadcast_to(jnp.exp(log_gamma * ln)[:, None, None], (C_HEADS, 1, vd))
    pair = lambda shape: pl.BlockSpec(shape, lambda b, hp, c: (hp,) + (0,) * (len(shape) - 1))
    return pl.pallas_call(
        _retention_kernel,
        grid=(bsz, C_HEADS // 2, seq // ln),
        in_specs=[pl.BlockSpec((1, ln, 128), lambda b, hp, c: (b, c, hp)),
                  pl.BlockSpec((1, ln, 128), lambda b, hp, c: (b, c, C_QK_W // 128 + hp)),
                  pl.BlockSpec((1, ln, 2 * vd), lambda b, hp, c: (b, c, (2 * C_QK_W) // (2 * vd) + hp)),
                  pl.BlockSpec((1, ln, 2 * vd), lambda b, hp, c: (b, c, hp)),
                  pair((2, ln, ln)), pair((2, ln, vd)), pair((2, ln, vd)), pair((2, 1, vd)),
                  pl.BlockSpec((1, 2 * vd), lambda b, hp, c: (0, hp)),
                  pl.BlockSpec((1, 2 * vd), lambda b, hp, c: (0, hp))],
        out_specs=pl.BlockSpec((1, ln, 2 * vd), lambda b, hp, c: (b, c, hp)),
        out_shape=jax.ShapeDtypeStruct((bsz, seq, C_W), BF16),
        scratch_shapes=[pltpu.VMEM((2, vd, vd), F32)],
        compiler_params=_params(("parallel", "parallel", "arbitrary")),
        name="retention",
    )(qkv, qkv, qkv, gz, dmask, qdec, kdec, cdec, gn_g.reshape(1, C_W), gn_b.reshape(1, C_W))


def _sgu_kernel(u_ref, v_ref, w_ref, bt_ref, g_ref, beta_ref, o_ref):
    u = jax.nn.gelu(u_ref[0])
    v = _layer_norm(jax.nn.gelu(v_ref[0]), g_ref[...], beta_ref[...])
    gw = D_WIDTH // D_GROUPS
    r = lax.broadcasted_iota(jnp.int32, (D_CHUNK, D_CHUNK), 0)
    c = lax.broadcasted_iota(jnp.int32, (D_CHUNK, D_CHUNK), 1)
    for g in range(D_GROUPS):
        w = jnp.where(r >= c, w_ref[g], 0.0).astype(BF16)
        bias = bt_ref[:, g:g + 1]
        cols = slice(g * gw, (g + 1) * gw)
        vg = v[:, cols].astype(BF16)
        for n in range(SGU_ROWS // D_CHUNK):
            rows = slice(n * D_CHUNK, (n + 1) * D_CHUNK)
            mixed = jnp.dot(w, vg[rows], preferred_element_type=F32) + bias
            o_ref[0, rows, cols] = (u[rows, cols] * mixed).astype(o_ref.dtype)


def _spatial_gating(gz, w_s, b_s, ln_g, ln_b):
    bsz, seq, _ = gz.shape
    rws = SGU_ROWS
    small = lambda shape: pl.BlockSpec(shape, lambda b, i: (0,) * len(shape))
    return pl.pallas_call(
        _sgu_kernel,
        grid=(bsz, seq // rws),
        in_specs=[pl.BlockSpec((1, rws, D_WIDTH), lambda b, i: (b, i, 1)),
                  pl.BlockSpec((1, rws, D_WIDTH), lambda b, i: (b, i, 2)),
                  small((D_GROUPS, D_CHUNK, D_CHUNK)), small((D_CHUNK, D_GROUPS)),
                  small((1, D_WIDTH)), small((1, D_WIDTH))],
        out_specs=pl.BlockSpec((1, rws, D_WIDTH), lambda b, i: (b, i, 0)),
        out_shape=jax.ShapeDtypeStruct((bsz, seq, D_WIDTH), BF16),
        compiler_params=_params(("parallel", "parallel")),
        name="spatial_gating",
    )(gz, gz, w_s.astype(F32), b_s.T.astype(F32), ln_g.reshape(1, D_WIDTH), ln_b.reshape(1, D_WIDTH))


def kernel(x, w_in_even, lam_q1, lam_k1, lam_q2, lam_k2, diff_subln_g, rel_bias, w_out_even,
           w_in_odd, ret_gn_g, ret_gn_b, sgu_ln_g, sgu_ln_b, sgu_w, sgu_b, w_out_odd,
           ln_mix_g, ln_mix_b, ffn_w_gate, ffn_w_up, ffn_w_down, ln_ffn_g, ln_ffn_b):
    bsz, seq, d = x.shape
    t = bsz * seq
    x2d = x.reshape(t, d)
    for l in range(DEPTH):
        j = l // 2
        if l % 2 == 0:
            hproj, vta, vtb = _proj_even(x2d, w_in_even[j].astype(BF16), bsz, seq)
            hproj = hproj.reshape(bsz, seq, -1)
            lam_init = 0.8 - 0.6 * math.exp(-0.3 * l)
            mix_a = _diff_attention(hproj, vta, lam_q1[j], lam_k1[j], lam_q2[j], lam_k2[j],
                                    diff_subln_g[j], lam_init)
            mix_b = _band_attention(hproj, vtb, rel_bias[j])
            w_out = w_out_even[j]
        else:
            qkv, gz = _proj(x2d, w_in_odd[j].astype(BF16), [2 * C_QK_W + C_W, C_W + 2 * D_WIDTH],
                            [BF16, F32])
            qkv = qkv.reshape(bsz, seq, -1)
            gz = gz.reshape(bsz, seq, -1)
            mix_a = _retention(qkv, gz, ret_gn_g[j], ret_gn_b[j])
            mix_b = _spatial_gating(gz, sgu_w[j], sgu_b[j], sgu_ln_g[j], sgu_ln_b[j])
            w_out = w_out_odd[j]
        x2d = _mix_ffn(mix_a.reshape(t, -1), mix_b.reshape(t, -1), x2d, w_out.astype(BF16),
                       ln_mix_g[l], ln_mix_b[l], ffn_w_gate[l].astype(BF16), ffn_w_up[l].astype(BF16),
                       ffn_w_down[l].astype(BF16), ln_ffn_g[l], ln_ffn_b[l])
    return x2d.reshape(bsz, seq, d)
```

```python
import functools
import math

import jax
import jax.numpy as jnp
from jax import lax
from jax.experimental import pallas as pl
from jax.experimental.pallas import tpu as pltpu

F32 = jnp.float32
BF16 = jnp.bfloat16

D_MODEL = 1024
HEAD_DIM = 64
CHUNK = 64
A_HEADS = 4
A_W = 512
B_HEADS = 8
B_W = 512
B_LEFT_CHUNKS = 8
REL_CLIP = 128
C_HEADS = 4
C_QK_W = 256
C_W = 512
D_CHUNK = 128
D_GROUPS = 4
D_WIDTH = 512
DEPTH = 2
ALPHA = (2.0 * DEPTH) ** 0.25
LN_EPS = 1e-5
NEG_INF = -1e30
LOG2E = 1.4426950408889634
SKIP_LOG2 = 160.0
NORM_SLACK = 1.0 + 2.0 ** -8

VMEM_LIMIT_V7X = 56 * 1024 * 1024
BF16_ROWS = 16

ROW_BLOCK = 512
ATT_BLOCK = 512
BAND_BLOCK = 256
BAND_WINDOW = BAND_BLOCK + B_LEFT_CHUNKS * CHUNK
RET_BLOCK = 256
SGU_ROWS = 512


def _params(semantics):
    return pltpu.CompilerParams(dimension_semantics=semantics, vmem_limit_bytes=VMEM_LIMIT_V7X)


def _const_spec(shape):
    nd = len(shape)
    return pl.BlockSpec(shape, lambda *_: (0,) * nd, pipeline_mode=pl.Buffered(1))


def _layer_norm(y, g, b):
    mu = jnp.mean(y, axis=-1, keepdims=True)
    d = y - mu
    var = jnp.mean(d * d, axis=-1, keepdims=True)
    return d * lax.rsqrt(var + LN_EPS) * g + b


def _proj_kernel(x_ref, w_ref, *out_refs):
    x = x_ref[...].astype(BF16)
    off = 0
    for o_ref in out_refs:
        n = o_ref.shape[-1]
        for c in range(0, n, 512):
            acc = jnp.dot(x, w_ref[:, off + c:off + c + 512], preferred_element_type=F32)
            o_ref[:, c:c + 512] = acc.astype(o_ref.dtype)
        off += n


def _proj(x2d, w_bf16, out_widths, out_dtypes):
    t, k = x2d.shape
    n = w_bf16.shape[1]
    assert sum(out_widths) == n and all(wd % 512 == 0 for wd in out_widths)
    return pl.pallas_call(
        _proj_kernel,
        grid=(t // ROW_BLOCK,),
        in_specs=[pl.BlockSpec((ROW_BLOCK, k), lambda i: (i, 0)), _const_spec((k, n))],
        out_specs=[pl.BlockSpec((ROW_BLOCK, wd), lambda i: (i, 0)) for wd in out_widths],
        out_shape=[jax.ShapeDtypeStruct((t, wd), dt) for wd, dt in zip(out_widths, out_dtypes)],
        compiler_params=_params(("parallel",)),
        name="in_proj",
    )(x2d, w_bf16)


def _transposed_value_tiles(vals, block):
    ones_tile = (lax.broadcasted_iota(jnp.int32, (BF16_ROWS, block), 0) == 0).astype(BF16)
    return [jnp.concatenate([vals[r:r + block].T.astype(BF16), ones_tile], axis=0)
            for r in range(0, vals.shape[0], block)]


def _proj_even_kernel(x_ref, w_ref, h_ref, vta_ref, vtb_ref):
    x = x_ref[...].astype(BF16)
    vd = 2 * HEAD_DIM
    dot = lambda c: jnp.dot(x, w_ref[:, c:c + 512], preferred_element_type=F32)
    h_ref[:, 0:512] = dot(0).astype(BF16)
    h_ref[:, 512:1024] = dot(A_W).astype(BF16)
    h_ref[:, 1024:1536] = dot(3 * A_W).astype(BF16)
    h_ref[:, 1536:2048] = dot(3 * A_W + B_W).astype(BF16)
    va = dot(2 * A_W)
    for h in range(A_HEADS):
        vta_ref[0, h, 0] = _transposed_value_tiles(va[:, h * vd:(h + 1) * vd], ATT_BLOCK)[0]
    vb = dot(3 * A_W + 2 * B_W)
    for hp in range(B_HEADS // 2):
        tiles = _transposed_value_tiles(vb[:, hp * vd:(hp + 1) * vd], BAND_BLOCK)
        for n, tile in enumerate(tiles):
            vtb_ref[0, hp, n] = tile


def _proj_even(x2d, w_bf16, bsz, seq):
    t, k = x2d.shape
    n = w_bf16.shape[1]
    assert ROW_BLOCK == ATT_BLOCK and seq % ROW_BLOCK == 0
    per_seq = seq // ROW_BLOCK
    vrows = 2 * HEAD_DIM + BF16_ROWS
    return pl.pallas_call(
        _proj_even_kernel,
        grid=(t // ROW_BLOCK,),
        in_specs=[pl.BlockSpec((ROW_BLOCK, k), lambda i: (i, 0)), _const_spec((k, n))],
        out_specs=[pl.BlockSpec((ROW_BLOCK, 2 * A_W + 2 * B_W), lambda i: (i, 0)),
                   pl.BlockSpec((1, A_HEADS, 1, vrows, ATT_BLOCK),
                                lambda i: (i // per_seq, 0, i % per_seq, 0, 0)),
                   pl.BlockSpec((1, B_HEADS // 2, ROW_BLOCK // BAND_BLOCK, vrows, BAND_BLOCK),
                                lambda i: (i // per_seq, 0, i % per_seq, 0, 0))],
        out_shape=[jax.ShapeDtypeStruct((t, 2 * A_W + 2 * B_W), BF16),
                   jax.ShapeDtypeStruct((bsz, A_HEADS, seq // ATT_BLOCK, vrows, ATT_BLOCK), BF16),
                   jax.ShapeDtypeStruct((bsz, B_HEADS // 2, seq // BAND_BLOCK, vrows, BAND_BLOCK), BF16)],
        compiler_params=_params(("parallel",)),
        name="in_proj_even",
    )(x2d, w_bf16)


def _mix_ffn_kernel(a_ref, b_ref, x_ref, wo_ref, g1_ref, b1_ref, wg_ref, wu_ref, wd_ref,
                    g2_ref, b2_ref, o_ref):
    ka = a_ref.shape[-1]
    half = x_ref.shape[0] // 2
    halves = (slice(0, half), slice(half, 2 * half))
    mix = [jnp.dot(a_ref[rows, :], wo_ref[0:ka, :], preferred_element_type=F32)
           + jnp.dot(b_ref[rows, :], wo_ref[ka:, :], preferred_element_type=F32) for rows in halves]
    x1 = [_layer_norm(ALPHA * x_ref[rows, :] + m, g1_ref[...], b1_ref[...]) for rows, m in zip(halves, mix)]
    xb = [v.astype(BF16) for v in x1]
    gate = [jnp.dot(v, wg_ref[...], preferred_element_type=F32) for v in xb]
    up = [jnp.dot(v, wu_ref[...], preferred_element_type=F32) for v in xb]
    h = [(g * jax.nn.sigmoid(g) * u).astype(BF16) for g, u in zip(gate, up)]
    y = [ALPHA * v + jnp.dot(hh, wd_ref[...], preferred_element_type=F32) for v, hh in zip(x1, h)]
    for rows, v in zip(halves, y):
        o_ref[rows, :] = _layer_norm(v, g2_ref[...], b2_ref[...])


def _mix_ffn(a, b, x2d, wo, g1, b1, wg, wu, wd, g2, b2):
    t, d = x2d.shape
    ka, kb = a.shape[1], b.shape[1]
    f = wg.shape[1]
    vec = lambda v: v.reshape(1, d)
    return pl.pallas_call(
        _mix_ffn_kernel,
        grid=(t // ROW_BLOCK,),
        in_specs=[pl.BlockSpec((ROW_BLOCK, ka), lambda i: (i, 0)),
                  pl.BlockSpec((ROW_BLOCK, kb), lambda i: (i, 0)),
                  pl.BlockSpec((ROW_BLOCK, d), lambda i: (i, 0)),
                  _const_spec((ka + kb, d)), _const_spec((1, d)), _const_spec((1, d)),
                  _const_spec((d, f)), _const_spec((d, f)), _const_spec((f, d)),
                  _const_spec((1, d)), _const_spec((1, d))],
        out_specs=pl.BlockSpec((ROW_BLOCK, d), lambda i: (i, 0)),
        out_shape=jax.ShapeDtypeStruct((t, d), F32),
        compiler_params=_params(("parallel",)),
        name="mix_ffn",
    )(a, b, x2d, wo, vec(g1), vec(b1), wg, wu, wd, vec(g2), vec(b2))


def _diff_attn_kernel(q_ref, k_ref, vt_ref, past_ref, diag_ref, past_max_ref, diag_max_ref, step_ref,
                      lq1_ref, lk1_ref, lq2_ref, lk2_ref, g_ref, o_ref,
                      qt_ref, sa_ref, sb_ref, ra_ref, rb_ref, m_ref, acc_ref, knorm_ref, *, lam_init):
    t = ATT_BLOCK
    n_chunks = t // CHUNK
    i = pl.program_id(2)
    scale = HEAD_DIM ** -0.5
    qt = q_ref[0].astype(F32).T * (scale * LOG2E)
    row = lax.broadcasted_iota(jnp.int32, qt.shape, 0)
    qt_ref[0] = jnp.where(row < HEAD_DIM, qt, 0.0).astype(BF16)
    qt_ref[1] = jnp.where(row >= HEAD_DIM, qt, 0.0).astype(BF16)

    m_ref[...] = jnp.full(m_ref.shape, NEG_INF, F32)
    acc_ref[...] = jnp.zeros(acc_ref.shape, F32)

    @pl.when(i == 0)
    def _():
        lane = lax.broadcasted_iota(jnp.int32, (t, 2 * HEAD_DIM), 1)

        def block_norms(j, carry):
            kb = k_ref[0, pl.ds(pl.multiple_of(j * t, t), t), :].astype(F32)
            sq = kb * kb
            n1 = jnp.sum(jnp.where(lane < HEAD_DIM, sq, 0.0), axis=1, keepdims=True)
            n2 = jnp.sum(jnp.where(lane >= HEAD_DIM, sq, 0.0), axis=1, keepdims=True)
            return (jnp.maximum(carry[0], jnp.max(n1, axis=0, keepdims=True)),
                    jnp.maximum(carry[1], jnp.max(n2, axis=0, keepdims=True)))

        zero = jnp.zeros((1, 1), F32)
        n1, n2 = lax.fori_loop(0, k_ref.shape[1] // t, block_norms, (zero, zero))
        knorm_ref[0] = jnp.broadcast_to(jnp.sqrt(n1), (1, t))
        knorm_ref[1] = jnp.broadcast_to(jnp.sqrt(n2), (1, t))

    def scores(j, s_ref, r_ref):
        start = pl.multiple_of(j * t, t)
        k_blk = k_ref[0, pl.ds(start, t), :]
        for mp in range(2):
            s = jnp.dot(k_blk, qt_ref[mp], preferred_element_type=F32)
            s_ref[mp] = s
            r_ref[mp] = jnp.max(s.reshape(n_chunks, CHUNK // 8, 8, t), axis=1)

    def softmax_pv(j, s_ref, r_ref, tile_ref, tile_max_ref):
        vt_blk = vt_ref[0, 0, j]
        step = step_ref[0]
        for mp in range(2):
            m_prev = m_ref[mp]
            bound = jnp.max(r_ref[mp] + tile_max_ref[0][:, None, :], axis=0)
            bound = jnp.max(bound, axis=0, keepdims=True)
            m_new = jnp.maximum(m_prev, bound)
            alpha = jnp.exp2(m_prev - m_new)
            p = jnp.exp2(s_ref[mp] + tile_ref[0] - m_new)
            acc_ref[mp] = alpha * acc_ref[mp] + jnp.dot(vt_blk, p.astype(BF16),
                                                        preferred_element_type=F32)
            m_ref[mp] = m_new + step

    def prev_block(r):
        return jnp.maximum(i - r, 0)

    scores(i, sa_ref, ra_ref)
    scores(prev_block(1), sb_ref, rb_ref)
    softmax_pv(i, sa_ref, ra_ref, diag_ref, diag_max_ref)

    tile_top = jnp.max(past_max_ref[0], axis=0, keepdims=True)
    inv_step = 1.0 / step_ref[0]
    reach = jnp.zeros((1, t), F32)
    for mp in range(2):
        qb = qt_ref[mp].astype(F32)
        qnorm = jnp.sqrt(jnp.sum(qb * qb, axis=0, keepdims=True))
        gap = qnorm * knorm_ref[mp] * NORM_SLACK + tile_top - m_ref[mp]
        reach = jnp.maximum(reach, (gap + SKIP_LOG2) * inv_step + 1.0)
    reach = jnp.floor(jnp.minimum(reach, float(k_ref.shape[1] // t))).astype(jnp.int32)
    n_keep = jnp.minimum(jnp.max(reach), i)

    def past_pair(jj, carry):
        r0 = 2 * jj + 1
        scores(prev_block(r0 + 1), sa_ref, ra_ref)
        softmax_pv(i - r0, sb_ref, rb_ref, past_ref, past_max_ref)
        scores(prev_block(r0 + 2), sb_ref, rb_ref)
        softmax_pv(i - r0 - 1, sa_ref, ra_ref, past_ref, past_max_ref)
        return carry

    lax.fori_loop(0, lax.shift_right_logical(n_keep, 1), past_pair, 0)

    @pl.when((n_keep & 1) == 1)
    def _():
        softmax_pv(i - n_keep, sb_ref, rb_ref, past_ref, past_max_ref)

    lam = (jnp.exp(jnp.sum(lq1_ref[...] * lk1_ref[...], axis=-1, keepdims=True))
           - jnp.exp(jnp.sum(lq2_ref[...] * lk2_ref[...], axis=-1, keepdims=True)) + lam_init)
    vd = 2 * HEAD_DIM
    ot = (acc_ref[0, 0:vd] / acc_ref[0, vd:vd + 1]
          - lam * (acc_ref[1, 0:vd] / acc_ref[1, vd:vd + 1]))
    ms = jnp.mean(ot * ot, axis=0, keepdims=True)
    y = (ot * lax.rsqrt(ms + LN_EPS)).T * g_ref[...] * (1.0 - lam_init)
    o_ref[0] = y.astype(o_ref.dtype)


def _diff_attention(hproj, vt, lam_q1, lam_k1, lam_q2, lam_k2, subln_g, lam_init):
    bsz, seq, _ = hproj.shape
    t = ATT_BLOCK
    nb = seq // t
    n_chunks = t // CHUNK
    vd = 2 * HEAD_DIM
    slopes = 2.0 ** (-8.0 * jnp.arange(1, A_HEADS + 1, dtype=F32) / A_HEADS)
    kk = jnp.arange(t)[:, None]
    qq = jnp.arange(t)[None, :]
    rel = (qq - kk).astype(F32)
    past = -(slopes * LOG2E)[:, None, None] * rel[None]
    allowed = (kk // CHUNK) <= (qq // CHUNK)
    diag = jnp.where(allowed[None], -(slopes * LOG2E)[:, None, None] * jnp.abs(rel)[None], NEG_INF)
    chunk_max = lambda tile: jnp.max(tile.reshape(A_HEADS, n_chunks, CHUNK, t), axis=2)
    step = jnp.broadcast_to((slopes * (LOG2E * t))[:, None, None], (A_HEADS, 1, t))
    vec = lambda v: v.reshape(1, HEAD_DIM).astype(F32)
    small = lambda shape: pl.BlockSpec(shape, lambda b, h, i: (0,) * len(shape))
    per_head = lambda shape: pl.BlockSpec(shape, lambda b, h, i: (h,) + (0,) * (len(shape) - 1))
    return pl.pallas_call(
        functools.partial(_diff_attn_kernel, lam_init=lam_init),
        grid=(bsz, A_HEADS, nb),
        in_specs=[pl.BlockSpec((1, t, vd), lambda b, h, i: (b, i, h)),
                  pl.BlockSpec((1, seq, vd), lambda b, h, i: (b, 0, A_HEADS + h)),
                  pl.BlockSpec((1, 1, nb, vd + BF16_ROWS, t), lambda b, h, i: (b, h, 0, 0, 0)),
                  per_head((1, t, t)), per_head((1, t, t)),
                  per_head((1, n_chunks, t)), per_head((1, n_chunks, t)), per_head((1, 1, t)),
                  small((1, HEAD_DIM)), small((1, HEAD_DIM)), small((1, HEAD_DIM)), small((1, HEAD_DIM)),
                  small((1, vd))],
        out_specs=pl.BlockSpec((1, t, vd), lambda b, h, i: (b, i, h)),
        out_shape=jax.ShapeDtypeStruct((bsz, seq, A_W), BF16),
        scratch_shapes=[pltpu.VMEM((2, vd, t), BF16),
                        pltpu.VMEM((2, t, t), F32), pltpu.VMEM((2, t, t), F32),
                        pltpu.VMEM((2, n_chunks, 8, t), F32), pltpu.VMEM((2, n_chunks, 8, t), F32),
                        pltpu.VMEM((2, 1, t), F32),
                        pltpu.VMEM((2, vd + BF16_ROWS, t), F32),
                        pltpu.VMEM((2, 1, t), F32)],
        compiler_params=_params(("parallel", "parallel", "arbitrary")),
        name="diff_attn",
    )(hproj, hproj, vt, past, diag, chunk_max(past), chunk_max(diag), step,
      vec(lam_q1), vec(lam_k1), vec(lam_q2), vec(lam_k2), subln_g.reshape(1, vd).astype(F32))


def _band_attn_kernel(q_ref, k_ref, vt_ref, diag_vals_ref, band_ref, o_ref, bias_ref, sa_ref, sb_ref):
    t = BAND_BLOCK
    nb = q_ref.shape[1] // t
    n_win = BAND_WINDOW // t
    vd = 2 * HEAD_DIM
    scale = HEAD_DIM ** -0.5
    row = lax.broadcasted_iota(jnp.int32, (vd, t), 0)
    n_vals = diag_vals_ref.shape[-1]

    for hh in range(2):
        vals = jnp.broadcast_to(diag_vals_ref[hh], (8, n_vals))
        for c0 in range(0, BAND_WINDOW, 8):
            rolled = pltpu.roll(vals, (c0 - (BAND_WINDOW - 1)) % n_vals, axis=1, stride=1, stride_axis=0)
            bias_ref[hh, c0:c0 + 8, :] = jnp.where(band_ref[c0:c0 + 8, :] > 0.0, rolled[:, 0:t], NEG_INF)

    def rows(n):
        return pl.ds(pl.multiple_of(n * t, t), t)

    def key_block(n, w):
        return jnp.maximum(n - (n_win - 1) + w, 0)

    def scores(n, s_ref):
        qt = q_ref[0, rows(n), :].astype(F32).T * (scale * LOG2E)
        for hh in range(2):
            in_head = (row < HEAD_DIM) if hh == 0 else (row >= HEAD_DIM)
            qh = jnp.where(in_head, qt, 0.0).astype(BF16)
            for w in range(n_win):
                s_ref[hh, w * t:(w + 1) * t] = jnp.dot(k_ref[0, rows(key_block(n, w)), :], qh,
                                                       preferred_element_type=F32)

    def softmax_pv(n, s_ref, mask_keys):
        outs = []
        for hh in range(2):
            st = s_ref[hh] + bias_ref[hh]
            if mask_keys:
                key_pos = (n - (n_win - 1)) * t + lax.broadcasted_iota(jnp.int32, (BAND_WINDOW, 1), 0)
                st = jnp.where(key_pos >= 0, st, NEG_INF)
            p = jnp.exp2(st - jnp.max(st, axis=0, keepdims=True)).astype(BF16)
            ot = jnp.dot(vt_ref[0, 0, key_block(n, 0)], p[0:t], preferred_element_type=F32)
            for w in range(1, n_win):
                ot = ot + jnp.dot(vt_ref[0, 0, key_block(n, w)], p[w * t:(w + 1) * t],
                                  preferred_element_type=F32)
            outs.append(ot[0:vd] / ot[vd:vd + 1])
        o_ref[0, rows(n), :] = jnp.where(row < HEAD_DIM, outs[0], outs[1]).T.astype(o_ref.dtype)

    for n in range(n_win - 1):
        scores(n, sa_ref)
        softmax_pv(n, sa_ref, True)

    first = n_win - 1
    scores(first, sa_ref)

    def pair(jj, carry):
        n0 = first + 2 * jj
        scores(n0 + 1, sb_ref)
        softmax_pv(n0, sa_ref, False)
        scores(n0 + 2, sa_ref)
        softmax_pv(n0 + 1, sb_ref, False)
        return carry

    n_pairs = (nb - first - 2) // 2
    lax.fori_loop(0, n_pairs, pair, 0)
    for n in range(first + 2 * n_pairs, nb):
        if n + 1 < nb:
            scores(n + 1, sb_ref)
        softmax_pv(n, sa_ref, False)
        sa_ref, sb_ref = sb_ref, sa_ref


def _band_attention(hproj, vt, rel_bias):
    bsz, seq, _ = hproj.shape
    t = BAND_BLOCK
    vd = 2 * HEAD_DIM
    c = jnp.arange(BAND_WINDOW)[:, None]
    r = jnp.arange(t)[None, :]
    in_band = ((c // CHUNK) >= (r // CHUNK)) & ((c // CHUNK) <= (r // CHUNK) + B_LEFT_CHUNKS)
    rb = rel_bias.astype(F32) * LOG2E
    n_lo = BAND_WINDOW - 1 - B_LEFT_CHUNKS * CHUNK - REL_CLIP
    n_hi = t + BAND_WINDOW - n_lo - rb.shape[1]
    diag_vals = jnp.concatenate([jnp.broadcast_to(rb[:, :1], (B_HEADS, n_lo)), rb,
                                 jnp.broadcast_to(rb[:, -1:], (B_HEADS, n_hi))], axis=1)[:, None, :]
    qcol, kcol = (2 * A_W) // vd, (2 * A_W + B_W) // vd
    seq_spec = lambda col: pl.BlockSpec((1, seq, vd), lambda b, hp: (b, 0, col + hp))
    return pl.pallas_call(
        _band_attn_kernel,
        grid=(bsz, B_HEADS // 2),
        in_specs=[seq_spec(qcol), seq_spec(kcol),
                  pl.BlockSpec((1, 1, seq // t, vd + BF16_ROWS, t), lambda b, hp: (b, hp, 0, 0, 0)),
                  pl.BlockSpec((2, 1, t + BAND_WINDOW), lambda b, hp: (hp, 0, 0)),
                  pl.BlockSpec((BAND_WINDOW, t), lambda b, hp: (0, 0))],
        out_specs=pl.BlockSpec((1, seq, vd), lambda b, hp: (b, 0, hp)),
        out_shape=jax.ShapeDtypeStruct((bsz, seq, B_W), BF16),
        scratch_shapes=[pltpu.VMEM((2, BAND_WINDOW, t), F32),
                        pltpu.VMEM((2, BAND_WINDOW, t), F32), pltpu.VMEM((2, BAND_WINDOW, t), F32)],
        compiler_params=_params(("parallel", "parallel")),
        name="band_attn",
    )(hproj, hproj, vt, diag_vals, in_band.astype(F32))


def _retention_kernel(q_ref, k_ref, v_ref, gate_ref, dmask_ref, qdec_ref, kdec_ref, cdec_ref,
                      g_ref, beta_ref, o_ref, state_ref):
    ln = RET_BLOCK
    nb = q_ref.shape[1] // ln
    vd = 2 * HEAD_DIM
    scale = HEAD_DIM ** -0.5
    lane = lax.broadcasted_iota(jnp.int32, (ln, vd), 1)
    state_ref[...] = jnp.zeros(state_ref.shape, F32)

    def block(n, carry):
        rows = pl.ds(pl.multiple_of(n * ln, ln), ln)
        q = q_ref[0, rows, :]
        k = k_ref[0, rows, :]
        for hh in range(2):
            in_head = (lane < HEAD_DIM) if hh == 0 else (lane >= HEAD_DIM)
            qh = jnp.where(in_head, q, jnp.zeros_like(q))
            kh = jnp.where(in_head, k.astype(F32) * scale, 0.0)
            sl = slice(hh * vd, (hh + 1) * vd)
            v = v_ref[0, rows, sl]
            scores = lax.dot_general(qh, kh.astype(BF16), (((1,), (1,)), ((), ())),
                                     preferred_element_type=F32) * dmask_ref[hh]
            inner = jnp.dot(scores.astype(BF16), v, preferred_element_type=F32)
            state = state_ref[hh]
            cross = jnp.dot(qh, state.astype(BF16), preferred_element_type=F32) * qdec_ref[hh]
            kdt = (kh * kdec_ref[hh]).T.astype(BF16)
            kv = jnp.dot(kdt, v, preferred_element_type=F32)
            state_ref[hh] = cdec_ref[hh] * state + kv
            o = inner + cross
            mu = jnp.mean(o, axis=-1, keepdims=True)
            d = o - mu
            var = jnp.mean(d * d, axis=-1, keepdims=True)
            y = d * lax.rsqrt(var + LN_EPS) * g_ref[:, sl] + beta_ref[:, sl]
            gate = gate_ref[0, rows, sl]
            o_ref[0, rows, sl] = (y * (gate * jax.nn.sigmoid(gate))).astype(o_ref.dtype)
        return carry

    lax.fori_loop(0, nb, block, 0, unroll=2)


def _retention(qkv, gz, gn_g, gn_b):
    bsz, seq, _ = qkv.shape
    ln = RET_BLOCK
    vd = 2 * HEAD_DIM
    log_gamma = jnp.log(1.0 - 2.0 ** (-5.0 - jnp.arange(C_HEADS, dtype=F32)))
    idx = jnp.arange(ln, dtype=F32)
    diff = idx[:, None] - idx[None, :]
    dmask = jnp.where(diff >= 0, jnp.exp(log_gamma[:, None, None] * jnp.maximum(diff, 0.0)), 0.0)
    qdec = jnp.broadcast_to(jnp.exp(log_gamma[:, None] * (idx + 1.0))[:, :, None], (C_HEADS, ln, vd))
    kdec = jnp.broadcast_to(jnp.exp(log_gamma[:, None] * (ln - 1.0 - idx))[:, :, None], (C_HEADS, ln, vd))
    cdec = jnp.broadcast_to(jnp.exp(log_gamma * ln)[:, None, None], (C_HEADS, 1, vd))
    pair = lambda shape: pl.BlockSpec(shape, lambda b, hp: (hp,) + (0,) * (len(shape) - 1))
    return pl.pallas_call(
        _retention_kernel,
        grid=(bsz, C_HEADS // 2),
        in_specs=[pl.BlockSpec((1, seq, vd), lambda b, hp: (b, 0, hp)),
                  pl.BlockSpec((1, seq, vd), lambda b, hp: (b, 0, C_QK_W // vd + hp)),
                  pl.BlockSpec((1, seq, 2 * vd), lambda b, hp: (b, 0, (2 * C_QK_W) // (2 * vd) + hp)),
                  pl.BlockSpec((1, seq, 2 * vd), lambda b, hp: (b, 0, hp)),
                  pair((2, ln, ln)), pair((2, ln, vd)), pair((2, ln, vd)), pair((2, 1, vd)),
                  pl.BlockSpec((1, 2 * vd), lambda b, hp: (0, hp)),
                  pl.BlockSpec((1, 2 * vd), lambda b, hp: (0, hp))],
        out_specs=pl.BlockSpec((1, seq, 2 * vd), lambda b, hp: (b, 0, hp)),
        out_shape=jax.ShapeDtypeStruct((bsz, seq, C_W), BF16),
        scratch_shapes=[pltpu.VMEM((2, vd, vd), F32)],
        compiler_params=_params(("parallel", "parallel")),
        name="retention",
    )(qkv, qkv, qkv, gz, dmask, qdec, kdec, cdec, gn_g.reshape(1, C_W), gn_b.reshape(1, C_W))


def _sgu_kernel(u_ref, v_ref, w_ref, bt_ref, g_ref, beta_ref, o_ref):
    u = jax.nn.gelu(u_ref[0])
    v = _layer_norm(jax.nn.gelu(v_ref[0]), g_ref[...], beta_ref[...])
    gw = D_WIDTH // D_GROUPS
    r = lax.broadcasted_iota(jnp.int32, (D_CHUNK, D_CHUNK), 0)
    c = lax.broadcasted_iota(jnp.int32, (D_CHUNK, D_CHUNK), 1)
    for g in range(D_GROUPS):
        w = jnp.where(r >= c, w_ref[g], 0.0).astype(BF16)
        bias = bt_ref[:, g:g + 1]
        cols = slice(g * gw, (g + 1) * gw)
        vg = v[:, cols].astype(BF16)
        for n in range(SGU_ROWS // D_CHUNK):
            rows = slice(n * D_CHUNK, (n + 1) * D_CHUNK)
            mixed = jnp.dot(w, vg[rows], preferred_element_type=F32) + bias
            o_ref[0, rows, cols] = (u[rows, cols] * mixed).astype(o_ref.dtype)


def _spatial_gating(gz, w_s, b_s, ln_g, ln_b):
    bsz, seq, _ = gz.shape
    rws = SGU_ROWS
    small = lambda shape: pl.BlockSpec(shape, lambda b, i: (0,) * len(shape))
    return pl.pallas_call(
        _sgu_kernel,
        grid=(bsz, seq // rws),
        in_specs=[pl.BlockSpec((1, rws, D_WIDTH), lambda b, i: (b, i, 1)),
                  pl.BlockSpec((1, rws, D_WIDTH), lambda b, i: (b, i, 2)),
                  small((D_GROUPS, D_CHUNK, D_CHUNK)), small((D_CHUNK, D_GROUPS)),
                  small((1, D_WIDTH)), small((1, D_WIDTH))],
        out_specs=pl.BlockSpec((1, rws, D_WIDTH), lambda b, i: (b, i, 0)),
        out_shape=jax.ShapeDtypeStruct((bsz, seq, D_WIDTH), BF16),
        compiler_params=_params(("parallel", "parallel")),
        name="spatial_gating",
    )(gz, gz, w_s.astype(F32), b_s.T.astype(F32), ln_g.reshape(1, D_WIDTH), ln_b.reshape(1, D_WIDTH))


def kernel(x, w_in_even, lam_q1, lam_k1, lam_q2, lam_k2, diff_subln_g, rel_bias, w_out_even,
           w_in_odd, ret_gn_g, ret_gn_b, sgu_ln_g, sgu_ln_b, sgu_w, sgu_b, w_out_odd,
           ln_mix_g, ln_mix_b, ffn_w_gate, ffn_w_up, ffn_w_down, ln_ffn_g, ln_ffn_b):
    bsz, seq, d = x.shape
    t = bsz * seq
    x2d = x.reshape(t, d)
    for l in range(DEPTH):
        j = l // 2
        if l % 2 == 0:
            hproj, vta, vtb = _proj_even(x2d, w_in_even[j].astype(BF16), bsz, seq)
            hproj = hproj.reshape(bsz, seq, -1)
            lam_init = 0.8 - 0.6 * math.exp(-0.3 * l)
            mix_a = _diff_attention(hproj, vta, lam_q1[j], lam_k1[j], lam_q2[j], lam_k2[j],
                                    diff_subln_g[j], lam_init)
            mix_b = _band_attention(hproj, vtb, rel_bias[j])
            w_out = w_out_even[j]
        else:
            qkv, gz = _proj(x2d, w_in_odd[j].astype(BF16), [2 * C_QK_W + C_W, C_W + 2 * D_WIDTH],
                            [BF16, F32])
            qkv = qkv.reshape(bsz, seq, -1)
            gz = gz.reshape(bsz, seq, -1)
            mix_a = _retention(qkv, gz, ret_gn_g[j], ret_gn_b[j])
            mix_b = _spatial_gating(gz, sgu_w[j], sgu_b[j], sgu_ln_g[j], sgu_ln_b[j])
            w_out = w_out_odd[j]
        x2d = _mix_ffn(mix_a.reshape(t, -1), mix_b.reshape(t, -1), x2d, w_out.astype(BF16),
                       ln_mix_g[l], ln_mix_b[l], ffn_w_gate[l].astype(BF16), ffn_w_up[l].astype(BF16),
                       ffn_w_down[l].astype(BF16), ln_ffn_g[l], ln_ffn_b[l])
    return x2d.reshape(bsz, seq, d)
```

```python
import functools
import math

import jax
import jax.numpy as jnp
from jax import lax
from jax.experimental import pallas as pl
from jax.experimental.pallas import tpu as pltpu

F32 = jnp.float32
BF16 = jnp.bfloat16

D_MODEL = 1024
HEAD_DIM = 64
CHUNK = 64
A_HEADS = 4
A_W = 512
B_HEADS = 8
B_W = 512
B_LEFT_CHUNKS = 8
REL_CLIP = 128
C_HEADS = 4
C_QK_W = 256
C_W = 512
D_CHUNK = 128
D_GROUPS = 4
D_WIDTH = 512
DEPTH = 2
ALPHA = (2.0 * DEPTH) ** 0.25
LN_EPS = 1e-5
NEG_INF = -1e30
LOG2E = 1.4426950408889634
SKIP_LOG2 = 160.0
NORM_SLACK = 1.0 + 2.0 ** -8

VMEM_LIMIT_V7X = 56 * 1024 * 1024
BF16_ROWS = 16

ROW_BLOCK = 512
ATT_BLOCK = 512
BAND_BLOCK = 256
BAND_WINDOW = BAND_BLOCK + B_LEFT_CHUNKS * CHUNK
RET_BLOCK = 256


def _params(semantics):
    return pltpu.CompilerParams(dimension_semantics=semantics, vmem_limit_bytes=VMEM_LIMIT_V7X)


def _const_spec(shape):
    nd = len(shape)
    return pl.BlockSpec(shape, lambda *_: (0,) * nd, pipeline_mode=pl.Buffered(1))


def _layer_norm(y, g, b):
    mu = jnp.mean(y, axis=-1, keepdims=True)
    d = y - mu
    var = jnp.mean(d * d, axis=-1, keepdims=True)
    return d * lax.rsqrt(var + LN_EPS) * g + b


def _proj_odd_kernel(x_ref, w_ref, ws_ref, bt_ref, g_ref, beta_ref, qkv_ref, gate_ref, od_ref):
    x = x_ref[...].astype(BF16)
    dot = lambda c: jnp.dot(x, w_ref[:, c:c + 512], preferred_element_type=F32)
    off_d = 2 * C_QK_W + 2 * C_W
    u = dot(off_d)
    v = dot(off_d + D_WIDTH)
    qkv_ref[:, 0:512] = dot(0).astype(BF16)
    qkv_ref[:, 512:1024] = dot(512).astype(BF16)
    gate_ref[...] = dot(2 * C_QK_W + C_W)
    u = jax.nn.gelu(u)
    v = _layer_norm(jax.nn.gelu(v), g_ref[...], beta_ref[...])
    gw = D_WIDTH // D_GROUPS
    r = lax.broadcasted_iota(jnp.int32, (D_CHUNK, D_CHUNK), 0)
    c = lax.broadcasted_iota(jnp.int32, (D_CHUNK, D_CHUNK), 1)
    for g in range(D_GROUPS):
        w = jnp.where(r >= c, ws_ref[g], 0.0).astype(BF16)
        bias = bt_ref[:, g:g + 1]
        cols = slice(g * gw, (g + 1) * gw)
        vg = v[:, cols].astype(BF16)
        for n in range(ROW_BLOCK // D_CHUNK):
            rows = slice(n * D_CHUNK, (n + 1) * D_CHUNK)
            mixed = jnp.dot(w, vg[rows], preferred_element_type=F32) + bias
            od_ref[rows, cols] = (u[rows, cols] * mixed).astype(od_ref.dtype)


def _proj_odd(x2d, w_bf16, w_s, b_s, ln_g, ln_b):
    t, k = x2d.shape
    n = w_bf16.shape[1]
    assert ROW_BLOCK % D_CHUNK == 0 and n == 2 * C_QK_W + 2 * C_W + 2 * D_WIDTH
    row_spec = lambda wd: pl.BlockSpec((ROW_BLOCK, wd), lambda i: (i, 0))
    return pl.pallas_call(
        _proj_odd_kernel,
        grid=(t // ROW_BLOCK,),
        in_specs=[row_spec(k), _const_spec((k, n)), _const_spec((D_GROUPS, D_CHUNK, D_CHUNK)),
                  _const_spec((D_CHUNK, D_GROUPS)), _const_spec((1, D_WIDTH)), _const_spec((1, D_WIDTH))],
        out_specs=[row_spec(2 * C_QK_W + C_W), row_spec(C_W), row_spec(D_WIDTH)],
        out_shape=[jax.ShapeDtypeStruct((t, 2 * C_QK_W + C_W), BF16),
                   jax.ShapeDtypeStruct((t, C_W), F32),
                   jax.ShapeDtypeStruct((t, D_WIDTH), BF16)],
        compiler_params=_params(("parallel",)),
        name="in_proj_odd",
    )(x2d, w_bf16, w_s.astype(F32), b_s.T.astype(F32), ln_g.reshape(1, D_WIDTH), ln_b.reshape(1, D_WIDTH))


def _transposed_value_tiles(vals, block):
    ones_tile = (lax.broadcasted_iota(jnp.int32, (BF16_ROWS, block), 0) == 0).astype(BF16)
    return [jnp.concatenate([vals[r:r + block].T.astype(BF16), ones_tile], axis=0)
            for r in range(0, vals.shape[0], block)]


def _proj_even_kernel(x_ref, w_ref, h_ref, vta_ref, vtb_ref):
    x = x_ref[...].astype(BF16)
    vd = 2 * HEAD_DIM
    dot = lambda c: jnp.dot(x, w_ref[:, c:c + 512], preferred_element_type=F32)
    h_ref[:, 0:512] = dot(0).astype(BF16)
    h_ref[:, 512:1024] = dot(A_W).astype(BF16)
    h_ref[:, 1024:1536] = dot(3 * A_W).astype(BF16)
    h_ref[:, 1536:2048] = dot(3 * A_W + B_W).astype(BF16)
    va = dot(2 * A_W)
    for h in range(A_HEADS):
        vta_ref[0, h, 0] = _transposed_value_tiles(va[:, h * vd:(h + 1) * vd], ATT_BLOCK)[0]
    vb = dot(3 * A_W + 2 * B_W)
    for hp in range(B_HEADS // 2):
        tiles = _transposed_value_tiles(vb[:, hp * vd:(hp + 1) * vd], BAND_BLOCK)
        for n, tile in enumerate(tiles):
            vtb_ref[0, hp, n] = tile


def _proj_even(x2d, w_bf16, bsz, seq):
    t, k = x2d.shape
    n = w_bf16.shape[1]
    assert ROW_BLOCK == ATT_BLOCK and seq % ROW_BLOCK == 0
    per_seq = seq // ROW_BLOCK
    vrows = 2 * HEAD_DIM + BF16_ROWS
    return pl.pallas_call(
        _proj_even_kernel,
        grid=(t // ROW_BLOCK,),
        in_specs=[pl.BlockSpec((ROW_BLOCK, k), lambda i: (i, 0)), _const_spec((k, n))],
        out_specs=[pl.BlockSpec((ROW_BLOCK, 2 * A_W + 2 * B_W), lambda i: (i, 0)),
                   pl.BlockSpec((1, A_HEADS, 1, vrows, ATT_BLOCK),
                                lambda i: (i // per_seq, 0, i % per_seq, 0, 0)),
                   pl.BlockSpec((1, B_HEADS // 2, ROW_BLOCK // BAND_BLOCK, vrows, BAND_BLOCK),
                                lambda i: (i // per_seq, 0, i % per_seq, 0, 0))],
        out_shape=[jax.ShapeDtypeStruct((t, 2 * A_W + 2 * B_W), BF16),
                   jax.ShapeDtypeStruct((bsz, A_HEADS, seq // ATT_BLOCK, vrows, ATT_BLOCK), BF16),
                   jax.ShapeDtypeStruct((bsz, B_HEADS // 2, seq // BAND_BLOCK, vrows, BAND_BLOCK), BF16)],
        compiler_params=_params(("parallel",)),
        name="in_proj_even",
    )(x2d, w_bf16)


def _mix_ffn_kernel(a_ref, b_ref, x_ref, wo_ref, g1_ref, b1_ref, wg_ref, wu_ref, wd_ref,
                    g2_ref, b2_ref, o_ref):
    ka = a_ref.shape[-1]
    half = x_ref.shape[0] // 2
    halves = (slice(0, half), slice(half, 2 * half))
    mix = [jnp.dot(a_ref[rows, :], wo_ref[0:ka, :], preferred_element_type=F32)
           + jnp.dot(b_ref[rows, :], wo_ref[ka:, :], preferred_element_type=F32) for rows in halves]
    x1 = [_layer_norm(ALPHA * x_ref[rows, :] + m, g1_ref[...], b1_ref[...]) for rows, m in zip(halves, mix)]
    xb = [v.astype(BF16) for v in x1]
    gate = [jnp.dot(v, wg_ref[...], preferred_element_type=F32) for v in xb]
    up = [jnp.dot(v, wu_ref[...], preferred_element_type=F32) for v in xb]
    h = [(g * jax.nn.sigmoid(g) * u).astype(BF16) for g, u in zip(gate, up)]
    y = [ALPHA * v + jnp.dot(hh, wd_ref[...], preferred_element_type=F32) for v, hh in zip(x1, h)]
    for rows, v in zip(halves, y):
        o_ref[rows, :] = _layer_norm(v, g2_ref[...], b2_ref[...])


def _mix_ffn(a, b, x2d, wo, g1, b1, wg, wu, wd, g2, b2):
    t, d = x2d.shape
    ka, kb = a.shape[1], b.shape[1]
    f = wg.shape[1]
    vec = lambda v: v.reshape(1, d)
    return pl.pallas_call(
        _mix_ffn_kernel,
        grid=(t // ROW_BLOCK,),
        in_specs=[pl.BlockSpec((ROW_BLOCK, ka), lambda i: (i, 0)),
                  pl.BlockSpec((ROW_BLOCK, kb), lambda i: (i, 0)),
                  pl.BlockSpec((ROW_BLOCK, d), lambda i: (i, 0)),
                  _const_spec((ka + kb, d)), _const_spec((1, d)), _const_spec((1, d)),
                  _const_spec((d, f)), _const_spec((d, f)), _const_spec((f, d)),
                  _const_spec((1, d)), _const_spec((1, d))],
        out_specs=pl.BlockSpec((ROW_BLOCK, d), lambda i: (i, 0)),
        out_shape=jax.ShapeDtypeStruct((t, d), F32),
        compiler_params=_params(("parallel",)),
        name="mix_ffn",
    )(a, b, x2d, wo, vec(g1), vec(b1), wg, wu, wd, vec(g2), vec(b2))


def _diff_attn_kernel(q_ref, k_ref, vt_ref, past_ref, diag_ref, past_max_ref, diag_max_ref, step_ref,
                      lq1_ref, lk1_ref, lq2_ref, lk2_ref, g_ref, o_ref,
                      qt_ref, sa_ref, sb_ref, ra_ref, rb_ref, m_ref, acc_ref, knorm_ref, *, lam_init):
    t = ATT_BLOCK
    n_chunks = t // CHUNK
    i = pl.program_id(2)
    scale = HEAD_DIM ** -0.5
    qt = q_ref[0].astype(F32).T * (scale * LOG2E)
    row = lax.broadcasted_iota(jnp.int32, qt.shape, 0)
    qt_ref[0] = jnp.where(row < HEAD_DIM, qt, 0.0).astype(BF16)
    qt_ref[1] = jnp.where(row >= HEAD_DIM, qt, 0.0).astype(BF16)

    m_ref[...] = jnp.full(m_ref.shape, NEG_INF, F32)
    acc_ref[...] = jnp.zeros(acc_ref.shape, F32)

    @pl.when(i == 0)
    def _():
        lane = lax.broadcasted_iota(jnp.int32, (t, 2 * HEAD_DIM), 1)

        def block_norms(j, carry):
            kb = k_ref[0, pl.ds(pl.multiple_of(j * t, t), t), :].astype(F32)
            sq = kb * kb
            n1 = jnp.sum(jnp.where(lane < HEAD_DIM, sq, 0.0), axis=1, keepdims=True)
            n2 = jnp.sum(jnp.where(lane >= HEAD_DIM, sq, 0.0), axis=1, keepdims=True)
            return (jnp.maximum(carry[0], jnp.max(n1, axis=0, keepdims=True)),
                    jnp.maximum(carry[1], jnp.max(n2, axis=0, keepdims=True)))

        zero = jnp.zeros((1, 1), F32)
        n1, n2 = lax.fori_loop(0, k_ref.shape[1] // t, block_norms, (zero, zero))
        knorm_ref[0] = jnp.broadcast_to(jnp.sqrt(n1), (1, t))
        knorm_ref[1] = jnp.broadcast_to(jnp.sqrt(n2), (1, t))

    def scores(j, s_ref, r_ref):
        start = pl.multiple_of(j * t, t)
        k_blk = k_ref[0, pl.ds(start, t), :]
        for mp in range(2):
            s = jnp.dot(k_blk, qt_ref[mp], preferred_element_type=F32)
            s_ref[mp] = s
            r_ref[mp] = jnp.max(s.reshape(n_chunks, CHUNK // 8, 8, t), axis=1)

    def softmax_pv(j, s_ref, r_ref, tile_ref, tile_max_ref):
        vt_blk = vt_ref[0, 0, j]
        step = step_ref[0]
        for mp in range(2):
            m_prev = m_ref[mp]
            bound = jnp.max(r_ref[mp] + tile_max_ref[0][:, None, :], axis=0)
            bound = jnp.max(bound, axis=0, keepdims=True)
            m_new = jnp.maximum(m_prev, bound)
            alpha = jnp.exp2(m_prev - m_new)
            p = jnp.exp2(s_ref[mp] + tile_ref[0] - m_new)
            acc_ref[mp] = alpha * acc_ref[mp] + jnp.dot(vt_blk, p.astype(BF16),
                                                        preferred_element_type=F32)
            m_ref[mp] = m_new + step

    def prev_block(r):
        return jnp.maximum(i - r, 0)

    scores(i, sa_ref, ra_ref)
    scores(prev_block(1), sb_ref, rb_ref)
    softmax_pv(i, sa_ref, ra_ref, diag_ref, diag_max_ref)

    tile_top = jnp.max(past_max_ref[0], axis=0, keepdims=True)
    inv_step = 1.0 / step_ref[0]
    reach = jnp.zeros((1, t), F32)
    for mp in range(2):
        qb = qt_ref[mp].astype(F32)
        qnorm = jnp.sqrt(jnp.sum(qb * qb, axis=0, keepdims=True))
        gap = qnorm * knorm_ref[mp] * NORM_SLACK + tile_top - m_ref[mp]
        reach = jnp.maximum(reach, (gap + SKIP_LOG2) * inv_step + 1.0)
    reach = jnp.floor(jnp.minimum(reach, float(k_ref.shape[1] // t))).astype(jnp.int32)
    n_keep = jnp.minimum(jnp.max(reach), i)

    def past_pair(jj, carry):
        r0 = 2 * jj + 1
        scores(prev_block(r0 + 1), sa_ref, ra_ref)
        softmax_pv(i - r0, sb_ref, rb_ref, past_ref, past_max_ref)
        scores(prev_block(r0 + 2), sb_ref, rb_ref)
        softmax_pv(i - r0 - 1, sa_ref, ra_ref, past_ref, past_max_ref)
        return carry

    lax.fori_loop(0, lax.shift_right_logical(n_keep, 1), past_pair, 0)

    @pl.when((n_keep & 1) == 1)
    def _():
        softmax_pv(i - n_keep, sb_ref, rb_ref, past_ref, past_max_ref)

    lam = (jnp.exp(jnp.sum(lq1_ref[...] * lk1_ref[...], axis=-1, keepdims=True))
           - jnp.exp(jnp.sum(lq2_ref[...] * lk2_ref[...], axis=-1, keepdims=True)) + lam_init)
    vd = 2 * HEAD_DIM
    ot = (acc_ref[0, 0:vd] / acc_ref[0, vd:vd + 1]
          - lam * (acc_ref[1, 0:vd] / acc_ref[1, vd:vd + 1]))
    ms = jnp.mean(ot * ot, axis=0, keepdims=True)
    y = (ot * lax.rsqrt(ms + LN_EPS)).T * g_ref[...] * (1.0 - lam_init)
    o_ref[0] = y.astype(o_ref.dtype)


def _diff_attention(hproj, vt, lam_q1, lam_k1, lam_q2, lam_k2, subln_g, lam_init):
    bsz, seq, _ = hproj.shape
    t = ATT_BLOCK
    nb = seq // t
    n_chunks = t // CHUNK
    vd = 2 * HEAD_DIM
    slopes = 2.0 ** (-8.0 * jnp.arange(1, A_HEADS + 1, dtype=F32) / A_HEADS)
    kk = jnp.arange(t)[:, None]
    qq = jnp.arange(t)[None, :]
    rel = (qq - kk).astype(F32)
    past = -(slopes * LOG2E)[:, None, None] * rel[None]
    allowed = (kk // CHUNK) <= (qq // CHUNK)
    diag = jnp.where(allowed[None], -(slopes * LOG2E)[:, None, None] * jnp.abs(rel)[None], NEG_INF)
    chunk_max = lambda tile: jnp.max(tile.reshape(A_HEADS, n_chunks, CHUNK, t), axis=2)
    step = jnp.broadcast_to((slopes * (LOG2E * t))[:, None, None], (A_HEADS, 1, t))
    vec = lambda v: v.reshape(1, HEAD_DIM).astype(F32)
    small = lambda shape: pl.BlockSpec(shape, lambda b, h, i: (0,) * len(shape))
    per_head = lambda shape: pl.BlockSpec(shape, lambda b, h, i: (h,) + (0,) * (len(shape) - 1))
    return pl.pallas_call(
        functools.partial(_diff_attn_kernel, lam_init=lam_init),
        grid=(bsz, A_HEADS, nb),
        in_specs=[pl.BlockSpec((1, t, vd), lambda b, h, i: (b, i, h)),
                  pl.BlockSpec((1, seq, vd), lambda b, h, i: (b, 0, A_HEADS + h)),
                  pl.BlockSpec((1, 1, nb, vd + BF16_ROWS, t), lambda b, h, i: (b, h, 0, 0, 0)),
                  per_head((1, t, t)), per_head((1, t, t)),
                  per_head((1, n_chunks, t)), per_head((1, n_chunks, t)), per_head((1, 1, t)),
                  small((1, HEAD_DIM)), small((1, HEAD_DIM)), small((1, HEAD_DIM)), small((1, HEAD_DIM)),
                  small((1, vd))],
        out_specs=pl.BlockSpec((1, t, vd), lambda b, h, i: (b, i, h)),
        out_shape=jax.ShapeDtypeStruct((bsz, seq, A_W), BF16),
        scratch_shapes=[pltpu.VMEM((2, vd, t), BF16),
                        pltpu.VMEM((2, t, t), F32), pltpu.VMEM((2, t, t), F32),
                        pltpu.VMEM((2, n_chunks, 8, t), F32), pltpu.VMEM((2, n_chunks, 8, t), F32),
                        pltpu.VMEM((2, 1, t), F32),
                        pltpu.VMEM((2, vd + BF16_ROWS, t), F32),
                        pltpu.VMEM((2, 1, t), F32)],
        compiler_params=_params(("parallel", "parallel", "arbitrary")),
        name="diff_attn",
    )(hproj, hproj, vt, past, diag, chunk_max(past), chunk_max(diag), step,
      vec(lam_q1), vec(lam_k1), vec(lam_q2), vec(lam_k2), subln_g.reshape(1, vd).astype(F32))


def _band_attn_kernel(q_ref, k_ref, vt_ref, diag_vals_ref, band_ref, o_ref, bias_ref, sa_ref, sb_ref):
    t = BAND_BLOCK
    nb = q_ref.shape[1] // t
    n_win = BAND_WINDOW // t
    vd = 2 * HEAD_DIM
    scale = HEAD_DIM ** -0.5
    row = lax.broadcasted_iota(jnp.int32, (vd, t), 0)
    n_vals = diag_vals_ref.shape[-1]

    for hh in range(2):
        vals = jnp.broadcast_to(diag_vals_ref[hh], (8, n_vals))
        for c0 in range(0, BAND_WINDOW, 8):
            rolled = pltpu.roll(vals, (c0 - (BAND_WINDOW - 1)) % n_vals, axis=1, stride=1, stride_axis=0)
            bias_ref[hh, c0:c0 + 8, :] = jnp.where(band_ref[c0:c0 + 8, :] > 0.0, rolled[:, 0:t], NEG_INF)

    def rows(n):
        return pl.ds(pl.multiple_of(n * t, t), t)

    def key_block(n, w):
        return jnp.maximum(n - (n_win - 1) + w, 0)

    def scores(n, s_ref):
        qt = q_ref[0, rows(n), :].astype(F32).T * (scale * LOG2E)
        for hh in range(2):
            in_head = (row < HEAD_DIM) if hh == 0 else (row >= HEAD_DIM)
            qh = jnp.where(in_head, qt, 0.0).astype(BF16)
            for w in range(n_win):
                s_ref[hh, w * t:(w + 1) * t] = jnp.dot(k_ref[0, rows(key_block(n, w)), :], qh,
                                                       preferred_element_type=F32)

    def softmax_pv(n, s_ref, mask_keys):
        outs = []
        for hh in range(2):
            st = s_ref[hh] + bias_ref[hh]
            if mask_keys:
                key_pos = (n - (n_win - 1)) * t + lax.broadcasted_iota(jnp.int32, (BAND_WINDOW, 1), 0)
                st = jnp.where(key_pos >= 0, st, NEG_INF)
            p = jnp.exp2(st - jnp.max(st, axis=0, keepdims=True)).astype(BF16)
            ot = jnp.dot(vt_ref[0, 0, key_block(n, 0)], p[0:t], preferred_element_type=F32)
            for w in range(1, n_win):
                ot = ot + jnp.dot(vt_ref[0, 0, key_block(n, w)], p[w * t:(w + 1) * t],
                                  preferred_element_type=F32)
            outs.append(ot[0:vd] / ot[vd:vd + 1])
        o_ref[0, rows(n), :] = jnp.where(row < HEAD_DIM, outs[0], outs[1]).T.astype(o_ref.dtype)

    for n in range(n_win - 1):
        scores(n, sa_ref)
        softmax_pv(n, sa_ref, True)

    first = n_win - 1
    scores(first, sa_ref)

    def pair(jj, carry):
        n0 = first + 2 * jj
        scores(n0 + 1, sb_ref)
        softmax_pv(n0, sa_ref, False)
        scores(n0 + 2, sa_ref)
        softmax_pv(n0 + 1, sb_ref, False)
        return carry

    n_pairs = (nb - first - 2) // 2
    lax.fori_loop(0, n_pairs, pair, 0)
    for n in range(first + 2 * n_pairs, nb):
        if n + 1 < nb:
            scores(n + 1, sb_ref)
        softmax_pv(n, sa_ref, False)
        sa_ref, sb_ref = sb_ref, sa_ref


def _band_attention(hproj, vt, rel_bias):
    bsz, seq, _ = hproj.shape
    t = BAND_BLOCK
    vd = 2 * HEAD_DIM
    c = jnp.arange(BAND_WINDOW)[:, None]
    r = jnp.arange(t)[None, :]
    in_band = ((c // CHUNK) >= (r // CHUNK)) & ((c // CHUNK) <= (r // CHUNK) + B_LEFT_CHUNKS)
    rb = rel_bias.astype(F32) * LOG2E
    n_lo = BAND_WINDOW - 1 - B_LEFT_CHUNKS * CHUNK - REL_CLIP
    n_hi = t + BAND_WINDOW - n_lo - rb.shape[1]
    diag_vals = jnp.concatenate([jnp.broadcast_to(rb[:, :1], (B_HEADS, n_lo)), rb,
                                 jnp.broadcast_to(rb[:, -1:], (B_HEADS, n_hi))], axis=1)[:, None, :]
    qcol, kcol = (2 * A_W) // vd, (2 * A_W + B_W) // vd
    seq_spec = lambda col: pl.BlockSpec((1, seq, vd), lambda b, hp: (b, 0, col + hp))
    return pl.pallas_call(
        _band_attn_kernel,
        grid=(bsz, B_HEADS // 2),
        in_specs=[seq_spec(qcol), seq_spec(kcol),
                  pl.BlockSpec((1, 1, seq // t, vd + BF16_ROWS, t), lambda b, hp: (b, hp, 0, 0, 0)),
                  pl.BlockSpec((2, 1, t + BAND_WINDOW), lambda b, hp: (hp, 0, 0)),
                  pl.BlockSpec((BAND_WINDOW, t), lambda b, hp: (0, 0))],
        out_specs=pl.BlockSpec((1, seq, vd), lambda b, hp: (b, 0, hp)),
        out_shape=jax.ShapeDtypeStruct((bsz, seq, B_W), BF16),
        scratch_shapes=[pltpu.VMEM((2, BAND_WINDOW, t), F32),
                        pltpu.VMEM((2, BAND_WINDOW, t), F32), pltpu.VMEM((2, BAND_WINDOW, t), F32)],
        compiler_params=_params(("parallel", "parallel")),
        name="band_attn",
    )(hproj, hproj, vt, diag_vals, in_band.astype(F32))


def _retention_kernel(q_ref, k_ref, v_ref, gate_ref, dmask_ref, qdec_ref, kdec_ref, cdec_ref,
                      g_ref, beta_ref, o_ref, state_ref):
    ln = RET_BLOCK
    nb = q_ref.shape[1] // ln
    vd = 2 * HEAD_DIM
    scale = HEAD_DIM ** -0.5
    lane = lax.broadcasted_iota(jnp.int32, (ln, vd), 1)
    state_ref[...] = jnp.zeros(state_ref.shape, F32)

    def block(n, carry):
        rows = pl.ds(pl.multiple_of(n * ln, ln), ln)
        q = q_ref[0, rows, :]
        k = k_ref[0, rows, :]
        for hh in range(2):
            in_head = (lane < HEAD_DIM) if hh == 0 else (lane >= HEAD_DIM)
            qh = jnp.where(in_head, q, jnp.zeros_like(q))
            kh = jnp.where(in_head, k.astype(F32) * scale, 0.0)
            sl = slice(hh * vd, (hh + 1) * vd)
            v = v_ref[0, rows, sl]
            scores = lax.dot_general(qh, kh.astype(BF16), (((1,), (1,)), ((), ())),
                                     preferred_element_type=F32) * dmask_ref[hh]
            inner = jnp.dot(scores.astype(BF16), v, preferred_element_type=F32)
            state = state_ref[hh]
            cross = jnp.dot(qh, state.astype(BF16), preferred_element_type=F32) * qdec_ref[hh]
            kdt = (kh * kdec_ref[hh]).T.astype(BF16)
            kv = jnp.dot(kdt, v, preferred_element_type=F32)
            state_ref[hh] = cdec_ref[hh] * state + kv
            o = inner + cross
            mu = jnp.mean(o, axis=-1, keepdims=True)
            d = o - mu
            var = jnp.mean(d * d, axis=-1, keepdims=True)
            y = d * lax.rsqrt(var + LN_EPS) * g_ref[:, sl] + beta_ref[:, sl]
            gate = gate_ref[0, rows, sl]
            o_ref[0, rows, sl] = (y * (gate * jax.nn.sigmoid(gate))).astype(o_ref.dtype)
        return carry

    lax.fori_loop(0, nb, block, 0, unroll=2)


def _retention(qkv, gate, gn_g, gn_b):
    bsz, seq, _ = qkv.shape
    ln = RET_BLOCK
    vd = 2 * HEAD_DIM
    log_gamma = jnp.log(1.0 - 2.0 ** (-5.0 - jnp.arange(C_HEADS, dtype=F32)))
    idx = jnp.arange(ln, dtype=F32)
    diff = idx[:, None] - idx[None, :]
    dmask = jnp.where(diff >= 0, jnp.exp(log_gamma[:, None, None] * jnp.maximum(diff, 0.0)), 0.0)
    qdec = jnp.broadcast_to(jnp.exp(log_gamma[:, None] * (idx + 1.0))[:, :, None], (C_HEADS, ln, vd))
    kdec = jnp.broadcast_to(jnp.exp(log_gamma[:, None] * (ln - 1.0 - idx))[:, :, None], (C_HEADS, ln, vd))
    cdec = jnp.broadcast_to(jnp.exp(log_gamma * ln)[:, None, None], (C_HEADS, 1, vd))
    pair = lambda shape: pl.BlockSpec(shape, lambda b, hp: (hp,) + (0,) * (len(shape) - 1))
    return pl.pallas_call(
        _retention_kernel,
        grid=(bsz, C_HEADS // 2),
        in_specs=[pl.BlockSpec((1, seq, vd), lambda b, hp: (b, 0, hp)),
                  pl.BlockSpec((1, seq, vd), lambda b, hp: (b, 0, C_QK_W // vd + hp)),
                  pl.BlockSpec((1, seq, 2 * vd), lambda b, hp: (b, 0, (2 * C_QK_W) // (2 * vd) + hp)),
                  pl.BlockSpec((1, seq, 2 * vd), lambda b, hp: (b, 0, hp)),
                  pair((2, ln, ln)), pair((2, ln, vd)), pair((2, ln, vd)), pair((2, 1, vd)),
                  pl.BlockSpec((1, 2 * vd), lambda b, hp: (0, hp)),
                  pl.BlockSpec((1, 2 * vd), lambda b, hp: (0, hp))],
        out_specs=pl.BlockSpec((1, seq, 2 * vd), lambda b, hp: (b, 0, hp)),
        out_shape=jax.ShapeDtypeStruct((bsz, seq, C_W), BF16),
        scratch_shapes=[pltpu.VMEM((2, vd, vd), F32)],
        compiler_params=_params(("parallel", "parallel")),
        name="retention",
    )(qkv, qkv, qkv, gate, dmask, qdec, kdec, cdec, gn_g.reshape(1, C_W), gn_b.reshape(1, C_W))


def kernel(x, w_in_even, lam_q1, lam_k1, lam_q2, lam_k2, diff_subln_g, rel_bias, w_out_even,
           w_in_odd, ret_gn_g, ret_gn_b, sgu_ln_g, sgu_ln_b, sgu_w, sgu_b, w_out_odd,
           ln_mix_g, ln_mix_b, ffn_w_gate, ffn_w_up, ffn_w_down, ln_ffn_g, ln_ffn_b):
    bsz, seq, d = x.shape
    t = bsz * seq
    x2d = x.reshape(t, d)
    for l in range(DEPTH):
        j = l // 2
        if l % 2 == 0:
            hproj, vta, vtb = _proj_even(x2d, w_in_even[j].astype(BF16), bsz, seq)
            hproj = hproj.reshape(bsz, seq, -1)
            lam_init = 0.8 - 0.6 * math.exp(-0.3 * l)
            mix_a = _diff_attention(hproj, vta, lam_q1[j], lam_k1[j], lam_q2[j], lam_k2[j],
                                    diff_subln_g[j], lam_init)
            mix_b = _band_attention(hproj, vtb, rel_bias[j])
            w_out = w_out_even[j]
        else:
            qkv, gate, mix_b = _proj_odd(x2d, w_in_odd[j].astype(BF16), sgu_w[j], sgu_b[j],
                                         sgu_ln_g[j], sgu_ln_b[j])
            mix_a = _retention(qkv.reshape(bsz, seq, -1), gate.reshape(bsz, seq, -1),
                               ret_gn_g[j], ret_gn_b[j])
            w_out = w_out_odd[j]
        x2d = _mix_ffn(mix_a.reshape(t, -1), mix_b.reshape(t, -1), x2d, w_out.astype(BF16),
                       ln_mix_g[l], ln_mix_b[l], ffn_w_gate[l].astype(BF16), ffn_w_up[l].astype(BF16),
                       ffn_w_down[l].astype(BF16), ln_ffn_g[l], ln_ffn_b[l])
    return x2d.reshape(bsz, seq, d)
```

```python
import functools
import math

import jax
import jax.numpy as jnp
from jax import lax
from jax.experimental import pallas as pl
from jax.experimental.pallas import tpu as pltpu

F32 = jnp.float32
BF16 = jnp.bfloat16

D_MODEL = 1024
HEAD_DIM = 64
CHUNK = 64
A_HEADS = 4
A_W = 512
B_HEADS = 8
B_W = 512
B_LEFT_CHUNKS = 8
REL_CLIP = 128
C_HEADS = 4
C_QK_W = 256
C_W = 512
D_CHUNK = 128
D_GROUPS = 4
D_WIDTH = 512
DEPTH = 2
ALPHA = (2.0 * DEPTH) ** 0.25
LN_EPS = 1e-5
NEG_INF = -1e30
LOG2E = 1.4426950408889634
SKIP_LOG2 = 160.0
NORM_SLACK = 1.0 + 2.0 ** -8

VMEM_LIMIT_V7X = 56 * 1024 * 1024
BF16_ROWS = 16

ROW_BLOCK = 512
ATT_BLOCK = 512
BAND_BLOCK = 256
BAND_WINDOW = BAND_BLOCK + B_LEFT_CHUNKS * CHUNK
RET_BLOCK = 256


def _params(semantics):
    return pltpu.CompilerParams(dimension_semantics=semantics, vmem_limit_bytes=VMEM_LIMIT_V7X)


def _const_spec(shape):
    nd = len(shape)
    return pl.BlockSpec(shape, lambda *_: (0,) * nd, pipeline_mode=pl.Buffered(1))


def _layer_norm(y, g, b):
    mu = jnp.mean(y, axis=-1, keepdims=True)
    d = y - mu
    var = jnp.mean(d * d, axis=-1, keepdims=True)
    return d * lax.rsqrt(var + LN_EPS) * g + b


def _proj_odd_kernel(x_ref, w_ref, ws_ref, bt_ref, g_ref, beta_ref, qkv_ref, gate_ref, od_ref):
    x = x_ref[...].astype(BF16)
    dot = lambda c: jnp.dot(x, w_ref[:, c:c + 512], preferred_element_type=F32)
    off_d = 2 * C_QK_W + 2 * C_W
    u = dot(off_d)
    v = dot(off_d + D_WIDTH)
    qkv_ref[:, 0:512] = dot(0).astype(BF16)
    qkv_ref[:, 512:1024] = dot(512).astype(BF16)
    gate_ref[...] = dot(2 * C_QK_W + C_W)
    u = jax.nn.gelu(u)
    v = _layer_norm(jax.nn.gelu(v), g_ref[...], beta_ref[...])
    gw = D_WIDTH // D_GROUPS
    r = lax.broadcasted_iota(jnp.int32, (D_CHUNK, D_CHUNK), 0)
    c = lax.broadcasted_iota(jnp.int32, (D_CHUNK, D_CHUNK), 1)
    for g in range(D_GROUPS):
        w = jnp.where(r >= c, ws_ref[g], 0.0).astype(BF16)
        bias = bt_ref[:, g:g + 1]
        cols = slice(g * gw, (g + 1) * gw)
        vg = v[:, cols].astype(BF16)
        for n in range(ROW_BLOCK // D_CHUNK):
            rows = slice(n * D_CHUNK, (n + 1) * D_CHUNK)
            mixed = jnp.dot(w, vg[rows], preferred_element_type=F32) + bias
            od_ref[rows, cols] = (u[rows, cols] * mixed).astype(od_ref.dtype)


def _proj_odd(x2d, w_bf16, w_s, b_s, ln_g, ln_b):
    t, k = x2d.shape
    n = w_bf16.shape[1]
    assert ROW_BLOCK % D_CHUNK == 0 and n == 2 * C_QK_W + 2 * C_W + 2 * D_WIDTH
    row_spec = lambda wd: pl.BlockSpec((ROW_BLOCK, wd), lambda i: (i, 0))
    return pl.pallas_call(
        _proj_odd_kernel,
        grid=(t // ROW_BLOCK,),
        in_specs=[row_spec(k), _const_spec((k, n)), _const_spec((D_GROUPS, D_CHUNK, D_CHUNK)),
                  _const_spec((D_CHUNK, D_GROUPS)), _const_spec((1, D_WIDTH)), _const_spec((1, D_WIDTH))],
        out_specs=[row_spec(2 * C_QK_W + C_W), row_spec(C_W), row_spec(D_WIDTH)],
        out_shape=[jax.ShapeDtypeStruct((t, 2 * C_QK_W + C_W), BF16),
                   jax.ShapeDtypeStruct((t, C_W), F32),
                   jax.ShapeDtypeStruct((t, D_WIDTH), BF16)],
        compiler_params=_params(("parallel",)),
        name="in_proj_odd",
    )(x2d, w_bf16, w_s.astype(F32), b_s.T.astype(F32), ln_g.reshape(1, D_WIDTH), ln_b.reshape(1, D_WIDTH))


def _transposed_value_tiles(vals, block):
    ones_tile = (lax.broadcasted_iota(jnp.int32, (BF16_ROWS, block), 0) == 0).astype(BF16)
    return [jnp.concatenate([vals[r:r + block].T.astype(BF16), ones_tile], axis=0)
            for r in range(0, vals.shape[0], block)]


def _transposed_query_tiles(q, block):
    tiles = []
    for r in range(0, q.shape[0], block):
        qt = q[r:r + block].T * (HEAD_DIM ** -0.5 * LOG2E)
        row = lax.broadcasted_iota(jnp.int32, qt.shape, 0)
        tiles.append((jnp.where(row < HEAD_DIM, qt, 0.0).astype(BF16),
                      jnp.where(row >= HEAD_DIM, qt, 0.0).astype(BF16)))
    return tiles


def _proj_even_kernel(x_ref, w_ref, k_ref, qta_ref, vta_ref, qtb_ref, vtb_ref):
    x = x_ref[...].astype(BF16)
    vd = 2 * HEAD_DIM
    dot = lambda c: jnp.dot(x, w_ref[:, c:c + 512], preferred_element_type=F32)
    k_ref[:, 0:512] = dot(A_W).astype(BF16)
    k_ref[:, 512:1024] = dot(3 * A_W + B_W).astype(BF16)
    qa = dot(0)
    for h in range(A_HEADS):
        tile0, tile1 = _transposed_query_tiles(qa[:, h * vd:(h + 1) * vd], ATT_BLOCK)[0]
        qta_ref[0, h, 0, 0] = tile0
        qta_ref[0, h, 0, 1] = tile1
    qb = dot(3 * A_W)
    for hp in range(B_HEADS // 2):
        for n, (tile0, tile1) in enumerate(_transposed_query_tiles(qb[:, hp * vd:(hp + 1) * vd], BAND_BLOCK)):
            qtb_ref[0, hp, n, 0] = tile0
            qtb_ref[0, hp, n, 1] = tile1
    va = dot(2 * A_W)
    for h in range(A_HEADS):
        vta_ref[0, h, 0] = _transposed_value_tiles(va[:, h * vd:(h + 1) * vd], ATT_BLOCK)[0]
    vb = dot(3 * A_W + 2 * B_W)
    for hp in range(B_HEADS // 2):
        tiles = _transposed_value_tiles(vb[:, hp * vd:(hp + 1) * vd], BAND_BLOCK)
        for n, tile in enumerate(tiles):
            vtb_ref[0, hp, n] = tile


def _proj_even(x2d, w_bf16, bsz, seq):
    t, k = x2d.shape
    n = w_bf16.shape[1]
    assert ROW_BLOCK == ATT_BLOCK and seq % ROW_BLOCK == 0
    per_seq = seq // ROW_BLOCK
    vd = 2 * HEAD_DIM
    vrows = vd + BF16_ROWS
    nba, nbb, per_step_b = seq // ATT_BLOCK, seq // BAND_BLOCK, ROW_BLOCK // BAND_BLOCK
    seq_block = lambda i: (i // per_seq, 0, i % per_seq, 0, 0)
    seq_block6 = lambda i: (i // per_seq, 0, i % per_seq, 0, 0, 0)
    return pl.pallas_call(
        _proj_even_kernel,
        grid=(t // ROW_BLOCK,),
        in_specs=[pl.BlockSpec((ROW_BLOCK, k), lambda i: (i, 0)), _const_spec((k, n))],
        out_specs=[pl.BlockSpec((ROW_BLOCK, A_W + B_W), lambda i: (i, 0)),
                   pl.BlockSpec((1, A_HEADS, 1, 2, vd, ATT_BLOCK), seq_block6),
                   pl.BlockSpec((1, A_HEADS, 1, vrows, ATT_BLOCK), seq_block),
                   pl.BlockSpec((1, B_HEADS // 2, per_step_b, 2, vd, BAND_BLOCK), seq_block6),
                   pl.BlockSpec((1, B_HEADS // 2, per_step_b, vrows, BAND_BLOCK), seq_block)],
        out_shape=[jax.ShapeDtypeStruct((t, A_W + B_W), BF16),
                   jax.ShapeDtypeStruct((bsz, A_HEADS, nba, 2, vd, ATT_BLOCK), BF16),
                   jax.ShapeDtypeStruct((bsz, A_HEADS, nba, vrows, ATT_BLOCK), BF16),
                   jax.ShapeDtypeStruct((bsz, B_HEADS // 2, nbb, 2, vd, BAND_BLOCK), BF16),
                   jax.ShapeDtypeStruct((bsz, B_HEADS // 2, nbb, vrows, BAND_BLOCK), BF16)],
        compiler_params=_params(("parallel",)),
        name="in_proj_even",
    )(x2d, w_bf16)


def _mix_ffn_kernel(a_ref, b_ref, x_ref, wo_ref, g1_ref, b1_ref, wg_ref, wu_ref, wd_ref,
                    g2_ref, b2_ref, o_ref):
    ka = a_ref.shape[-1]
    half = x_ref.shape[0] // 2
    halves = (slice(0, half), slice(half, 2 * half))
    mix = [jnp.dot(a_ref[rows, :], wo_ref[0:ka, :], preferred_element_type=F32)
           + jnp.dot(b_ref[rows, :], wo_ref[ka:, :], preferred_element_type=F32) for rows in halves]
    x1 = [_layer_norm(ALPHA * x_ref[rows, :] + m, g1_ref[...], b1_ref[...]) for rows, m in zip(halves, mix)]
    xb = [v.astype(BF16) for v in x1]
    gate = [jnp.dot(v, wg_ref[...], preferred_element_type=F32) for v in xb]
    up = [jnp.dot(v, wu_ref[...], preferred_element_type=F32) for v in xb]
    h = [(g * jax.nn.sigmoid(g) * u).astype(BF16) for g, u in zip(gate, up)]
    y = [ALPHA * v + jnp.dot(hh, wd_ref[...], preferred_element_type=F32) for v, hh in zip(x1, h)]
    for rows, v in zip(halves, y):
        o_ref[rows, :] = _layer_norm(v, g2_ref[...], b2_ref[...])


def _mix_ffn(a, b, x2d, wo, g1, b1, wg, wu, wd, g2, b2):
    t, d = x2d.shape
    ka, kb = a.shape[1], b.shape[1]
    f = wg.shape[1]
    vec = lambda v: v.reshape(1, d)
    return pl.pallas_call(
        _mix_ffn_kernel,
        grid=(t // ROW_BLOCK,),
        in_specs=[pl.BlockSpec((ROW_BLOCK, ka), lambda i: (i, 0)),
                  pl.BlockSpec((ROW_BLOCK, kb), lambda i: (i, 0)),
                  pl.BlockSpec((ROW_BLOCK, d), lambda i: (i, 0)),
                  _const_spec((ka + kb, d)), _const_spec((1, d)), _const_spec((1, d)),
                  _const_spec((d, f)), _const_spec((d, f)), _const_spec((f, d)),
                  _const_spec((1, d)), _const_spec((1, d))],
        out_specs=pl.BlockSpec((ROW_BLOCK, d), lambda i: (i, 0)),
        out_shape=jax.ShapeDtypeStruct((t, d), F32),
        compiler_params=_params(("parallel",)),
        name="mix_ffn",
    )(a, b, x2d, wo, vec(g1), vec(b1), wg, wu, wd, vec(g2), vec(b2))


def _diff_attn_kernel(qt_ref, k_ref, vt_ref, past_ref, diag_ref, past_max_ref, diag_max_ref, step_ref,
                      lq1_ref, lk1_ref, lq2_ref, lk2_ref, g_ref, o_ref,
                      sa_ref, sb_ref, ra_ref, rb_ref, m_ref, acc_ref, knorm_ref, *, lam_init):
    t = ATT_BLOCK
    n_chunks = t // CHUNK
    i = pl.program_id(2)
    qt_ref = qt_ref.at[0, 0, 0]

    m_ref[...] = jnp.full(m_ref.shape, NEG_INF, F32)
    acc_ref[...] = jnp.zeros(acc_ref.shape, F32)

    @pl.when(i == 0)
    def _():
        lane = lax.broadcasted_iota(jnp.int32, (t, 2 * HEAD_DIM), 1)

        def block_norms(j, carry):
            kb = k_ref[0, pl.ds(pl.multiple_of(j * t, t), t), :].astype(F32)
            sq = kb * kb
            n1 = jnp.sum(jnp.where(lane < HEAD_DIM, sq, 0.0), axis=1, keepdims=True)
            n2 = jnp.sum(jnp.where(lane >= HEAD_DIM, sq, 0.0), axis=1, keepdims=True)
            return (jnp.maximum(carry[0], jnp.max(n1, axis=0, keepdims=True)),
                    jnp.maximum(carry[1], jnp.max(n2, axis=0, keepdims=True)))

        zero = jnp.zeros((1, 1), F32)
        n1, n2 = lax.fori_loop(0, k_ref.shape[1] // t, block_norms, (zero, zero))
        knorm_ref[0] = jnp.broadcast_to(jnp.sqrt(n1), (1, t))
        knorm_ref[1] = jnp.broadcast_to(jnp.sqrt(n2), (1, t))

    def scores(j, s_ref, r_ref):
        start = pl.multiple_of(j * t, t)
        k_blk = k_ref[0, pl.ds(start, t), :]
        for mp in range(2):
            s = jnp.dot(k_blk, qt_ref[mp], preferred_element_type=F32)
            s_ref[mp] = s
            r_ref[mp] = jnp.max(s.reshape(n_chunks, CHUNK // 8, 8, t), axis=1)

    def softmax_pv(j, s_ref, r_ref, tile_ref, tile_max_ref):
        vt_blk = vt_ref[0, 0, j]
        step = step_ref[0]
        for mp in range(2):
            m_prev = m_ref[mp]
            bound = jnp.max(r_ref[mp] + tile_max_ref[0][:, None, :], axis=0)
            bound = jnp.max(bound, axis=0, keepdims=True)
            m_new = jnp.maximum(m_prev, bound)
            alpha = jnp.exp2(m_prev - m_new)
            p = jnp.exp2(s_ref[mp] + tile_ref[0] - m_new)
            acc_ref[mp] = alpha * acc_ref[mp] + jnp.dot(vt_blk, p.astype(BF16),
                                                        preferred_element_type=F32)
            m_ref[mp] = m_new + step

    def prev_block(r):
        return jnp.maximum(i - r, 0)

    scores(i, sa_ref, ra_ref)
    scores(prev_block(1), sb_ref, rb_ref)
    softmax_pv(i, sa_ref, ra_ref, diag_ref, diag_max_ref)

    tile_top = jnp.max(past_max_ref[0], axis=0, keepdims=True)
    inv_step = 1.0 / step_ref[0]
    reach = jnp.zeros((1, t), F32)
    for mp in range(2):
        qb = qt_ref[mp].astype(F32)
        qnorm = jnp.sqrt(jnp.sum(qb * qb, axis=0, keepdims=True))
        gap = qnorm * knorm_ref[mp] * NORM_SLACK + tile_top - m_ref[mp]
        reach = jnp.maximum(reach, (gap + SKIP_LOG2) * inv_step + 1.0)
    reach = jnp.floor(jnp.minimum(reach, float(k_ref.shape[1] // t))).astype(jnp.int32)
    n_keep = jnp.minimum(jnp.max(reach), i)

    def past_pair(jj, carry):
        r0 = 2 * jj + 1
        scores(prev_block(r0 + 1), sa_ref, ra_ref)
        softmax_pv(i - r0, sb_ref, rb_ref, past_ref, past_max_ref)
        scores(prev_block(r0 + 2), sb_ref, rb_ref)
        softmax_pv(i - r0 - 1, sa_ref, ra_ref, past_ref, past_max_ref)
        return carry

    lax.fori_loop(0, lax.shift_right_logical(n_keep, 1), past_pair, 0)

    @pl.when((n_keep & 1) == 1)
    def _():
        softmax_pv(i - n_keep, sb_ref, rb_ref, past_ref, past_max_ref)

    lam = (jnp.exp(jnp.sum(lq1_ref[...] * lk1_ref[...], axis=-1, keepdims=True))
           - jnp.exp(jnp.sum(lq2_ref[...] * lk2_ref[...], axis=-1, keepdims=True)) + lam_init)
    vd = 2 * HEAD_DIM
    ot = (acc_ref[0, 0:vd] / acc_ref[0, vd:vd + 1]
          - lam * (acc_ref[1, 0:vd] / acc_ref[1, vd:vd + 1]))
    ms = jnp.mean(ot * ot, axis=0, keepdims=True)
    y = (ot * lax.rsqrt(ms + LN_EPS)).T * g_ref[...] * (1.0 - lam_init)
    o_ref[0] = y.astype(o_ref.dtype)


def _diff_attention(keys, qt, vt, lam_q1, lam_k1, lam_q2, lam_k2, subln_g, lam_init):
    bsz, seq, _ = keys.shape
    t = ATT_BLOCK
    nb = seq // t
    n_chunks = t // CHUNK
    vd = 2 * HEAD_DIM
    slopes = 2.0 ** (-8.0 * jnp.arange(1, A_HEADS + 1, dtype=F32) / A_HEADS)
    kk = jnp.arange(t)[:, None]
    qq = jnp.arange(t)[None, :]
    rel = (qq - kk).astype(F32)
    past = -(slopes * LOG2E)[:, None, None] * rel[None]
    allowed = (kk // CHUNK) <= (qq // CHUNK)
    diag = jnp.where(allowed[None], -(slopes * LOG2E)[:, None, None] * jnp.abs(rel)[None], NEG_INF)
    chunk_max = lambda tile: jnp.max(tile.reshape(A_HEADS, n_chunks, CHUNK, t), axis=2)
    step = jnp.broadcast_to((slopes * (LOG2E * t))[:, None, None], (A_HEADS, 1, t))
    vec = lambda v: v.reshape(1, HEAD_DIM).astype(F32)
    small = lambda shape: pl.BlockSpec(shape, lambda b, h, i: (0,) * len(shape))
    per_head = lambda shape: pl.BlockSpec(shape, lambda b, h, i: (h,) + (0,) * (len(shape) - 1))
    return pl.pallas_call(
        functools.partial(_diff_attn_kernel, lam_init=lam_init),
        grid=(bsz, A_HEADS, nb),
        in_specs=[pl.BlockSpec((1, 1, 1, 2, vd, t), lambda b, h, i: (b, h, i, 0, 0, 0)),
                  pl.BlockSpec((1, seq, vd), lambda b, h, i: (b, 0, h)),
                  pl.BlockSpec((1, 1, nb, vd + BF16_ROWS, t), lambda b, h, i: (b, h, 0, 0, 0)),
                  per_head((1, t, t)), per_head((1, t, t)),
                  per_head((1, n_chunks, t)), per_head((1, n_chunks, t)), per_head((1, 1, t)),
                  small((1, HEAD_DIM)), small((1, HEAD_DIM)), small((1, HEAD_DIM)), small((1, HEAD_DIM)),
                  small((1, vd))],
        out_specs=pl.BlockSpec((1, t, vd), lambda b, h, i: (b, i, h)),
        out_shape=jax.ShapeDtypeStruct((bsz, seq, A_W), BF16),
        scratch_shapes=[pltpu.VMEM((2, t, t), F32), pltpu.VMEM((2, t, t), F32),
                        pltpu.VMEM((2, n_chunks, 8, t), F32), pltpu.VMEM((2, n_chunks, 8, t), F32),
                        pltpu.VMEM((2, 1, t), F32),
                        pltpu.VMEM((2, vd + BF16_ROWS, t), F32),
                        pltpu.VMEM((2, 1, t), F32)],
        compiler_params=_params(("parallel", "parallel", "arbitrary")),
        name="diff_attn",
    )(qt, keys, vt, past, diag, chunk_max(past), chunk_max(diag), step,
      vec(lam_q1), vec(lam_k1), vec(lam_q2), vec(lam_k2), subln_g.reshape(1, vd).astype(F32))


def _band_attn_kernel(qt_ref, k_ref, vt_ref, diag_vals_ref, band_ref, o_ref, bias_ref, sa_ref, sb_ref):
    t = BAND_BLOCK
    nb = k_ref.shape[1] // t
    n_win = BAND_WINDOW // t
    vd = 2 * HEAD_DIM
    row = lax.broadcasted_iota(jnp.int32, (vd, t), 0)
    n_vals = diag_vals_ref.shape[-1]

    for hh in range(2):
        vals = jnp.broadcast_to(diag_vals_ref[hh], (8, n_vals))
        for c0 in range(0, BAND_WINDOW, 8):
            rolled = pltpu.roll(vals, (c0 - (BAND_WINDOW - 1)) % n_vals, axis=1, stride=1, stride_axis=0)
            bias_ref[hh, c0:c0 + 8, :] = jnp.where(band_ref[c0:c0 + 8, :] > 0.0, rolled[:, 0:t], NEG_INF)

    def rows(n):
        return pl.ds(pl.multiple_of(n * t, t), t)

    def key_block(n, w):
        return jnp.maximum(n - (n_win - 1) + w, 0)

    def scores(n, s_ref):
        for hh in range(2):
            qh = qt_ref[0, 0, n, hh]
            for w in range(n_win):
                s_ref[hh, w * t:(w + 1) * t] = jnp.dot(k_ref[0, rows(key_block(n, w)), :], qh,
                                                       preferred_element_type=F32)

    def softmax_pv(n, s_ref, mask_keys):
        outs = []
        for hh in range(2):
            st = s_ref[hh] + bias_ref[hh]
            if mask_keys:
                key_pos = (n - (n_win - 1)) * t + lax.broadcasted_iota(jnp.int32, (BAND_WINDOW, 1), 0)
                st = jnp.where(key_pos >= 0, st, NEG_INF)
            p = jnp.exp2(st - jnp.max(st, axis=0, keepdims=True)).astype(BF16)
            ot = jnp.dot(vt_ref[0, 0, key_block(n, 0)], p[0:t], preferred_element_type=F32)
            for w in range(1, n_win):
                ot = ot + jnp.dot(vt_ref[0, 0, key_block(n, w)], p[w * t:(w + 1) * t],
                                  preferred_element_type=F32)
            outs.append(ot[0:vd] / ot[vd:vd + 1])
        o_ref[0, rows(n), :] = jnp.where(row < HEAD_DIM, outs[0], outs[1]).T.astype(o_ref.dtype)

    for n in range(n_win - 1):
        scores(n, sa_ref)
        softmax_pv(n, sa_ref, True)

    first = n_win - 1
    scores(first, sa_ref)

    def pair(jj, carry):
        n0 = first + 2 * jj
        scores(n0 + 1, sb_ref)
        softmax_pv(n0, sa_ref, False)
        scores(n0 + 2, sa_ref)
        softmax_pv(n0 + 1, sb_ref, False)
        return carry

    n_pairs = (nb - first - 2) // 2
    lax.fori_loop(0, n_pairs, pair, 0)
    for n in range(first + 2 * n_pairs, nb):
        if n + 1 < nb:
            scores(n + 1, sb_ref)
        softmax_pv(n, sa_ref, False)
        sa_ref, sb_ref = sb_ref, sa_ref


def _band_attention(keys, qt, vt, rel_bias):
    bsz, seq, _ = keys.shape
    t = BAND_BLOCK
    vd = 2 * HEAD_DIM
    c = jnp.arange(BAND_WINDOW)[:, None]
    r = jnp.arange(t)[None, :]
    in_band = ((c // CHUNK) >= (r // CHUNK)) & ((c // CHUNK) <= (r // CHUNK) + B_LEFT_CHUNKS)
    rb = rel_bias.astype(F32) * LOG2E
    n_lo = BAND_WINDOW - 1 - B_LEFT_CHUNKS * CHUNK - REL_CLIP
    n_hi = t + BAND_WINDOW - n_lo - rb.shape[1]
    diag_vals = jnp.concatenate([jnp.broadcast_to(rb[:, :1], (B_HEADS, n_lo)), rb,
                                 jnp.broadcast_to(rb[:, -1:], (B_HEADS, n_hi))], axis=1)[:, None, :]
    kcol = A_W // vd
    return pl.pallas_call(
        _band_attn_kernel,
        grid=(bsz, B_HEADS // 2),
        in_specs=[pl.BlockSpec((1, 1, seq // t, 2, vd, t), lambda b, hp: (b, hp, 0, 0, 0, 0)),
                  pl.BlockSpec((1, seq, vd), lambda b, hp: (b, 0, kcol + hp)),
                  pl.BlockSpec((1, 1, seq // t, vd + BF16_ROWS, t), lambda b, hp: (b, hp, 0, 0, 0)),
                  pl.BlockSpec((2, 1, t + BAND_WINDOW), lambda b, hp: (hp, 0, 0)),
                  pl.BlockSpec((BAND_WINDOW, t), lambda b, hp: (0, 0))],
        out_specs=pl.BlockSpec((1, seq, vd), lambda b, hp: (b, 0, hp)),
        out_shape=jax.ShapeDtypeStruct((bsz, seq, B_W), BF16),
        scratch_shapes=[pltpu.VMEM((2, BAND_WINDOW, t), F32),
                        pltpu.VMEM((2, BAND_WINDOW, t), F32), pltpu.VMEM((2, BAND_WINDOW, t), F32)],
        compiler_params=_params(("parallel", "parallel")),
        name="band_attn",
    )(qt, keys, vt, diag_vals, in_band.astype(F32))


def _retention_kernel(q_ref, k_ref, v_ref, gate_ref, dmask_ref, qdec_ref, kdec_ref, cdec_ref,
                      g_ref, beta_ref, o_ref, state_ref):
    ln = RET_BLOCK
    nb = q_ref.shape[1] // ln
    vd = 2 * HEAD_DIM
    scale = HEAD_DIM ** -0.5
    lane = lax.broadcasted_iota(jnp.int32, (ln, vd), 1)
    state_ref[...] = jnp.zeros(state_ref.shape, F32)

    def block(n, carry):
        rows = pl.ds(pl.multiple_of(n * ln, ln), ln)
        q = q_ref[0, rows, :]
        k = k_ref[0, rows, :]
        for hh in range(2):
            in_head = (lane < HEAD_DIM) if hh == 0 else (lane >= HEAD_DIM)
            qh = jnp.where(in_head, q, jnp.zeros_like(q))
            kh = jnp.where(in_head, k.astype(F32) * scale, 0.0)
            sl = slice(hh * vd, (hh + 1) * vd)
            v = v_ref[0, rows, sl]
            scores = lax.dot_general(qh, kh.astype(BF16), (((1,), (1,)), ((), ())),
                                     preferred_element_type=F32) * dmask_ref[hh]
            inner = jnp.dot(scores.astype(BF16), v, preferred_element_type=F32)
            state = state_ref[hh]
            cross = jnp.dot(qh, state.astype(BF16), preferred_element_type=F32) * qdec_ref[hh]
            kdt = (kh * kdec_ref[hh]).T.astype(BF16)
            kv = jnp.dot(kdt, v, preferred_element_type=F32)
            state_ref[hh] = cdec_ref[hh] * state + kv
            o = inner + cross
            mu = jnp.mean(o, axis=-1, keepdims=True)
            d = o - mu
            var = jnp.mean(d * d, axis=-1, keepdims=True)
            y = d * lax.rsqrt(var + LN_EPS) * g_ref[:, sl] + beta_ref[:, sl]
            gate = gate_ref[0, rows, sl]
            o_ref[0, rows, sl] = (y * (gate * jax.nn.sigmoid(gate))).astype(o_ref.dtype)
        return carry

    lax.fori_loop(0, nb, block, 0, unroll=2)


def _retention(qkv, gate, gn_g, gn_b):
    bsz, seq, _ = qkv.shape
    ln = RET_BLOCK
    vd = 2 * HEAD_DIM
    log_gamma = jnp.log(1.0 - 2.0 ** (-5.0 - jnp.arange(C_HEADS, dtype=F32)))
    idx = jnp.arange(ln, dtype=F32)
    diff = idx[:, None] - idx[None, :]
    dmask = jnp.where(diff >= 0, jnp.exp(log_gamma[:, None, None] * jnp.maximum(diff, 0.0)), 0.0)
    qdec = jnp.broadcast_to(jnp.exp(log_gamma[:, None] * (idx + 1.0))[:, :, None], (C_HEADS, ln, vd))
    kdec = jnp.broadcast_to(jnp.exp(log_gamma[:, None] * (ln - 1.0 - idx))[:, :, None], (C_HEADS, ln, vd))
    cdec = jnp.broadcast_to(jnp.exp(log_gamma * ln)[:, None, None], (C_HEADS, 1, vd))
    pair = lambda shape: pl.BlockSpec(shape, lambda b, hp: (hp,) + (0,) * (len(shape) - 1))
    return pl.pallas_call(
        _retention_kernel,
        grid=(bsz, C_HEADS // 2),
        in_specs=[pl.BlockSpec((1, seq, vd), lambda b, hp: (b, 0, hp)),
                  pl.BlockSpec((1, seq, vd), lambda b, hp: (b, 0, C_QK_W // vd + hp)),
                  pl.BlockSpec((1, seq, 2 * vd), lambda b, hp: (b, 0, (2 * C_QK_W) // (2 * vd) + hp)),
                  pl.BlockSpec((1, seq, 2 * vd), lambda b, hp: (b, 0, hp)),
                  pair((2, ln, ln)), pair((2, ln, vd)), pair((2, ln, vd)), pair((2, 1, vd)),
                  pl.BlockSpec((1, 2 * vd), lambda b, hp: (0, hp)),
                  pl.BlockSpec((1, 2 * vd), lambda b, hp: (0, hp))],
        out_specs=pl.BlockSpec((1, seq, 2 * vd), lambda b, hp: (b, 0, hp)),
        out_shape=jax.ShapeDtypeStruct((bsz, seq, C_W), BF16),
        scratch_shapes=[pltpu.VMEM((2, vd, vd), F32)],
        compiler_params=_params(("parallel", "parallel")),
        name="retention",
    )(qkv, qkv, qkv, gate, dmask, qdec, kdec, cdec, gn_g.reshape(1, C_W), gn_b.reshape(1, C_W))


def kernel(x, w_in_even, lam_q1, lam_k1, lam_q2, lam_k2, diff_subln_g, rel_bias, w_out_even,
           w_in_odd, ret_gn_g, ret_gn_b, sgu_ln_g, sgu_ln_b, sgu_w, sgu_b, w_out_odd,
           ln_mix_g, ln_mix_b, ffn_w_gate, ffn_w_up, ffn_w_down, ln_ffn_g, ln_ffn_b):
    bsz, seq, d = x.shape
    t = bsz * seq
    x2d = x.reshape(t, d)
    for l in range(DEPTH):
        j = l // 2
        if l % 2 == 0:
            keys, qta, vta, qtb, vtb = _proj_even(x2d, w_in_even[j].astype(BF16), bsz, seq)
            keys = keys.reshape(bsz, seq, -1)
            lam_init = 0.8 - 0.6 * math.exp(-0.3 * l)
            mix_a = _diff_attention(keys, qta, vta, lam_q1[j], lam_k1[j], lam_q2[j], lam_k2[j],
                                    diff_subln_g[j], lam_init)
            mix_b = _band_attention(keys, qtb, vtb, rel_bias[j])
            w_out = w_out_even[j]
        else:
            qkv, gate, mix_b = _proj_odd(x2d, w_in_odd[j].astype(BF16), sgu_w[j], sgu_b[j],
                                         sgu_ln_g[j], sgu_ln_b[j])
            mix_a = _retention(qkv.reshape(bsz, seq, -1), gate.reshape(bsz, seq, -1),
                               ret_gn_g[j], ret_gn_b[j])
            w_out = w_out_odd[j]
        x2d = _mix_ffn(mix_a.reshape(t, -1), mix_b.reshape(t, -1), x2d, w_out.astype(BF16),
                       ln_mix_g[l], ln_mix_b[l], ffn_w_gate[l].astype(BF16), ffn_w_up[l].astype(BF16),
                       ffn_w_down[l].astype(BF16), ln_ffn_g[l], ln_ffn_b[l])
    return x2d.reshape(bsz, seq, d)
```

```python
import functools
import math

import jax
import jax.numpy as jnp
from jax import lax
from jax.experimental import pallas as pl
from jax.experimental.pallas import tpu as pltpu

F32 = jnp.float32
BF16 = jnp.bfloat16

D_MODEL = 1024
HEAD_DIM = 64
CHUNK = 64
A_HEADS = 4
A_W = 512
B_HEADS = 8
B_W = 512
B_LEFT_CHUNKS = 8
REL_CLIP = 128
C_HEADS = 4
C_QK_W = 256
C_W = 512
D_CHUNK = 128
D_GROUPS = 4
D_WIDTH = 512
DEPTH = 2
ALPHA = (2.0 * DEPTH) ** 0.25
LN_EPS = 1e-5
NEG_INF = -1e30
LOG2E = 1.4426950408889634
SKIP_LOG2 = 160.0
NORM_SLACK = 1.0 + 2.0 ** -8

VMEM_LIMIT_V7X = 56 * 1024 * 1024
BF16_ROWS = 16

ROW_BLOCK = 512
ATT_BLOCK = 512
BAND_BLOCK = 256
BAND_WINDOW = BAND_BLOCK + B_LEFT_CHUNKS * CHUNK
BAND_UNROLL = 7
RET_BLOCK = 256


def _params(semantics):
    return pltpu.CompilerParams(dimension_semantics=semantics, vmem_limit_bytes=VMEM_LIMIT_V7X)


def _const_spec(shape):
    nd = len(shape)
    return pl.BlockSpec(shape, lambda *_: (0,) * nd, pipeline_mode=pl.Buffered(1))


def _layer_norm(y, g, b):
    mu = jnp.mean(y, axis=-1, keepdims=True)
    d = y - mu
    var = jnp.mean(d * d, axis=-1, keepdims=True)
    return d * lax.rsqrt(var + LN_EPS) * g + b


def _proj_odd_kernel(x_ref, w_ref, ws_ref, bt_ref, g_ref, beta_ref, qkv_ref, gate_ref, od_ref):
    x = x_ref[...].astype(BF16)
    dot = lambda c: jnp.dot(x, w_ref[:, c:c + 512], preferred_element_type=F32)
    off_d = 2 * C_QK_W + 2 * C_W
    u = dot(off_d)
    v = dot(off_d + D_WIDTH)
    qkv_ref[:, 0:512] = dot(0).astype(BF16)
    qkv_ref[:, 512:1024] = dot(512).astype(BF16)
    gate_ref[...] = dot(2 * C_QK_W + C_W)
    u = jax.nn.gelu(u)
    v = _layer_norm(jax.nn.gelu(v), g_ref[...], beta_ref[...])
    gw = D_WIDTH // D_GROUPS
    r = lax.broadcasted_iota(jnp.int32, (D_CHUNK, D_CHUNK), 0)
    c = lax.broadcasted_iota(jnp.int32, (D_CHUNK, D_CHUNK), 1)
    for g in range(D_GROUPS):
        w = jnp.where(r >= c, ws_ref[g], 0.0).astype(BF16)
        bias = bt_ref[:, g:g + 1]
        cols = slice(g * gw, (g + 1) * gw)
        vg = v[:, cols].astype(BF16)
        for n in range(ROW_BLOCK // D_CHUNK):
            rows = slice(n * D_CHUNK, (n + 1) * D_CHUNK)
            mixed = jnp.dot(w, vg[rows], preferred_element_type=F32) + bias
            od_ref[rows, cols] = (u[rows, cols] * mixed).astype(od_ref.dtype)


def _proj_odd(x2d, w_bf16, w_s, b_s, ln_g, ln_b):
    t, k = x2d.shape
    n = w_bf16.shape[1]
    assert ROW_BLOCK % D_CHUNK == 0 and n == 2 * C_QK_W + 2 * C_W + 2 * D_WIDTH
    row_spec = lambda wd: pl.BlockSpec((ROW_BLOCK, wd), lambda i: (i, 0))
    return pl.pallas_call(
        _proj_odd_kernel,
        grid=(t // ROW_BLOCK,),
        in_specs=[row_spec(k), _const_spec((k, n)), _const_spec((D_GROUPS, D_CHUNK, D_CHUNK)),
                  _const_spec((D_CHUNK, D_GROUPS)), _const_spec((1, D_WIDTH)), _const_spec((1, D_WIDTH))],
        out_specs=[row_spec(2 * C_QK_W + C_W), row_spec(C_W), row_spec(D_WIDTH)],
        out_shape=[jax.ShapeDtypeStruct((t, 2 * C_QK_W + C_W), BF16),
                   jax.ShapeDtypeStruct((t, C_W), F32),
                   jax.ShapeDtypeStruct((t, D_WIDTH), BF16)],
        compiler_params=_params(("parallel",)),
        name="in_proj_odd",
    )(x2d, w_bf16, w_s.astype(F32), b_s.T.astype(F32), ln_g.reshape(1, D_WIDTH), ln_b.reshape(1, D_WIDTH))


def _transposed_value_tiles(vals, block):
    ones_tile = (lax.broadcasted_iota(jnp.int32, (BF16_ROWS, block), 0) == 0).astype(BF16)
    return [jnp.concatenate([vals[r:r + block].T.astype(BF16), ones_tile], axis=0)
            for r in range(0, vals.shape[0], block)]


def _transposed_query_tiles(q, block):
    tiles = []
    for r in range(0, q.shape[0], block):
        qt = q[r:r + block].T * (HEAD_DIM ** -0.5 * LOG2E)
        row = lax.broadcasted_iota(jnp.int32, qt.shape, 0)
        tiles.append((jnp.where(row < HEAD_DIM, qt, 0.0).astype(BF16),
                      jnp.where(row >= HEAD_DIM, qt, 0.0).astype(BF16)))
    return tiles


def _proj_even_kernel(x_ref, w_ref, h_ref, qta_ref, vta_ref, vtb_ref):
    x = x_ref[...].astype(BF16)
    vd = 2 * HEAD_DIM
    dot = lambda c: jnp.dot(x, w_ref[:, c:c + 512], preferred_element_type=F32)
    h_ref[:, 0:512] = dot(A_W).astype(BF16)
    h_ref[:, 512:1024] = dot(3 * A_W).astype(BF16)
    h_ref[:, 1024:1536] = dot(3 * A_W + B_W).astype(BF16)
    qa = dot(0)
    for h in range(A_HEADS):
        tile0, tile1 = _transposed_query_tiles(qa[:, h * vd:(h + 1) * vd], ATT_BLOCK)[0]
        qta_ref[0, h, 0, 0] = tile0
        qta_ref[0, h, 0, 1] = tile1
    va = dot(2 * A_W)
    for h in range(A_HEADS):
        vta_ref[0, h, 0] = _transposed_value_tiles(va[:, h * vd:(h + 1) * vd], ATT_BLOCK)[0]
    vb = dot(3 * A_W + 2 * B_W)
    for hp in range(B_HEADS // 2):
        tiles = _transposed_value_tiles(vb[:, hp * vd:(hp + 1) * vd], BAND_BLOCK)
        for n, tile in enumerate(tiles):
            vtb_ref[0, hp, n] = tile


def _proj_even(x2d, w_bf16, bsz, seq):
    t, k = x2d.shape
    n = w_bf16.shape[1]
    assert ROW_BLOCK == ATT_BLOCK and seq % ROW_BLOCK == 0
    per_seq = seq // ROW_BLOCK
    vd = 2 * HEAD_DIM
    vrows = vd + BF16_ROWS
    nba, nbb, per_step_b = seq // ATT_BLOCK, seq // BAND_BLOCK, ROW_BLOCK // BAND_BLOCK
    seq_block = lambda i: (i // per_seq, 0, i % per_seq, 0, 0)
    seq_block6 = lambda i: (i // per_seq, 0, i % per_seq, 0, 0, 0)
    return pl.pallas_call(
        _proj_even_kernel,
        grid=(t // ROW_BLOCK,),
        in_specs=[pl.BlockSpec((ROW_BLOCK, k), lambda i: (i, 0)), _const_spec((k, n))],
        out_specs=[pl.BlockSpec((ROW_BLOCK, A_W + 2 * B_W), lambda i: (i, 0)),
                   pl.BlockSpec((1, A_HEADS, 1, 2, vd, ATT_BLOCK), seq_block6),
                   pl.BlockSpec((1, A_HEADS, 1, vrows, ATT_BLOCK), seq_block),
                   pl.BlockSpec((1, B_HEADS // 2, per_step_b, vrows, BAND_BLOCK), seq_block)],
        out_shape=[jax.ShapeDtypeStruct((t, A_W + 2 * B_W), BF16),
                   jax.ShapeDtypeStruct((bsz, A_HEADS, nba, 2, vd, ATT_BLOCK), BF16),
                   jax.ShapeDtypeStruct((bsz, A_HEADS, nba, vrows, ATT_BLOCK), BF16),
                   jax.ShapeDtypeStruct((bsz, B_HEADS // 2, nbb, vrows, BAND_BLOCK), BF16)],
        compiler_params=_params(("parallel",)),
        name="in_proj_even",
    )(x2d, w_bf16)


def _mix_ffn_kernel(a_ref, b_ref, x_ref, wo_ref, g1_ref, b1_ref, wg_ref, wu_ref, wd_ref,
                    g2_ref, b2_ref, o_ref):
    ka = a_ref.shape[-1]
    half = x_ref.shape[0] // 2
    halves = (slice(0, half), slice(half, 2 * half))
    mix = [jnp.dot(a_ref[rows, :], wo_ref[0:ka, :], preferred_element_type=F32)
           + jnp.dot(b_ref[rows, :], wo_ref[ka:, :], preferred_element_type=F32) for rows in halves]
    x1 = [_layer_norm(ALPHA * x_ref[rows, :] + m, g1_ref[...], b1_ref[...]) for rows, m in zip(halves, mix)]
    xb = [v.astype(BF16) for v in x1]
    gate = [jnp.dot(v, wg_ref[...], preferred_element_type=F32) for v in xb]
    up = [jnp.dot(v, wu_ref[...], preferred_element_type=F32) for v in xb]
    h = [(g * jax.nn.sigmoid(g) * u).astype(BF16) for g, u in zip(gate, up)]
    y = [ALPHA * v + jnp.dot(hh, wd_ref[...], preferred_element_type=F32) for v, hh in zip(x1, h)]
    for rows, v in zip(halves, y):
        o_ref[rows, :] = _layer_norm(v, g2_ref[...], b2_ref[...])


def _mix_ffn(a, b, x2d, wo, g1, b1, wg, wu, wd, g2, b2):
    t, d = x2d.shape
    ka, kb = a.shape[1], b.shape[1]
    f = wg.shape[1]
    vec = lambda v: v.reshape(1, d)
    return pl.pallas_call(
        _mix_ffn_kernel,
        grid=(t // ROW_BLOCK,),
        in_specs=[pl.BlockSpec((ROW_BLOCK, ka), lambda i: (i, 0)),
                  pl.BlockSpec((ROW_BLOCK, kb), lambda i: (i, 0)),
                  pl.BlockSpec((ROW_BLOCK, d), lambda i: (i, 0)),
                  _const_spec((ka + kb, d)), _const_spec((1, d)), _const_spec((1, d)),
                  _const_spec((d, f)), _const_spec((d, f)), _const_spec((f, d)),
                  _const_spec((1, d)), _const_spec((1, d))],
        out_specs=pl.BlockSpec((ROW_BLOCK, d), lambda i: (i, 0)),
        out_shape=jax.ShapeDtypeStruct((t, d), F32),
        compiler_params=_params(("parallel",)),
        name="mix_ffn",
    )(a, b, x2d, wo, vec(g1), vec(b1), wg, wu, wd, vec(g2), vec(b2))


def _diff_attn_kernel(qt_ref, k_ref, vt_ref, past_ref, diag_ref, past_max_ref, diag_max_ref, step_ref,
                      lq1_ref, lk1_ref, lq2_ref, lk2_ref, g_ref, o_ref,
                      sa_ref, sb_ref, ra_ref, rb_ref, m_ref, acc_ref, knorm_ref, *, lam_init):
    t = ATT_BLOCK
    n_chunks = t // CHUNK
    i = pl.program_id(2)
    qt_ref = qt_ref.at[0, 0, 0]

    m_ref[...] = jnp.full(m_ref.shape, NEG_INF, F32)
    acc_ref[...] = jnp.zeros(acc_ref.shape, F32)

    @pl.when(i == 0)
    def _():
        lane = lax.broadcasted_iota(jnp.int32, (t, 2 * HEAD_DIM), 1)

        def block_norms(j, carry):
            kb = k_ref[0, pl.ds(pl.multiple_of(j * t, t), t), :].astype(F32)
            sq = kb * kb
            n1 = jnp.sum(jnp.where(lane < HEAD_DIM, sq, 0.0), axis=1, keepdims=True)
            n2 = jnp.sum(jnp.where(lane >= HEAD_DIM, sq, 0.0), axis=1, keepdims=True)
            return (jnp.maximum(carry[0], jnp.max(n1, axis=0, keepdims=True)),
                    jnp.maximum(carry[1], jnp.max(n2, axis=0, keepdims=True)))

        zero = jnp.zeros((1, 1), F32)
        n1, n2 = lax.fori_loop(0, k_ref.shape[1] // t, block_norms, (zero, zero))
        knorm_ref[0] = jnp.broadcast_to(jnp.sqrt(n1), (1, t))
        knorm_ref[1] = jnp.broadcast_to(jnp.sqrt(n2), (1, t))

    def scores(j, s_ref, r_ref):
        start = pl.multiple_of(j * t, t)
        k_blk = k_ref[0, pl.ds(start, t), :]
        for mp in range(2):
            s = jnp.dot(k_blk, qt_ref[mp], preferred_element_type=F32)
            s_ref[mp] = s
            r_ref[mp] = jnp.max(s.reshape(n_chunks, CHUNK // 8, 8, t), axis=1)

    def softmax_pv(j, s_ref, r_ref, tile_ref, tile_max_ref):
        vt_blk = vt_ref[0, 0, j]
        step = step_ref[0]
        for mp in range(2):
            m_prev = m_ref[mp]
            bound = jnp.max(r_ref[mp] + tile_max_ref[0][:, None, :], axis=0)
            bound = jnp.max(bound, axis=0, keepdims=True)
            m_new = jnp.maximum(m_prev, bound)
            alpha = jnp.exp2(m_prev - m_new)
            p = jnp.exp2(s_ref[mp] + tile_ref[0] - m_new)
            acc_ref[mp] = alpha * acc_ref[mp] + jnp.dot(vt_blk, p.astype(BF16),
                                                        preferred_element_type=F32)
            m_ref[mp] = m_new + step

    def prev_block(r):
        return jnp.maximum(i - r, 0)

    scores(i, sa_ref, ra_ref)
    scores(prev_block(1), sb_ref, rb_ref)
    softmax_pv(i, sa_ref, ra_ref, diag_ref, diag_max_ref)

    tile_top = jnp.max(past_max_ref[0], axis=0, keepdims=True)
    inv_step = 1.0 / step_ref[0]
    reach = jnp.zeros((1, t), F32)
    for mp in range(2):
        qb = qt_ref[mp].astype(F32)
        qnorm = jnp.sqrt(jnp.sum(qb * qb, axis=0, keepdims=True))
        gap = qnorm * knorm_ref[mp] * NORM_SLACK + tile_top - m_ref[mp]
        reach = jnp.maximum(reach, (gap + SKIP_LOG2) * inv_step + 1.0)
    reach = jnp.floor(jnp.minimum(reach, float(k_ref.shape[1] // t))).astype(jnp.int32)
    n_keep = jnp.minimum(jnp.max(reach), i)

    def past_pair(r0):
        scores(prev_block(r0 + 1), sa_ref, ra_ref)
        softmax_pv(i - r0, sb_ref, rb_ref, past_ref, past_max_ref)
        scores(prev_block(r0 + 2), sb_ref, rb_ref)
        softmax_pv(i - r0 - 1, sa_ref, ra_ref, past_ref, past_max_ref)

    def past_oct(jj, carry):
        for pr in range(4):
            past_pair(8 * jj + 2 * pr + 1)
        return carry

    n_octs = lax.shift_right_logical(n_keep, 3)
    lax.fori_loop(0, n_octs, past_oct, 0)
    done = 8 * n_octs

    @pl.when((n_keep & 4) == 4)
    def _():
        past_pair(done + 1)
        past_pair(done + 3)

    @pl.when((n_keep & 2) == 2)
    def _():
        past_pair(done + (n_keep & 4) + 1)

    @pl.when((n_keep & 1) == 1)
    def _():
        softmax_pv(i - n_keep, sb_ref, rb_ref, past_ref, past_max_ref)

    lam = (jnp.exp(jnp.sum(lq1_ref[...] * lk1_ref[...], axis=-1, keepdims=True))
           - jnp.exp(jnp.sum(lq2_ref[...] * lk2_ref[...], axis=-1, keepdims=True)) + lam_init)
    vd = 2 * HEAD_DIM
    ot = (acc_ref[0, 0:vd] / acc_ref[0, vd:vd + 1]
          - lam * (acc_ref[1, 0:vd] / acc_ref[1, vd:vd + 1]))
    ms = jnp.mean(ot * ot, axis=0, keepdims=True)
    y = (ot * lax.rsqrt(ms + LN_EPS)).T * g_ref[...] * (1.0 - lam_init)
    o_ref[0] = y.astype(o_ref.dtype)


def _diff_attention(keys, qt, vt, lam_q1, lam_k1, lam_q2, lam_k2, subln_g, lam_init):
    bsz, seq, _ = keys.shape
    t = ATT_BLOCK
    nb = seq // t
    n_chunks = t // CHUNK
    vd = 2 * HEAD_DIM
    slopes = 2.0 ** (-8.0 * jnp.arange(1, A_HEADS + 1, dtype=F32) / A_HEADS)
    kk = jnp.arange(t)[:, None]
    qq = jnp.arange(t)[None, :]
    rel = (qq - kk).astype(F32)
    past = -(slopes * LOG2E)[:, None, None] * rel[None]
    allowed = (kk // CHUNK) <= (qq // CHUNK)
    diag = jnp.where(allowed[None], -(slopes * LOG2E)[:, None, None] * jnp.abs(rel)[None], NEG_INF)
    chunk_max = lambda tile: jnp.max(tile.reshape(A_HEADS, n_chunks, CHUNK, t), axis=2)
    step = jnp.broadcast_to((slopes * (LOG2E * t))[:, None, None], (A_HEADS, 1, t))
    vec = lambda v: v.reshape(1, HEAD_DIM).astype(F32)
    small = lambda shape: pl.BlockSpec(shape, lambda b, h, i: (0,) * len(shape))
    per_head = lambda shape: pl.BlockSpec(shape, lambda b, h, i: (h,) + (0,) * (len(shape) - 1))
    return pl.pallas_call(
        functools.partial(_diff_attn_kernel, lam_init=lam_init),
        grid=(bsz, A_HEADS, nb),
        in_specs=[pl.BlockSpec((1, 1, 1, 2, vd, t), lambda b, h, i: (b, h, i, 0, 0, 0)),
                  pl.BlockSpec((1, seq, vd), lambda b, h, i: (b, 0, h)),
                  pl.BlockSpec((1, 1, nb, vd + BF16_ROWS, t), lambda b, h, i: (b, h, 0, 0, 0)),
                  per_head((1, t, t)), per_head((1, t, t)),
                  per_head((1, n_chunks, t)), per_head((1, n_chunks, t)), per_head((1, 1, t)),
                  small((1, HEAD_DIM)), small((1, HEAD_DIM)), small((1, HEAD_DIM)), small((1, HEAD_DIM)),
                  small((1, vd))],
        out_specs=pl.BlockSpec((1, t, vd), lambda b, h, i: (b, i, h)),
        out_shape=jax.ShapeDtypeStruct((bsz, seq, A_W), BF16),
        scratch_shapes=[pltpu.VMEM((2, t, t), F32), pltpu.VMEM((2, t, t), F32),
                        pltpu.VMEM((2, n_chunks, 8, t), F32), pltpu.VMEM((2, n_chunks, 8, t), F32),
                        pltpu.VMEM((2, 1, t), F32),
                        pltpu.VMEM((2, vd + BF16_ROWS, t), F32),
                        pltpu.VMEM((2, 1, t), F32)],
        compiler_params=_params(("parallel", "parallel", "arbitrary")),
        name="diff_attn",
    )(qt, keys, vt, past, diag, chunk_max(past), chunk_max(diag), step,
      vec(lam_q1), vec(lam_k1), vec(lam_q2), vec(lam_k2), subln_g.reshape(1, vd).astype(F32))


def _band_attn_kernel(q_ref, k_ref, vt_ref, diag_vals_ref, band_ref, o_ref, bias_ref, sa_ref, sb_ref):
    t = BAND_BLOCK
    nb = k_ref.shape[1] // t
    n_win = BAND_WINDOW // t
    vd = 2 * HEAD_DIM
    row = lax.broadcasted_iota(jnp.int32, (vd, t), 0)
    n_vals = diag_vals_ref.shape[-1]

    for hh in range(2):
        vals = jnp.broadcast_to(diag_vals_ref[hh], (8, n_vals))
        for c0 in range(0, BAND_WINDOW, 8):
            rolled = pltpu.roll(vals, (c0 - (BAND_WINDOW - 1)) % n_vals, axis=1, stride=1, stride_axis=0)
            bias_ref[hh, c0:c0 + 8, :] = jnp.where(band_ref[c0:c0 + 8, :] > 0.0, rolled[:, 0:t], NEG_INF)

    def rows(n):
        return pl.ds(pl.multiple_of(n * t, t), t)

    def key_block(n, w):
        return jnp.maximum(n - (n_win - 1) + w, 0)

    def scores(n, s_ref):
        qt = q_ref[0, rows(n), :].astype(F32).T * (HEAD_DIM ** -0.5 * LOG2E)
        for hh in range(2):
            in_head = (row < HEAD_DIM) if hh == 0 else (row >= HEAD_DIM)
            qh = jnp.where(in_head, qt, 0.0).astype(BF16)
            for w in range(n_win):
                s_ref[hh, w * t:(w + 1) * t] = jnp.dot(k_ref[0, rows(key_block(n, w)), :], qh,
                                                       preferred_element_type=F32)

    def softmax_pv(n, s_ref, mask_keys):
        outs = []
        for hh in range(2):
            st = s_ref[hh] + bias_ref[hh]
            if mask_keys:
                key_pos = (n - (n_win - 1)) * t + lax.broadcasted_iota(jnp.int32, (BAND_WINDOW, 1), 0)
                st = jnp.where(key_pos >= 0, st, NEG_INF)
            p = jnp.exp2(st - jnp.max(st, axis=0, keepdims=True)).astype(BF16)
            ot = jnp.dot(vt_ref[0, 0, key_block(n, 0)], p[0:t], preferred_element_type=F32)
            for w in range(1, n_win):
                ot = ot + jnp.dot(vt_ref[0, 0, key_block(n, w)], p[w * t:(w + 1) * t],
                                  preferred_element_type=F32)
            outs.append(ot[0:vd] / ot[vd:vd + 1])
        o_ref[0, rows(n), :] = jnp.where(row < HEAD_DIM, outs[0], outs[1]).T.astype(o_ref.dtype)

    for n in range(n_win - 1):
        scores(n, sa_ref)
        softmax_pv(n, sa_ref, True)

    first = n_win - 1
    scores(first, sa_ref)

    def pair(jj, carry):
        n0 = first + 2 * jj
        scores(n0 + 1, sb_ref)
        softmax_pv(n0, sa_ref, False)
        scores(n0 + 2, sa_ref)
        softmax_pv(n0 + 1, sb_ref, False)
        return carry

    n_pairs = (nb - first - 2) // 2
    lax.fori_loop(0, n_pairs, pair, 0, unroll=BAND_UNROLL)
    for n in range(first + 2 * n_pairs, nb):
        if n + 1 < nb:
            scores(n + 1, sb_ref)
        softmax_pv(n, sa_ref, False)
        sa_ref, sb_ref = sb_ref, sa_ref


def _band_attention(hproj, vt, rel_bias):
    bsz, seq, _ = hproj.shape
    t = BAND_BLOCK
    vd = 2 * HEAD_DIM
    c = jnp.arange(BAND_WINDOW)[:, None]
    r = jnp.arange(t)[None, :]
    in_band = ((c // CHUNK) >= (r // CHUNK)) & ((c // CHUNK) <= (r // CHUNK) + B_LEFT_CHUNKS)
    rb = rel_bias.astype(F32) * LOG2E
    n_lo = BAND_WINDOW - 1 - B_LEFT_CHUNKS * CHUNK - REL_CLIP
    n_hi = t + BAND_WINDOW - n_lo - rb.shape[1]
    diag_vals = jnp.concatenate([jnp.broadcast_to(rb[:, :1], (B_HEADS, n_lo)), rb,
                                 jnp.broadcast_to(rb[:, -1:], (B_HEADS, n_hi))], axis=1)[:, None, :]
    qcol, kcol = A_W // vd, (A_W + B_W) // vd
    seq_spec = lambda col: pl.BlockSpec((1, seq, vd), lambda b, hp: (b, 0, col + hp))
    return pl.pallas_call(
        _band_attn_kernel,
        grid=(bsz, B_HEADS // 2),
        in_specs=[seq_spec(qcol), seq_spec(kcol),
                  pl.BlockSpec((1, 1, seq // t, vd + BF16_ROWS, t), lambda b, hp: (b, hp, 0, 0, 0)),
                  pl.BlockSpec((2, 1, t + BAND_WINDOW), lambda b, hp: (hp, 0, 0)),
                  pl.BlockSpec((BAND_WINDOW, t), lambda b, hp: (0, 0))],
        out_specs=pl.BlockSpec((1, seq, vd), lambda b, hp: (b, 0, hp)),
        out_shape=jax.ShapeDtypeStruct((bsz, seq, B_W), BF16),
        scratch_shapes=[pltpu.VMEM((2, BAND_WINDOW, t), F32),
                        pltpu.VMEM((2, BAND_WINDOW, t), F32), pltpu.VMEM((2, BAND_WINDOW, t), F32)],
        compiler_params=_params(("parallel", "parallel")),
        name="band_attn",
    )(hproj, hproj, vt, diag_vals, in_band.astype(F32))


def _retention_kernel(q_ref, k_ref, v_ref, gate_ref, dmask_ref, qdec_ref, kdec_ref, cdec_ref,
                      g_ref, beta_ref, o_ref, state_ref):
    ln = RET_BLOCK
    nb = q_ref.shape[1] // ln
    vd = 2 * HEAD_DIM
    scale = HEAD_DIM ** -0.5
    lane = lax.broadcasted_iota(jnp.int32, (ln, vd), 1)
    state_ref[...] = jnp.zeros(state_ref.shape, F32)

    def block(n, carry):
        rows = pl.ds(pl.multiple_of(n * ln, ln), ln)
        q = q_ref[0, rows, :]
        k = k_ref[0, rows, :]
        for hh in range(2):
            in_head = (lane < HEAD_DIM) if hh == 0 else (lane >= HEAD_DIM)
            qh = jnp.where(in_head, q, jnp.zeros_like(q))
            kh = jnp.where(in_head, k.astype(F32) * scale, 0.0)
            sl = slice(hh * vd, (hh + 1) * vd)
            v = v_ref[0, rows, sl]
            scores = lax.dot_general(qh, kh.astype(BF16), (((1,), (1,)), ((), ())),
                                     preferred_element_type=F32) * dmask_ref[hh]
            inner = jnp.dot(scores.astype(BF16), v, preferred_element_type=F32)
            state = state_ref[hh]
            cross = jnp.dot(qh, state.astype(BF16), preferred_element_type=F32) * qdec_ref[hh]
            kdt = (kh * kdec_ref[hh]).T.astype(BF16)
            kv = jnp.dot(kdt, v, preferred_element_type=F32)
            state_ref[hh] = cdec_ref[hh] * state + kv
            o = inner + cross
            mu = jnp.mean(o, axis=-1, keepdims=True)
            d = o - mu
            var = jnp.mean(d * d, axis=-1, keepdims=True)
            y = d * lax.rsqrt(var + LN_EPS) * g_ref[:, sl] + beta_ref[:, sl]
            gate = gate_ref[0, rows, sl]
            o_ref[0, rows, sl] = (y * (gate * jax.nn.sigmoid(gate))).astype(o_ref.dtype)
        return carry

    lax.fori_loop(0, nb, block, 0, unroll=2)


def _retention(qkv, gate, gn_g, gn_b):
    bsz, seq, _ = qkv.shape
    ln = RET_BLOCK
    vd = 2 * HEAD_DIM
    log_gamma = jnp.log(1.0 - 2.0 ** (-5.0 - jnp.arange(C_HEADS, dtype=F32)))
    idx = jnp.arange(ln, dtype=F32)
    diff = idx[:, None] - idx[None, :]
    dmask = jnp.where(diff >= 0, jnp.exp(log_gamma[:, None, None] * jnp.maximum(diff, 0.0)), 0.0)
    qdec = jnp.broadcast_to(jnp.exp(log_gamma[:, None] * (idx + 1.0))[:, :, None], (C_HEADS, ln, vd))
    kdec = jnp.broadcast_to(jnp.exp(log_gamma[:, None] * (ln - 1.0 - idx))[:, :, None], (C_HEADS, ln, vd))
    cdec = jnp.broadcast_to(jnp.exp(log_gamma * ln)[:, None, None], (C_HEADS, 1, vd))
    pair = lambda shape: pl.BlockSpec(shape, lambda b, hp: (hp,) + (0,) * (len(shape) - 1))
    return pl.pallas_call(
        _retention_kernel,
        grid=(bsz, C_HEADS // 2),
        in_specs=[pl.BlockSpec((1, seq, vd), lambda b, hp: (b, 0, hp)),
                  pl.BlockSpec((1, seq, vd), lambda b, hp: (b, 0, C_QK_W // vd + hp)),
                  pl.BlockSpec((1, seq, 2 * vd), lambda b, hp: (b, 0, (2 * C_QK_W) // (2 * vd) + hp)),
                  pl.BlockSpec((1, seq, 2 * vd), lambda b, hp: (b, 0, hp)),
                  pair((2, ln, ln)), pair((2, ln, vd)), pair((2, ln, vd)), pair((2, 1, vd)),
                  pl.BlockSpec((1, 2 * vd), lambda b, hp: (0, hp)),
                  pl.BlockSpec((1, 2 * vd), lambda b, hp: (0, hp))],
        out_specs=pl.BlockSpec((1, seq, 2 * vd), lambda b, hp: (b, 0, hp)),
        out_shape=jax.ShapeDtypeStruct((bsz, seq, C_W), BF16),
        scratch_shapes=[pltpu.VMEM((2, vd, vd), F32)],
        compiler_params=_params(("parallel", "parallel")),
        name="retention",
    )(qkv, qkv, qkv, gate, dmask, qdec, kdec, cdec, gn_g.reshape(1, C_W), gn_b.reshape(1, C_W))


def kernel(x, w_in_even, lam_q1, lam_k1, lam_q2, lam_k2, diff_subln_g, rel_bias, w_out_even,
           w_in_odd, ret_gn_g, ret_gn_b, sgu_ln_g, sgu_ln_b, sgu_w, sgu_b, w_out_odd,
           ln_mix_g, ln_mix_b, ffn_w_gate, ffn_w_up, ffn_w_down, ln_ffn_g, ln_ffn_b):
    bsz, seq, d = x.shape
    t = bsz * seq
    x2d = x.reshape(t, d)
    for l in range(DEPTH):
        j = l // 2
        if l % 2 == 0:
            hproj, qta, vta, vtb = _proj_even(x2d, w_in_even[j].astype(BF16), bsz, seq)
            hproj = hproj.reshape(bsz, seq, -1)
            lam_init = 0.8 - 0.6 * math.exp(-0.3 * l)
            mix_a = _diff_attention(hproj, qta, vta, lam_q1[j], lam_k1[j], lam_q2[j], lam_k2[j],
                                    diff_subln_g[j], lam_init)
            mix_b = _band_attention(hproj, vtb, rel_bias[j])
            w_out = w_out_even[j]
        else:
            qkv, gate, mix_b = _proj_odd(x2d, w_in_odd[j].astype(BF16), sgu_w[j], sgu_b[j],
                                         sgu_ln_g[j], sgu_ln_b[j])
            mix_a = _retention(qkv.reshape(bsz, seq, -1), gate.reshape(bsz, seq, -1),
                               ret_gn_g[j], ret_gn_b[j])
            w_out = w_out_odd[j]
        x2d = _mix_ffn(mix_a.reshape(t, -1), mix_b.reshape(t, -1), x2d, w_out.astype(BF16),
                       ln_mix_g[l], ln_mix_b[l], ffn_w_gate[l].astype(BF16), ffn_w_up[l].astype(BF16),
                       ffn_w_down[l].astype(BF16), ln_ffn_g[l], ln_ffn_b[l])
    return x2d.reshape(bsz, seq, d)
```

```python
import functools
import math

import jax
import jax.numpy as jnp
from jax import lax
from jax.experimental import pallas as pl
from jax.experimental.pallas import tpu as pltpu

F32 = jnp.float32
BF16 = jnp.bfloat16

D_MODEL = 1024
HEAD_DIM = 64
CHUNK = 64
A_HEADS = 4
A_W = 512
B_HEADS = 8
B_W = 512
B_LEFT_CHUNKS = 8
REL_CLIP = 128
C_HEADS = 4
C_QK_W = 256
C_W = 512
D_CHUNK = 128
D_GROUPS = 4
D_WIDTH = 512
DEPTH = 2
ALPHA = (2.0 * DEPTH) ** 0.25
LN_EPS = 1e-5
NEG_INF = -1e30
LOG2E = 1.4426950408889634
SKIP_LOG2 = 160.0
NORM_SLACK = 1.0 + 2.0 ** -8

VMEM_LIMIT_V7X = 56 * 1024 * 1024
BF16_ROWS = 16
POS_SPLIT = 256
POS_TERMS = 3

ROW_BLOCK = 512
ATT_BLOCK = 512
BAND_BLOCK = 256
BAND_WINDOW = BAND_BLOCK + B_LEFT_CHUNKS * CHUNK
BAND_UNROLL = 7
RET_BLOCK = 256


def _params(semantics):
    return pltpu.CompilerParams(dimension_semantics=semantics, vmem_limit_bytes=VMEM_LIMIT_V7X)


def _const_spec(shape):
    nd = len(shape)
    return pl.BlockSpec(shape, lambda *_: (0,) * nd, pipeline_mode=pl.Buffered(1))


def _layer_norm(y, g, b):
    mu = jnp.mean(y, axis=-1, keepdims=True)
    d = y - mu
    var = jnp.mean(d * d, axis=-1, keepdims=True)
    return d * lax.rsqrt(var + LN_EPS) * g + b


def _proj_odd_kernel(x_ref, w_ref, ws_ref, bt_ref, g_ref, beta_ref, qkv_ref, gate_ref, od_ref):
    x = x_ref[...].astype(BF16)
    dot = lambda c: jnp.dot(x, w_ref[:, c:c + 512], preferred_element_type=F32)
    off_d = 2 * C_QK_W + 2 * C_W
    u = dot(off_d)
    v = dot(off_d + D_WIDTH)
    qkv_ref[:, 0:512] = dot(0).astype(BF16)
    qkv_ref[:, 512:1024] = dot(512).astype(BF16)
    gate_ref[...] = dot(2 * C_QK_W + C_W)
    u = jax.nn.gelu(u)
    v = _layer_norm(jax.nn.gelu(v), g_ref[...], beta_ref[...])
    gw = D_WIDTH // D_GROUPS
    r = lax.broadcasted_iota(jnp.int32, (D_CHUNK, D_CHUNK), 0)
    c = lax.broadcasted_iota(jnp.int32, (D_CHUNK, D_CHUNK), 1)
    for g in range(D_GROUPS):
        w = jnp.where(r >= c, ws_ref[g], 0.0).astype(BF16)
        bias = bt_ref[:, g:g + 1]
        cols = slice(g * gw, (g + 1) * gw)
        vg = v[:, cols].astype(BF16)
        for n in range(ROW_BLOCK // D_CHUNK):
            rows = slice(n * D_CHUNK, (n + 1) * D_CHUNK)
            mixed = jnp.dot(w, vg[rows], preferred_element_type=F32) + bias
            od_ref[rows, cols] = (u[rows, cols] * mixed).astype(od_ref.dtype)


def _proj_odd(x2d, w_bf16, w_s, b_s, ln_g, ln_b):
    t, k = x2d.shape
    n = w_bf16.shape[1]
    assert ROW_BLOCK % D_CHUNK == 0 and n == 2 * C_QK_W + 2 * C_W + 2 * D_WIDTH
    row_spec = lambda wd: pl.BlockSpec((ROW_BLOCK, wd), lambda i: (i, 0))
    return pl.pallas_call(
        _proj_odd_kernel,
        grid=(t // ROW_BLOCK,),
        in_specs=[row_spec(k), _const_spec((k, n)), _const_spec((D_GROUPS, D_CHUNK, D_CHUNK)),
                  _const_spec((D_CHUNK, D_GROUPS)), _const_spec((1, D_WIDTH)), _const_spec((1, D_WIDTH))],
        out_specs=[row_spec(2 * C_QK_W + C_W), row_spec(C_W), row_spec(D_WIDTH)],
        out_shape=[jax.ShapeDtypeStruct((t, 2 * C_QK_W + C_W), BF16),
                   jax.ShapeDtypeStruct((t, C_W), F32),
                   jax.ShapeDtypeStruct((t, D_WIDTH), BF16)],
        compiler_params=_params(("parallel",)),
        name="in_proj_odd",
    )(x2d, w_bf16, w_s.astype(F32), b_s.T.astype(F32), ln_g.reshape(1, D_WIDTH), ln_b.reshape(1, D_WIDTH))


def _transposed_value_tiles(vals, block):
    ones_tile = (lax.broadcasted_iota(jnp.int32, (BF16_ROWS, block), 0) == 0).astype(BF16)
    return [jnp.concatenate([vals[r:r + block].T.astype(BF16), ones_tile], axis=0)
            for r in range(0, vals.shape[0], block)]


def _transposed_query_tiles(q, block):
    tiles = []
    for r in range(0, q.shape[0], block):
        qt = q[r:r + block].T * (HEAD_DIM ** -0.5 * LOG2E)
        row = lax.broadcasted_iota(jnp.int32, qt.shape, 0)
        tiles.append((jnp.where(row < HEAD_DIM, qt, 0.0).astype(BF16),
                      jnp.where(row >= HEAD_DIM, qt, 0.0).astype(BF16)))
    return tiles


def _proj_even_kernel(x_ref, w_ref, h_ref, qta_ref, vta_ref, vtb_ref):
    x = x_ref[...].astype(BF16)
    vd = 2 * HEAD_DIM
    dot = lambda c: jnp.dot(x, w_ref[:, c:c + 512], preferred_element_type=F32)
    ka = dot(A_W)
    r = lax.broadcasted_iota(jnp.int32, (ROW_BLOCK, vd), 0)
    col = lax.broadcasted_iota(jnp.int32, (ROW_BLOCK, vd), 1)
    pos = jnp.where(col < POS_TERMS, r & (POS_SPLIT - 1),
                    jnp.where(col < 2 * POS_TERMS, r & ~(POS_SPLIT - 1), 0)).astype(F32).astype(BF16)
    for h in range(A_HEADS):
        h_ref[:, 2 * vd * h:2 * vd * h + vd] = ka[:, h * vd:(h + 1) * vd].astype(BF16)
        h_ref[:, 2 * vd * h + vd:2 * vd * (h + 1)] = pos
    h_ref[:, 2 * A_W:2 * A_W + B_W] = dot(3 * A_W).astype(BF16)
    h_ref[:, 2 * A_W + B_W:2 * A_W + 2 * B_W] = dot(3 * A_W + B_W).astype(BF16)
    qa = dot(0)
    for h in range(A_HEADS):
        tile0, tile1 = _transposed_query_tiles(qa[:, h * vd:(h + 1) * vd], ATT_BLOCK)[0]
        qta_ref[0, h, 0, 0] = tile0
        qta_ref[0, h, 0, 1] = tile1
    va = dot(2 * A_W)
    for h in range(A_HEADS):
        vta_ref[0, h, 0] = _transposed_value_tiles(va[:, h * vd:(h + 1) * vd], ATT_BLOCK)[0]
    vb = dot(3 * A_W + 2 * B_W)
    for hp in range(B_HEADS // 2):
        tiles = _transposed_value_tiles(vb[:, hp * vd:(hp + 1) * vd], BAND_BLOCK)
        for n, tile in enumerate(tiles):
            vtb_ref[0, hp, n] = tile


def _proj_even(x2d, w_bf16, bsz, seq):
    t, k = x2d.shape
    n = w_bf16.shape[1]
    assert ROW_BLOCK == ATT_BLOCK and seq % ROW_BLOCK == 0
    per_seq = seq // ROW_BLOCK
    vd = 2 * HEAD_DIM
    vrows = vd + BF16_ROWS
    nba, nbb, per_step_b = seq // ATT_BLOCK, seq // BAND_BLOCK, ROW_BLOCK // BAND_BLOCK
    seq_block = lambda i: (i // per_seq, 0, i % per_seq, 0, 0)
    seq_block6 = lambda i: (i // per_seq, 0, i % per_seq, 0, 0, 0)
    return pl.pallas_call(
        _proj_even_kernel,
        grid=(t // ROW_BLOCK,),
        in_specs=[pl.BlockSpec((ROW_BLOCK, k), lambda i: (i, 0)), _const_spec((k, n))],
        out_specs=[pl.BlockSpec((ROW_BLOCK, 2 * A_W + 2 * B_W), lambda i: (i, 0)),
                   pl.BlockSpec((1, A_HEADS, 1, 2, vd, ATT_BLOCK), seq_block6),
                   pl.BlockSpec((1, A_HEADS, 1, vrows, ATT_BLOCK), seq_block),
                   pl.BlockSpec((1, B_HEADS // 2, per_step_b, vrows, BAND_BLOCK), seq_block)],
        out_shape=[jax.ShapeDtypeStruct((t, 2 * A_W + 2 * B_W), BF16),
                   jax.ShapeDtypeStruct((bsz, A_HEADS, nba, 2, vd, ATT_BLOCK), BF16),
                   jax.ShapeDtypeStruct((bsz, A_HEADS, nba, vrows, ATT_BLOCK), BF16),
                   jax.ShapeDtypeStruct((bsz, B_HEADS // 2, nbb, vrows, BAND_BLOCK), BF16)],
        compiler_params=_params(("parallel",)),
        name="in_proj_even",
    )(x2d, w_bf16)


def _mix_ffn_kernel(a_ref, b_ref, x_ref, wo_ref, g1_ref, b1_ref, wg_ref, wu_ref, wd_ref,
                    g2_ref, b2_ref, o_ref):
    ka = a_ref.shape[-1]
    half = x_ref.shape[0] // 2
    halves = (slice(0, half), slice(half, 2 * half))
    mix = [jnp.dot(a_ref[rows, :], wo_ref[0:ka, :], preferred_element_type=F32)
           + jnp.dot(b_ref[rows, :], wo_ref[ka:, :], preferred_element_type=F32) for rows in halves]
    x1 = [_layer_norm(ALPHA * x_ref[rows, :] + m, g1_ref[...], b1_ref[...]) for rows, m in zip(halves, mix)]
    xb = [v.astype(BF16) for v in x1]
    gate = [jnp.dot(v, wg_ref[...], preferred_element_type=F32) for v in xb]
    up = [jnp.dot(v, wu_ref[...], preferred_element_type=F32) for v in xb]
    h = [(g * jax.nn.sigmoid(g) * u).astype(BF16) for g, u in zip(gate, up)]
    y = [ALPHA * v + jnp.dot(hh, wd_ref[...], preferred_element_type=F32) for v, hh in zip(x1, h)]
    for rows, v in zip(halves, y):
        o_ref[rows, :] = _layer_norm(v, g2_ref[...], b2_ref[...])


def _mix_ffn(a, b, x2d, wo, g1, b1, wg, wu, wd, g2, b2):
    t, d = x2d.shape
    ka, kb = a.shape[1], b.shape[1]
    f = wg.shape[1]
    vec = lambda v: v.reshape(1, d)
    return pl.pallas_call(
        _mix_ffn_kernel,
        grid=(t // ROW_BLOCK,),
        in_specs=[pl.BlockSpec((ROW_BLOCK, ka), lambda i: (i, 0)),
                  pl.BlockSpec((ROW_BLOCK, kb), lambda i: (i, 0)),
                  pl.BlockSpec((ROW_BLOCK, d), lambda i: (i, 0)),
                  _const_spec((ka + kb, d)), _const_spec((1, d)), _const_spec((1, d)),
                  _const_spec((d, f)), _const_spec((d, f)), _const_spec((f, d)),
                  _const_spec((1, d)), _const_spec((1, d))],
        out_specs=pl.BlockSpec((ROW_BLOCK, d), lambda i: (i, 0)),
        out_shape=jax.ShapeDtypeStruct((t, d), F32),
        compiler_params=_params(("parallel",)),
        name="mix_ffn",
    )(a, b, x2d, wo, vec(g1), vec(b1), wg, wu, wd, vec(g2), vec(b2))


def _diff_attn_kernel(qt_ref, k_ref, vt_ref, slope_ref, diag_ref, diag_max_ref, step_ref,
                      lq1_ref, lk1_ref, lq2_ref, lk2_ref, g_ref, o_ref,
                      qa_ref, sa_ref, sb_ref, ra_ref, rb_ref, m_ref, acc_ref, knorm_ref, *, lam_init):
    t = ATT_BLOCK
    n_chunks = t // CHUNK
    i = pl.program_id(2)
    qt_ref = qt_ref.at[0, 0, 0]
    vd = 2 * HEAD_DIM
    for mp in range(2):
        qa_ref[mp, 0:vd] = qt_ref[mp]
        qa_ref[mp, vd:2 * vd] = slope_ref[0]

    m_ref[...] = jnp.full(m_ref.shape, NEG_INF, F32)
    acc_ref[...] = jnp.zeros(acc_ref.shape, F32)

    @pl.when(i == 0)
    def _():
        lane = lax.broadcasted_iota(jnp.int32, (t, 2 * HEAD_DIM), 1)

        def block_norms(j, carry):
            kb = k_ref[0, pl.ds(pl.multiple_of(j * t, t), t), 0:2 * HEAD_DIM].astype(F32)
            sq = kb * kb
            n1 = jnp.sum(jnp.where(lane < HEAD_DIM, sq, 0.0), axis=1, keepdims=True)
            n2 = jnp.sum(jnp.where(lane >= HEAD_DIM, sq, 0.0), axis=1, keepdims=True)
            return (jnp.maximum(carry[0], jnp.max(n1, axis=0, keepdims=True)),
                    jnp.maximum(carry[1], jnp.max(n2, axis=0, keepdims=True)))

        zero = jnp.zeros((1, 1), F32)
        n1, n2 = lax.fori_loop(0, k_ref.shape[1] // t, block_norms, (zero, zero))
        knorm_ref[0] = jnp.broadcast_to(jnp.sqrt(n1), (1, t))
        knorm_ref[1] = jnp.broadcast_to(jnp.sqrt(n2), (1, t))

    def scores(j, s_ref, r_ref, diagonal=False):
        start = pl.multiple_of(j * t, t)
        for mp in range(2):
            if diagonal:
                s = jnp.dot(k_ref[0, pl.ds(start, t), 0:vd], qt_ref[mp], preferred_element_type=F32)
            else:
                s = jnp.dot(k_ref[0, pl.ds(start, t), :], qa_ref[mp], preferred_element_type=F32)
            s_ref[mp] = s
            r_ref[mp] = jnp.max(s.reshape(n_chunks, CHUNK // 8, 8, t), axis=1)

    def softmax_pv(j, s_ref, r_ref, diagonal=False):
        vt_blk = vt_ref[0, 0, j]
        step = step_ref[0]
        for mp in range(2):
            m_prev = m_ref[mp]
            chunk_top = r_ref[mp] + diag_max_ref[0][:, None, :] if diagonal else r_ref[mp]
            bound = jnp.max(jnp.max(chunk_top, axis=0), axis=0, keepdims=True)
            m_new = jnp.maximum(m_prev, bound)
            alpha = jnp.exp2(m_prev - m_new)
            st = s_ref[mp] + diag_ref[0] if diagonal else s_ref[mp]
            p = jnp.exp2(st - m_new)
            acc_ref[mp] = alpha * acc_ref[mp] + jnp.dot(vt_blk, p.astype(BF16),
                                                        preferred_element_type=F32)
            m_ref[mp] = m_new + step

    def prev_block(r):
        return jnp.maximum(i - r, 0)

    scores(i, sa_ref, ra_ref, diagonal=True)
    scores(prev_block(1), sb_ref, rb_ref)
    softmax_pv(i, sa_ref, ra_ref, diagonal=True)

    tile_top = step_ref[0] * ((t - 1) / t)
    inv_step = 1.0 / step_ref[0]
    reach = jnp.zeros((1, t), F32)
    for mp in range(2):
        qb = qt_ref[mp].astype(F32)
        qnorm = jnp.sqrt(jnp.sum(qb * qb, axis=0, keepdims=True))
        gap = qnorm * knorm_ref[mp] * NORM_SLACK + tile_top - m_ref[mp]
        reach = jnp.maximum(reach, (gap + SKIP_LOG2) * inv_step + 1.0)
    reach = jnp.floor(jnp.minimum(reach, float(k_ref.shape[1] // t))).astype(jnp.int32)
    n_keep = jnp.minimum(jnp.max(reach), i)

    def past_pair(r0):
        scores(prev_block(r0 + 1), sa_ref, ra_ref)
        softmax_pv(i - r0, sb_ref, rb_ref)
        scores(prev_block(r0 + 2), sb_ref, rb_ref)
        softmax_pv(i - r0 - 1, sa_ref, ra_ref)

    def past_oct(jj, carry):
        for pr in range(4):
            past_pair(8 * jj + 2 * pr + 1)
        return carry

    n_octs = lax.shift_right_logical(n_keep, 3)
    lax.fori_loop(0, n_octs, past_oct, 0)
    done = 8 * n_octs

    @pl.when((n_keep & 4) == 4)
    def _():
        past_pair(done + 1)
        past_pair(done + 3)

    @pl.when((n_keep & 2) == 2)
    def _():
        past_pair(done + (n_keep & 4) + 1)

    @pl.when((n_keep & 1) == 1)
    def _():
        softmax_pv(i - n_keep, sb_ref, rb_ref)

    lam = (jnp.exp(jnp.sum(lq1_ref[...] * lk1_ref[...], axis=-1, keepdims=True))
           - jnp.exp(jnp.sum(lq2_ref[...] * lk2_ref[...], axis=-1, keepdims=True)) + lam_init)
    ot = (acc_ref[0, 0:vd] / acc_ref[0, vd:vd + 1]
          - lam * (acc_ref[1, 0:vd] / acc_ref[1, vd:vd + 1]))
    ms = jnp.mean(ot * ot, axis=0, keepdims=True)
    y = (ot * lax.rsqrt(ms + LN_EPS)).T * g_ref[...] * (1.0 - lam_init)
    o_ref[0] = y.astype(o_ref.dtype)


def _diff_attention(keys, qt, vt, lam_q1, lam_k1, lam_q2, lam_k2, subln_g, lam_init):
    bsz, seq, _ = keys.shape
    t = ATT_BLOCK
    nb = seq // t
    n_chunks = t // CHUNK
    vd = 2 * HEAD_DIM
    slopes = 2.0 ** (-8.0 * jnp.arange(1, A_HEADS + 1, dtype=F32) / A_HEADS)
    kk = jnp.arange(t)[:, None]
    qq = jnp.arange(t)[None, :]
    rel = (qq - kk).astype(F32)
    c = slopes * LOG2E
    allowed = (kk // CHUNK) <= (qq // CHUNK)
    diag = jnp.where(allowed[None], c[:, None, None] * (qq.astype(F32) - jnp.abs(rel))[None], NEG_INF)
    terms, rest = [], c
    for _ in range(POS_TERMS):
        terms.append(rest.astype(BF16))
        rest = rest - terms[-1].astype(F32)
    slope_rows = jnp.stack(terms + terms + [jnp.zeros_like(terms[0])] * (vd - 2 * POS_TERMS), axis=1)
    slope_tile = jnp.broadcast_to(slope_rows[:, :, None], (A_HEADS, vd, t))
    chunk_max = lambda tile: jnp.max(tile.reshape(A_HEADS, n_chunks, CHUNK, t), axis=2)
    step = jnp.broadcast_to((slopes * (LOG2E * t))[:, None, None], (A_HEADS, 1, t))
    vec = lambda v: v.reshape(1, HEAD_DIM).astype(F32)
    small = lambda shape: pl.BlockSpec(shape, lambda b, h, i: (0,) * len(shape))
    per_head = lambda shape: pl.BlockSpec(shape, lambda b, h, i: (h,) + (0,) * (len(shape) - 1))
    return pl.pallas_call(
        functools.partial(_diff_attn_kernel, lam_init=lam_init),
        grid=(bsz, A_HEADS, nb),
        in_specs=[pl.BlockSpec((1, 1, 1, 2, vd, t), lambda b, h, i: (b, h, i, 0, 0, 0)),
                  pl.BlockSpec((1, seq, 2 * vd), lambda b, h, i: (b, 0, h)),
                  pl.BlockSpec((1, 1, nb, vd + BF16_ROWS, t), lambda b, h, i: (b, h, 0, 0, 0)),
                  per_head((1, vd, t)), per_head((1, t, t)),
                  per_head((1, n_chunks, t)), per_head((1, 1, t)),
                  small((1, HEAD_DIM)), small((1, HEAD_DIM)), small((1, HEAD_DIM)), small((1, HEAD_DIM)),
                  small((1, vd))],
        out_specs=pl.BlockSpec((1, t, vd), lambda b, h, i: (b, i, h)),
        out_shape=jax.ShapeDtypeStruct((bsz, seq, A_W), BF16),
        scratch_shapes=[pltpu.VMEM((2, 2 * vd, t), BF16),
                        pltpu.VMEM((2, t, t), F32), pltpu.VMEM((2, t, t), F32),
                        pltpu.VMEM((2, n_chunks, 8, t), F32), pltpu.VMEM((2, n_chunks, 8, t), F32),
                        pltpu.VMEM((2, 1, t), F32),
                        pltpu.VMEM((2, vd + BF16_ROWS, t), F32),
                        pltpu.VMEM((2, 1, t), F32)],
        compiler_params=_params(("parallel", "parallel", "arbitrary")),
        name="diff_attn",
    )(qt, keys, vt, slope_tile, diag, chunk_max(diag), step,
      vec(lam_q1), vec(lam_k1), vec(lam_q2), vec(lam_k2), subln_g.reshape(1, vd).astype(F32))


def _band_attn_kernel(q_ref, k_ref, vt_ref, diag_vals_ref, band_ref, o_ref, bias_ref, sa_ref, sb_ref):
    t = BAND_BLOCK
    nb = k_ref.shape[1] // t
    n_win = BAND_WINDOW // t
    vd = 2 * HEAD_DIM
    row = lax.broadcasted_iota(jnp.int32, (vd, t), 0)
    n_vals = diag_vals_ref.shape[-1]

    for hh in range(2):
        vals = jnp.broadcast_to(diag_vals_ref[hh], (8, n_vals))
        for c0 in range(0, BAND_WINDOW, 8):
            rolled = pltpu.roll(vals, (c0 - (BAND_WINDOW - 1)) % n_vals, axis=1, stride=1, stride_axis=0)
            bias_ref[hh, c0:c0 + 8, :] = jnp.where(band_ref[c0:c0 + 8, :] > 0.0, rolled[:, 0:t], NEG_INF)

    def rows(n):
        return pl.ds(pl.multiple_of(n * t, t), t)

    def key_block(n, w):
        return jnp.maximum(n - (n_win - 1) + w, 0)

    def scores(n, s_ref):
        qt = q_ref[0, rows(n), :].astype(F32).T * (HEAD_DIM ** -0.5 * LOG2E)
        for hh in range(2):
            in_head = (row < HEAD_DIM) if hh == 0 else (row >= HEAD_DIM)
            qh = jnp.where(in_head, qt, 0.0).astype(BF16)
            for w in range(n_win):
                s_ref[hh, w * t:(w + 1) * t] = jnp.dot(k_ref[0, rows(key_block(n, w)), :], qh,
                                                       preferred_element_type=F32)

    def softmax_pv(n, s_ref, mask_keys):
        outs = []
        for hh in range(2):
            st = s_ref[hh] + bias_ref[hh]
            if mask_keys:
                key_pos = (n - (n_win - 1)) * t + lax.broadcasted_iota(jnp.int32, (BAND_WINDOW, 1), 0)
                st = jnp.where(key_pos >= 0, st, NEG_INF)
            p = jnp.exp2(st - jnp.max(st, axis=0, keepdims=True)).astype(BF16)
            ot = jnp.dot(vt_ref[0, 0, key_block(n, 0)], p[0:t], preferred_element_type=F32)
            for w in range(1, n_win):
                ot = ot + jnp.dot(vt_ref[0, 0, key_block(n, w)], p[w * t:(w + 1) * t],
                                  preferred_element_type=F32)
            outs.append(ot[0:vd] / ot[vd:vd + 1])
        o_ref[0, rows(n), :] = jnp.where(row < HEAD_DIM, outs[0], outs[1]).T.astype(o_ref.dtype)

    for n in range(n_win - 1):
        scores(n, sa_ref)
        softmax_pv(n, sa_ref, True)

    first = n_win - 1
    scores(first, sa_ref)

    def pair(jj, carry):
        n0 = first + 2 * jj
        scores(n0 + 1, sb_ref)
        softmax_pv(n0, sa_ref, False)
        scores(n0 + 2, sa_ref)
        softmax_pv(n0 + 1, sb_ref, False)
        return carry

    n_pairs = (nb - first - 2) // 2
    lax.fori_loop(0, n_pairs, pair, 0, unroll=BAND_UNROLL)
    for n in range(first + 2 * n_pairs, nb):
        if n + 1 < nb:
            scores(n + 1, sb_ref)
        softmax_pv(n, sa_ref, False)
        sa_ref, sb_ref = sb_ref, sa_ref


def _band_attention(hproj, vt, rel_bias):
    bsz, seq, _ = hproj.shape
    t = BAND_BLOCK
    vd = 2 * HEAD_DIM
    c = jnp.arange(BAND_WINDOW)[:, None]
    r = jnp.arange(t)[None, :]
    in_band = ((c // CHUNK) >= (r // CHUNK)) & ((c // CHUNK) <= (r // CHUNK) + B_LEFT_CHUNKS)
    rb = rel_bias.astype(F32) * LOG2E
    n_lo = BAND_WINDOW - 1 - B_LEFT_CHUNKS * CHUNK - REL_CLIP
    n_hi = t + BAND_WINDOW - n_lo - rb.shape[1]
    diag_vals = jnp.concatenate([jnp.broadcast_to(rb[:, :1], (B_HEADS, n_lo)), rb,
                                 jnp.broadcast_to(rb[:, -1:], (B_HEADS, n_hi))], axis=1)[:, None, :]
    qcol, kcol = (2 * A_W) // vd, (2 * A_W + B_W) // vd
    seq_spec = lambda col: pl.BlockSpec((1, seq, vd), lambda b, hp: (b, 0, col + hp))
    return pl.pallas_call(
        _band_attn_kernel,
        grid=(bsz, B_HEADS // 2),
        in_specs=[seq_spec(qcol), seq_spec(kcol),
                  pl.BlockSpec((1, 1, seq // t, vd + BF16_ROWS, t), lambda b, hp: (b, hp, 0, 0, 0)),
                  pl.BlockSpec((2, 1, t + BAND_WINDOW), lambda b, hp: (hp, 0, 0)),
                  pl.BlockSpec((BAND_WINDOW, t), lambda b, hp: (0, 0))],
        out_specs=pl.BlockSpec((1, seq, vd), lambda b, hp: (b, 0, hp)),
        out_shape=jax.ShapeDtypeStruct((bsz, seq, B_W), BF16),
        scratch_shapes=[pltpu.VMEM((2, BAND_WINDOW, t), F32),
                        pltpu.VMEM((2, BAND_WINDOW, t), F32), pltpu.VMEM((2, BAND_WINDOW, t), F32)],
        compiler_params=_params(("parallel", "parallel")),
        name="band_attn",
    )(hproj, hproj, vt, diag_vals, in_band.astype(F32))


def _retention_kernel(q_ref, k_ref, v_ref, gate_ref, dmask_ref, qdec_ref, kdec_ref, cdec_ref,
                      g_ref, beta_ref, o_ref, state_ref):
    ln = RET_BLOCK
    nb = q_ref.shape[1] // ln
    vd = 2 * HEAD_DIM
    scale = HEAD_DIM ** -0.5
    lane = lax.broadcasted_iota(jnp.int32, (ln, vd), 1)
    state_ref[...] = jnp.zeros(state_ref.shape, F32)

    def block(n, carry):
        rows = pl.ds(pl.multiple_of(n * ln, ln), ln)
        q = q_ref[0, rows, :]
        k = k_ref[0, rows, :]
        for hh in range(2):
            in_head = (lane < HEAD_DIM) if hh == 0 else (lane >= HEAD_DIM)
            qh = jnp.where(in_head, q, jnp.zeros_like(q))
            kh = jnp.where(in_head, k.astype(F32) * scale, 0.0)
            sl = slice(hh * vd, (hh + 1) * vd)
            v = v_ref[0, rows, sl]
            scores = lax.dot_general(qh, kh.astype(BF16), (((1,), (1,)), ((), ())),
                                     preferred_element_type=F32) * dmask_ref[hh]
            inner = jnp.dot(scores.astype(BF16), v, preferred_element_type=F32)
            state = state_ref[hh]
            cross = jnp.dot(qh, state.astype(BF16), preferred_element_type=F32) * qdec_ref[hh]
            kdt = (kh * kdec_ref[hh]).T.astype(BF16)
            kv = jnp.dot(kdt, v, preferred_element_type=F32)
            state_ref[hh] = cdec_ref[hh] * state + kv
            o = inner + cross
            mu = jnp.mean(o, axis=-1, keepdims=True)
            d = o - mu
            var = jnp.mean(d * d, axis=-1, keepdims=True)
            y = d * lax.rsqrt(var + LN_EPS) * g_ref[:, sl] + beta_ref[:, sl]
            gate = gate_ref[0, rows, sl]
            o_ref[0, rows, sl] = (y * (gate * jax.nn.sigmoid(gate))).astype(o_ref.dtype)
        return carry

    lax.fori_loop(0, nb, block, 0, unroll=8)


def _retention(qkv, gate, gn_g, gn_b):
    bsz, seq, _ = qkv.shape
    ln = RET_BLOCK
    vd = 2 * HEAD_DIM
    log_gamma = jnp.log(1.0 - 2.0 ** (-5.0 - jnp.arange(C_HEADS, dtype=F32)))
    idx = jnp.arange(ln, dtype=F32)
    diff = idx[:, None] - idx[None, :]
    dmask = jnp.where(diff >= 0, jnp.exp(log_gamma[:, None, None] * jnp.maximum(diff, 0.0)), 0.0)
    qdec = jnp.broadcast_to(jnp.exp(log_gamma[:, None] * (idx + 1.0))[:, :, None], (C_HEADS, ln, vd))
    kdec = jnp.broadcast_to(jnp.exp(log_gamma[:, None] * (ln - 1.0 - idx))[:, :, None], (C_HEADS, ln, vd))
    cdec = jnp.broadcast_to(jnp.exp(log_gamma * ln)[:, None, None], (C_HEADS, 1, vd))
    pair = lambda shape: pl.BlockSpec(shape, lambda b, hp: (hp,) + (0,) * (len(shape) - 1))
    return pl.pallas_call(
        _retention_kernel,
        grid=(bsz, C_HEADS // 2),
        in_specs=[pl.BlockSpec((1, seq, vd), lambda b, hp: (b, 0, hp)),
                  pl.BlockSpec((1, seq, vd), lambda b, hp: (b, 0, C_QK_W // vd + hp)),
                  pl.BlockSpec((1, seq, 2 * vd), lambda b, hp: (b, 0, (2 * C_QK_W) // (2 * vd) + hp)),
                  pl.BlockSpec((1, seq, 2 * vd), lambda b, hp: (b, 0, hp)),
                  pair((2, ln, ln)), pair((2, ln, vd)), pair((2, ln, vd)), pair((2, 1, vd)),
                  pl.BlockSpec((1, 2 * vd), lambda b, hp: (0, hp)),
                  pl.BlockSpec((1, 2 * vd), lambda b, hp: (0, hp))],
        out_specs=pl.BlockSpec((1, seq, 2 * vd), lambda b, hp: (b, 0, hp)),
        out_shape=jax.ShapeDtypeStruct((bsz, seq, C_W), BF16),
        scratch_shapes=[pltpu.VMEM((2, vd, vd), F32)],
        compiler_params=_params(("parallel", "parallel")),
        name="retention",
    )(qkv, qkv, qkv, gate, dmask, qdec, kdec, cdec, gn_g.reshape(1, C_W), gn_b.reshape(1, C_W))


def kernel(x, w_in_even, lam_q1, lam_k1, lam_q2, lam_k2, diff_subln_g, rel_bias, w_out_even,
           w_in_odd, ret_gn_g, ret_gn_b, sgu_ln_g, sgu_ln_b, sgu_w, sgu_b, w_out_odd,
           ln_mix_g, ln_mix_b, ffn_w_gate, ffn_w_up, ffn_w_down, ln_ffn_g, ln_ffn_b):
    bsz, seq, d = x.shape
    t = bsz * seq
    x2d = x.reshape(t, d)
    for l in range(DEPTH):
        j = l // 2
        if l % 2 == 0:
            hproj, qta, vta, vtb = _proj_even(x2d, w_in_even[j].astype(BF16), bsz, seq)
            hproj = hproj.reshape(bsz, seq, -1)
            lam_init = 0.8 - 0.6 * math.exp(-0.3 * l)
            mix_a = _diff_attention(hproj, qta, vta, lam_q1[j], lam_k1[j], lam_q2[j], lam_k2[j],
                                    diff_subln_g[j], lam_init)
            mix_b = _band_attention(hproj, vtb, rel_bias[j])
            w_out = w_out_even[j]
        else:
            qkv, gate, mix_b = _proj_odd(x2d, w_in_odd[j].astype(BF16), sgu_w[j], sgu_b[j],
                                         sgu_ln_g[j], sgu_ln_b[j])
            mix_a = _retention(qkv.reshape(bsz, seq, -1), gate.reshape(bsz, seq, -1),
                               ret_gn_g[j], ret_gn_b[j])
            w_out = w_out_odd[j]
        x2d = _mix_ffn(mix_a.reshape(t, -1), mix_b.reshape(t, -1), x2d, w_out.astype(BF16),
                       ln_mix_g[l], ln_mix_b[l], ffn_w_gate[l].astype(BF16), ffn_w_up[l].astype(BF16),
                       ffn_w_down[l].astype(BF16), ln_ffn_g[l], ln_ffn_b[l])
    return x2d.reshape(bsz, seq, d)
```

```python
import functools
import math

import jax
import jax.numpy as jnp
from jax import lax
from jax.experimental import pallas as pl
from jax.experimental.pallas import tpu as pltpu

F32 = jnp.float32
BF16 = jnp.bfloat16

D_MODEL = 1024
HEAD_DIM = 64
CHUNK = 64
A_HEADS = 4
A_W = 512
B_HEADS = 8
B_W = 512
B_LEFT_CHUNKS = 8
REL_CLIP = 128
C_HEADS = 4
C_QK_W = 256
C_W = 512
D_CHUNK = 128
D_GROUPS = 4
D_WIDTH = 512
DEPTH = 2
ALPHA = (2.0 * DEPTH) ** 0.25
LN_EPS = 1e-5
NEG_INF = -1e30
LOG2E = 1.4426950408889634
SKIP_LOG2 = 160.0
NORM_SLACK = 1.0 + 2.0 ** -8

VMEM_LIMIT_V7X = 56 * 1024 * 1024
BF16_ROWS = 16
POS_SPLIT = 256
POS_TERMS = 3

ROW_BLOCK = 512
ATT_BLOCK = 512
BAND_BLOCK = 256
BAND_WINDOW = BAND_BLOCK + B_LEFT_CHUNKS * CHUNK
BAND_UNROLL = 7
RET_BLOCK = 256


def _params(semantics):
    return pltpu.CompilerParams(dimension_semantics=semantics, vmem_limit_bytes=VMEM_LIMIT_V7X)


def _const_spec(shape):
    nd = len(shape)
    return pl.BlockSpec(shape, lambda *_: (0,) * nd, pipeline_mode=pl.Buffered(1))


def _layer_norm(y, g, b):
    mu = jnp.mean(y, axis=-1, keepdims=True)
    d = y - mu
    var = jnp.mean(d * d, axis=-1, keepdims=True)
    return d * lax.rsqrt(var + LN_EPS) * g + b


def _proj_odd_kernel(x_ref, w_ref, ws_ref, bt_ref, g_ref, beta_ref, qkv_ref, gate_ref, od_ref):
    x = x_ref[...].astype(BF16)
    dot = lambda c: jnp.dot(x, w_ref[:, c:c + 512], preferred_element_type=F32)
    off_d = 2 * C_QK_W + 2 * C_W
    u = dot(off_d)
    v = dot(off_d + D_WIDTH)
    qkv_ref[:, 0:512] = dot(0).astype(BF16)
    qkv_ref[:, 512:1024] = dot(512).astype(BF16)
    gate_ref[...] = dot(2 * C_QK_W + C_W)
    u = jax.nn.gelu(u)
    v = _layer_norm(jax.nn.gelu(v), g_ref[...], beta_ref[...])
    gw = D_WIDTH // D_GROUPS
    r = lax.broadcasted_iota(jnp.int32, (D_CHUNK, D_CHUNK), 0)
    c = lax.broadcasted_iota(jnp.int32, (D_CHUNK, D_CHUNK), 1)
    for g in range(D_GROUPS):
        w = jnp.where(r >= c, ws_ref[g], 0.0).astype(BF16)
        bias = bt_ref[:, g:g + 1]
        cols = slice(g * gw, (g + 1) * gw)
        vg = v[:, cols].astype(BF16)
        for n in range(ROW_BLOCK // D_CHUNK):
            rows = slice(n * D_CHUNK, (n + 1) * D_CHUNK)
            mixed = jnp.dot(w, vg[rows], preferred_element_type=F32) + bias
            od_ref[rows, cols] = (u[rows, cols] * mixed).astype(od_ref.dtype)


def _proj_odd(x2d, w_bf16, w_s, b_s, ln_g, ln_b):
    t, k = x2d.shape
    n = w_bf16.shape[1]
    assert ROW_BLOCK % D_CHUNK == 0 and n == 2 * C_QK_W + 2 * C_W + 2 * D_WIDTH
    row_spec = lambda wd: pl.BlockSpec((ROW_BLOCK, wd), lambda i: (i, 0))
    return pl.pallas_call(
        _proj_odd_kernel,
        grid=(t // ROW_BLOCK,),
        in_specs=[row_spec(k), _const_spec((k, n)), _const_spec((D_GROUPS, D_CHUNK, D_CHUNK)),
                  _const_spec((D_CHUNK, D_GROUPS)), _const_spec((1, D_WIDTH)), _const_spec((1, D_WIDTH))],
        out_specs=[row_spec(2 * C_QK_W + C_W), row_spec(C_W), row_spec(D_WIDTH)],
        out_shape=[jax.ShapeDtypeStruct((t, 2 * C_QK_W + C_W), BF16),
                   jax.ShapeDtypeStruct((t, C_W), F32),
                   jax.ShapeDtypeStruct((t, D_WIDTH), BF16)],
        compiler_params=_params(("parallel",)),
        name="in_proj_odd",
    )(x2d, w_bf16, w_s.astype(F32), b_s.T.astype(F32), ln_g.reshape(1, D_WIDTH), ln_b.reshape(1, D_WIDTH))


def _transposed_value_tiles(vals, block):
    ones_tile = (lax.broadcasted_iota(jnp.int32, (BF16_ROWS, block), 0) == 0).astype(BF16)
    return [jnp.concatenate([vals[r:r + block].T.astype(BF16), ones_tile], axis=0)
            for r in range(0, vals.shape[0], block)]


def _transposed_query_tiles(q, block):
    tiles = []
    for r in range(0, q.shape[0], block):
        qt = q[r:r + block].T * (HEAD_DIM ** -0.5 * LOG2E)
        row = lax.broadcasted_iota(jnp.int32, qt.shape, 0)
        tiles.append((jnp.where(row < HEAD_DIM, qt, 0.0).astype(BF16),
                      jnp.where(row >= HEAD_DIM, qt, 0.0).astype(BF16)))
    return tiles


def _proj_even_kernel(x_ref, w_ref, h_ref, qta_ref, vta_ref, vtb_ref):
    x = x_ref[...].astype(BF16)
    vd = 2 * HEAD_DIM
    dot = lambda c: jnp.dot(x, w_ref[:, c:c + 512], preferred_element_type=F32)
    vb = dot(3 * A_W + 2 * B_W)
    for hp in range(B_HEADS // 2):
        tiles = _transposed_value_tiles(vb[:, hp * vd:(hp + 1) * vd], BAND_BLOCK)
        for n, tile in enumerate(tiles):
            vtb_ref[0, hp, n] = tile
    va = dot(2 * A_W)
    for h in range(A_HEADS):
        vta_ref[0, h, 0] = _transposed_value_tiles(va[:, h * vd:(h + 1) * vd], ATT_BLOCK)[0]
    qa = dot(0)
    for h in range(A_HEADS):
        tile0, tile1 = _transposed_query_tiles(qa[:, h * vd:(h + 1) * vd], ATT_BLOCK)[0]
        qta_ref[0, h, 0, 0] = tile0
        qta_ref[0, h, 0, 1] = tile1
    ka = dot(A_W)
    r = lax.broadcasted_iota(jnp.int32, (ROW_BLOCK, vd), 0)
    col = lax.broadcasted_iota(jnp.int32, (ROW_BLOCK, vd), 1)
    pos = jnp.where(col < POS_TERMS, r & (POS_SPLIT - 1),
                    jnp.where(col < 2 * POS_TERMS, r & ~(POS_SPLIT - 1), 0)).astype(F32).astype(BF16)
    for h in range(A_HEADS):
        h_ref[:, 2 * vd * h:2 * vd * h + vd] = ka[:, h * vd:(h + 1) * vd].astype(BF16)
        h_ref[:, 2 * vd * h + vd:2 * vd * (h + 1)] = pos
    h_ref[:, 2 * A_W:2 * A_W + B_W] = dot(3 * A_W).astype(BF16)
    h_ref[:, 2 * A_W + B_W:2 * A_W + 2 * B_W] = dot(3 * A_W + B_W).astype(BF16)


def _proj_even(x2d, w_bf16, bsz, seq):
    t, k = x2d.shape
    n = w_bf16.shape[1]
    assert ROW_BLOCK == ATT_BLOCK and seq % ROW_BLOCK == 0
    per_seq = seq // ROW_BLOCK
    vd = 2 * HEAD_DIM
    vrows = vd + BF16_ROWS
    nba, nbb, per_step_b = seq // ATT_BLOCK, seq // BAND_BLOCK, ROW_BLOCK // BAND_BLOCK
    seq_block = lambda i: (i // per_seq, 0, i % per_seq, 0, 0)
    seq_block6 = lambda i: (i // per_seq, 0, i % per_seq, 0, 0, 0)
    return pl.pallas_call(
        _proj_even_kernel,
        grid=(t // ROW_BLOCK,),
        in_specs=[pl.BlockSpec((ROW_BLOCK, k), lambda i: (i, 0)), _const_spec((k, n))],
        out_specs=[pl.BlockSpec((ROW_BLOCK, 2 * A_W + 2 * B_W), lambda i: (i, 0)),
                   pl.BlockSpec((1, A_HEADS, 1, 2, vd, ATT_BLOCK), seq_block6),
                   pl.BlockSpec((1, A_HEADS, 1, vrows, ATT_BLOCK), seq_block),
                   pl.BlockSpec((1, B_HEADS // 2, per_step_b, vrows, BAND_BLOCK), seq_block)],
        out_shape=[jax.ShapeDtypeStruct((t, 2 * A_W + 2 * B_W), BF16),
                   jax.ShapeDtypeStruct((bsz, A_HEADS, nba, 2, vd, ATT_BLOCK), BF16),
                   jax.ShapeDtypeStruct((bsz, A_HEADS, nba, vrows, ATT_BLOCK), BF16),
                   jax.ShapeDtypeStruct((bsz, B_HEADS // 2, nbb, vrows, BAND_BLOCK), BF16)],
        compiler_params=_params(("parallel",)),
        name="in_proj_even",
    )(x2d, w_bf16)


def _mix_ffn_kernel(a_ref, b_ref, x_ref, wo_ref, g1_ref, b1_ref, wg_ref, wu_ref, wd_ref,
                    g2_ref, b2_ref, o_ref):
    ka = a_ref.shape[-1]
    half = x_ref.shape[0] // 2
    halves = (slice(0, half), slice(half, 2 * half))
    mix = [jnp.dot(a_ref[rows, :], wo_ref[0:ka, :], preferred_element_type=F32)
           + jnp.dot(b_ref[rows, :], wo_ref[ka:, :], preferred_element_type=F32) for rows in halves]
    x1 = [_layer_norm(ALPHA * x_ref[rows, :] + m, g1_ref[...], b1_ref[...]) for rows, m in zip(halves, mix)]
    xb = [v.astype(BF16) for v in x1]
    gate = [jnp.dot(v, wg_ref[...], preferred_element_type=F32) for v in xb]
    up = [jnp.dot(v, wu_ref[...], preferred_element_type=F32) for v in xb]
    h = [(g * jax.nn.sigmoid(g) * u).astype(BF16) for g, u in zip(gate, up)]
    y = [ALPHA * v + jnp.dot(hh, wd_ref[...], preferred_element_type=F32) for v, hh in zip(x1, h)]
    for rows, v in zip(halves, y):
        o_ref[rows, :] = _layer_norm(v, g2_ref[...], b2_ref[...])


def _mix_ffn(a, b, x2d, wo, g1, b1, wg, wu, wd, g2, b2):
    t, d = x2d.shape
    ka, kb = a.shape[1], b.shape[1]
    f = wg.shape[1]
    vec = lambda v: v.reshape(1, d)
    return pl.pallas_call(
        _mix_ffn_kernel,
        grid=(t // ROW_BLOCK,),
        in_specs=[pl.BlockSpec((ROW_BLOCK, ka), lambda i: (i, 0)),
                  pl.BlockSpec((ROW_BLOCK, kb), lambda i: (i, 0)),
                  pl.BlockSpec((ROW_BLOCK, d), lambda i: (i, 0)),
                  _const_spec((ka + kb, d)), _const_spec((1, d)), _const_spec((1, d)),
                  _const_spec((d, f)), _const_spec((d, f)), _const_spec((f, d)),
                  _const_spec((1, d)), _const_spec((1, d))],
        out_specs=pl.BlockSpec((ROW_BLOCK, d), lambda i: (i, 0)),
        out_shape=jax.ShapeDtypeStruct((t, d), F32),
        compiler_params=_params(("parallel",)),
        name="mix_ffn",
    )(a, b, x2d, wo, vec(g1), vec(b1), wg, wu, wd, vec(g2), vec(b2))


def _diff_attn_kernel(qt_ref, k_ref, vt_ref, slope_ref, diag_ref, diag_max_ref, step_ref,
                      lq1_ref, lk1_ref, lq2_ref, lk2_ref, g_ref, o_ref,
                      qa_ref, sa_ref, sb_ref, ra_ref, rb_ref, m_ref, acc_ref, knorm_ref, *, lam_init):
    t = ATT_BLOCK
    n_chunks = t // CHUNK
    i = pl.program_id(2)
    qt_ref = qt_ref.at[0, 0, 0]
    vd = 2 * HEAD_DIM
    for mp in range(2):
        qa_ref[mp, 0:vd] = qt_ref[mp]
        qa_ref[mp, vd:2 * vd] = slope_ref[0]

    @pl.when(i == 0)
    def _():
        lane = lax.broadcasted_iota(jnp.int32, (t, 2 * HEAD_DIM), 1)

        def block_norms(j, carry):
            kb = k_ref[0, pl.ds(pl.multiple_of(j * t, t), t), 0:2 * HEAD_DIM].astype(F32)
            sq = kb * kb
            n1 = jnp.sum(jnp.where(lane < HEAD_DIM, sq, 0.0), axis=1, keepdims=True)
            n2 = jnp.sum(jnp.where(lane >= HEAD_DIM, sq, 0.0), axis=1, keepdims=True)
            return (jnp.maximum(carry[0], jnp.max(n1, axis=0, keepdims=True)),
                    jnp.maximum(carry[1], jnp.max(n2, axis=0, keepdims=True)))

        zero = jnp.zeros((1, 1), F32)
        n1, n2 = lax.fori_loop(0, k_ref.shape[1] // t, block_norms, (zero, zero))
        knorm_ref[0] = jnp.broadcast_to(jnp.sqrt(n1), (1, t))
        knorm_ref[1] = jnp.broadcast_to(jnp.sqrt(n2), (1, t))

    half = t // 2
    lo, hi = slice(0, half), slice(half, t)

    def chunk_maxima(s):
        return jnp.max(s.reshape(s.shape[0] // CHUNK, CHUNK // 8, 8, s.shape[1]), axis=1)

    def diagonal_scores():
        start = pl.multiple_of(i * t, t)
        for mp in range(2):
            top = jnp.dot(k_ref[0, pl.ds(start, half), 0:vd], qt_ref[mp], preferred_element_type=F32)
            bot = jnp.dot(k_ref[0, pl.ds(pl.multiple_of(start + half, half), half), 0:vd],
                          qt_ref[mp, :, hi], preferred_element_type=F32)
            sa_ref[mp, lo] = top
            sa_ref[mp, hi, hi] = bot
            ra_ref[mp, 0:n_chunks // 2] = chunk_maxima(top)
            ra_ref[mp, n_chunks // 2:, :, hi] = chunk_maxima(bot)
            ra_ref[mp, n_chunks // 2:, :, lo] = jnp.zeros((n_chunks // 2, 8, half), F32)

    def diagonal_softmax_pv():
        vt_blk = vt_ref[0, 0, i]
        for mp in range(2):
            bound = jnp.max(ra_ref[mp] + diag_max_ref[0][:, None, :], axis=0)
            m_new = jnp.max(bound, axis=0, keepdims=True)
            p_top = jnp.exp2(sa_ref[mp, lo] + diag_ref[0, lo] - m_new).astype(BF16)
            p_bot = jnp.exp2(sa_ref[mp, hi, hi] + diag_ref[0, hi, hi] - m_new[:, hi]).astype(BF16)
            acc = jnp.dot(vt_blk[:, lo], p_top, preferred_element_type=F32)
            acc_ref[mp, :, lo] = acc[:, lo]
            acc_ref[mp, :, hi] = acc[:, hi] + jnp.dot(vt_blk[:, hi], p_bot, preferred_element_type=F32)
            m_ref[mp] = m_new + step_ref[0]

    def scores(j, s_ref, r_ref):
        start = pl.multiple_of(j * t, t)
        for mp in range(2):
            s = jnp.dot(k_ref[0, pl.ds(start, t), :], qa_ref[mp], preferred_element_type=F32)
            s_ref[mp] = s
            r_ref[mp] = chunk_maxima(s)

    def softmax_pv(j, s_ref, r_ref):
        vt_blk = vt_ref[0, 0, j]
        step = step_ref[0]
        for mp in range(2):
            m_prev = m_ref[mp]
            bound = jnp.max(jnp.max(r_ref[mp], axis=0), axis=0, keepdims=True)
            m_new = jnp.maximum(m_prev, bound)
            alpha = jnp.exp2(m_prev - m_new)
            p = jnp.exp2(s_ref[mp] - m_new)
            acc_ref[mp] = alpha * acc_ref[mp] + jnp.dot(vt_blk, p.astype(BF16),
                                                        preferred_element_type=F32)
            m_ref[mp] = m_new + step

    def prev_block(r):
        return jnp.maximum(i - r, 0)

    diagonal_scores()
    scores(prev_block(1), sb_ref, rb_ref)
    diagonal_softmax_pv()

    tile_top = step_ref[0] * ((t - 1) / t)
    inv_step = 1.0 / step_ref[0]
    reach = jnp.zeros((1, t), F32)
    for mp in range(2):
        qb = qt_ref[mp].astype(F32)
        qnorm = jnp.sqrt(jnp.sum(qb * qb, axis=0, keepdims=True))
        gap = qnorm * knorm_ref[mp] * NORM_SLACK + tile_top - m_ref[mp]
        reach = jnp.maximum(reach, (gap + SKIP_LOG2) * inv_step + 1.0)
    reach = jnp.floor(jnp.minimum(reach, float(k_ref.shape[1] // t))).astype(jnp.int32)
    n_keep = jnp.minimum(jnp.max(reach), i)

    def past_pair(r0):
        scores(prev_block(r0 + 1), sa_ref, ra_ref)
        softmax_pv(i - r0, sb_ref, rb_ref)
        scores(prev_block(r0 + 2), sb_ref, rb_ref)
        softmax_pv(i - r0 - 1, sa_ref, ra_ref)

    def past_oct(jj, carry):
        for pr in range(4):
            past_pair(8 * jj + 2 * pr + 1)
        return carry

    n_octs = lax.shift_right_logical(n_keep, 3)
    lax.fori_loop(0, n_octs, past_oct, 0)
    done = 8 * n_octs

    @pl.when((n_keep & 4) == 4)
    def _():
        past_pair(done + 1)
        past_pair(done + 3)

    @pl.when((n_keep & 2) == 2)
    def _():
        past_pair(done + (n_keep & 4) + 1)

    @pl.when((n_keep & 1) == 1)
    def _():
        softmax_pv(i - n_keep, sb_ref, rb_ref)

    lam = (jnp.exp(jnp.sum(lq1_ref[...] * lk1_ref[...], axis=-1, keepdims=True))
           - jnp.exp(jnp.sum(lq2_ref[...] * lk2_ref[...], axis=-1, keepdims=True)) + lam_init)
    ot = (acc_ref[0, 0:vd] / acc_ref[0, vd:vd + 1]
          - lam * (acc_ref[1, 0:vd] / acc_ref[1, vd:vd + 1]))
    ms = jnp.mean(ot * ot, axis=0, keepdims=True)
    y = (ot * lax.rsqrt(ms + LN_EPS)).T * g_ref[...] * (1.0 - lam_init)
    o_ref[0] = y.astype(o_ref.dtype)


def _diff_attention(keys, qt, vt, lam_q1, lam_k1, lam_q2, lam_k2, subln_g, lam_init):
    bsz, seq, _ = keys.shape
    t = ATT_BLOCK
    nb = seq // t
    n_chunks = t // CHUNK
    vd = 2 * HEAD_DIM
    slopes = 2.0 ** (-8.0 * jnp.arange(1, A_HEADS + 1, dtype=F32) / A_HEADS)
    kk = jnp.arange(t)[:, None]
    qq = jnp.arange(t)[None, :]
    rel = (qq - kk).astype(F32)
    c = slopes * LOG2E
    allowed = (kk // CHUNK) <= (qq // CHUNK)
    diag = jnp.where(allowed[None], c[:, None, None] * (qq.astype(F32) - jnp.abs(rel))[None], NEG_INF)
    terms, rest = [], c
    for _ in range(POS_TERMS):
        terms.append(rest.astype(BF16))
        rest = rest - terms[-1].astype(F32)
    slope_rows = jnp.stack(terms + terms + [jnp.zeros_like(terms[0])] * (vd - 2 * POS_TERMS), axis=1)
    slope_tile = jnp.broadcast_to(slope_rows[:, :, None], (A_HEADS, vd, t))
    chunk_max = lambda tile: jnp.max(tile.reshape(A_HEADS, n_chunks, CHUNK, t), axis=2)
    step = jnp.broadcast_to((slopes * (LOG2E * t))[:, None, None], (A_HEADS, 1, t))
    vec = lambda v: v.reshape(1, HEAD_DIM).astype(F32)
    small = lambda shape: pl.BlockSpec(shape, lambda b, h, i: (0,) * len(shape))
    per_head = lambda shape: pl.BlockSpec(shape, lambda b, h, i: (h,) + (0,) * (len(shape) - 1))
    return pl.pallas_call(
        functools.partial(_diff_attn_kernel, lam_init=lam_init),
        grid=(bsz, A_HEADS, nb),
        in_specs=[pl.BlockSpec((1, 1, 1, 2, vd, t), lambda b, h, i: (b, h, i, 0, 0, 0)),
                  pl.BlockSpec((1, seq, 2 * vd), lambda b, h, i: (b, 0, h)),
                  pl.BlockSpec((1, 1, nb, vd + BF16_ROWS, t), lambda b, h, i: (b, h, 0, 0, 0)),
                  per_head((1, vd, t)), per_head((1, t, t)),
                  per_head((1, n_chunks, t)), per_head((1, 1, t)),
                  small((1, HEAD_DIM)), small((1, HEAD_DIM)), small((1, HEAD_DIM)), small((1, HEAD_DIM)),
                  small((1, vd))],
        out_specs=pl.BlockSpec((1, t, vd), lambda b, h, i: (b, i, h)),
        out_shape=jax.ShapeDtypeStruct((bsz, seq, A_W), BF16),
        scratch_shapes=[pltpu.VMEM((2, 2 * vd, t), BF16),
                        pltpu.VMEM((2, t, t), F32), pltpu.VMEM((2, t, t), F32),
                        pltpu.VMEM((2, n_chunks, 8, t), F32), pltpu.VMEM((2, n_chunks, 8, t), F32),
                        pltpu.VMEM((2, 1, t), F32),
                        pltpu.VMEM((2, vd + BF16_ROWS, t), F32),
                        pltpu.VMEM((2, 1, t), F32)],
        compiler_params=_params(("parallel", "parallel", "arbitrary")),
        name="diff_attn",
    )(qt, keys, vt, slope_tile, diag, chunk_max(diag), step,
      vec(lam_q1), vec(lam_k1), vec(lam_q2), vec(lam_k2), subln_g.reshape(1, vd).astype(F32))


def _band_attn_kernel(q_ref, k_ref, vt_ref, diag_vals_ref, band_ref, o_ref, bias_ref, sa_ref, sb_ref):
    t = BAND_BLOCK
    nb = k_ref.shape[1] // t
    n_win = BAND_WINDOW // t
    vd = 2 * HEAD_DIM
    row = lax.broadcasted_iota(jnp.int32, (vd, t), 0)
    n_vals = diag_vals_ref.shape[-1]

    for hh in range(2):
        vals = jnp.broadcast_to(diag_vals_ref[hh], (8, n_vals))
        for c0 in range(0, BAND_WINDOW, 8):
            rolled = pltpu.roll(vals, (c0 - (BAND_WINDOW - 1)) % n_vals, axis=1, stride=1, stride_axis=0)
            bias_ref[hh, c0:c0 + 8, :] = jnp.where(band_ref[c0:c0 + 8, :] > 0.0, rolled[:, 0:t], NEG_INF)

    def rows(n):
        return pl.ds(pl.multiple_of(n * t, t), t)

    def key_block(n, w):
        return jnp.maximum(n - (n_win - 1) + w, 0)

    def scores(n, s_ref):
        qt = q_ref[0, rows(n), :].astype(F32).T * (HEAD_DIM ** -0.5 * LOG2E)
        for hh in range(2):
            in_head = (row < HEAD_DIM) if hh == 0 else (row >= HEAD_DIM)
            qh = jnp.where(in_head, qt, 0.0).astype(BF16)
            for w in range(n_win):
                s_ref[hh, w * t:(w + 1) * t] = jnp.dot(k_ref[0, rows(key_block(n, w)), :], qh,
                                                       preferred_element_type=F32)

    def softmax_pv(n, s_ref, mask_keys):
        outs = []
        for hh in range(2):
            st = s_ref[hh] + bias_ref[hh]
            if mask_keys:
                key_pos = (n - (n_win - 1)) * t + lax.broadcasted_iota(jnp.int32, (BAND_WINDOW, 1), 0)
                st = jnp.where(key_pos >= 0, st, NEG_INF)
            p = jnp.exp2(st - jnp.max(st, axis=0, keepdims=True)).astype(BF16)
            ot = jnp.dot(vt_ref[0, 0, key_block(n, 0)], p[0:t], preferred_element_type=F32)
            for w in range(1, n_win):
                ot = ot + jnp.dot(vt_ref[0, 0, key_block(n, w)], p[w * t:(w + 1) * t],
                                  preferred_element_type=F32)
            outs.append(ot[0:vd] / ot[vd:vd + 1])
        o_ref[0, rows(n), :] = jnp.where(row < HEAD_DIM, outs[0], outs[1]).T.astype(o_ref.dtype)

    first = n_win - 1
    assert first % 2 == 0
    scores(0, sa_ref)
    for n in range(first):
        cur, nxt = (sa_ref, sb_ref) if n % 2 == 0 else (sb_ref, sa_ref)
        scores(n + 1, nxt)
        softmax_pv(n, cur, True)

    def pair(jj, carry):
        n0 = first + 2 * jj
        scores(n0 + 1, sb_ref)
        softmax_pv(n0, sa_ref, False)
        scores(n0 + 2, sa_ref)
        softmax_pv(n0 + 1, sb_ref, False)
        return carry

    n_pairs = (nb - first - 2) // 2
    lax.fori_loop(0, n_pairs, pair, 0, unroll=BAND_UNROLL)
    for n in range(first + 2 * n_pairs, nb):
        if n + 1 < nb:
            scores(n + 1, sb_ref)
        softmax_pv(n, sa_ref, False)
        sa_ref, sb_ref = sb_ref, sa_ref


def _band_attention(hproj, vt, rel_bias):
    bsz, seq, _ = hproj.shape
    t = BAND_BLOCK
    vd = 2 * HEAD_DIM
    c = jnp.arange(BAND_WINDOW)[:, None]
    r = jnp.arange(t)[None, :]
    in_band = ((c // CHUNK) >= (r // CHUNK)) & ((c // CHUNK) <= (r // CHUNK) + B_LEFT_CHUNKS)
    rb = rel_bias.astype(F32) * LOG2E
    n_lo = BAND_WINDOW - 1 - B_LEFT_CHUNKS * CHUNK - REL_CLIP
    n_hi = t + BAND_WINDOW - n_lo - rb.shape[1]
    diag_vals = jnp.concatenate([jnp.broadcast_to(rb[:, :1], (B_HEADS, n_lo)), rb,
                                 jnp.broadcast_to(rb[:, -1:], (B_HEADS, n_hi))], axis=1)[:, None, :]
    qcol, kcol = (2 * A_W) // vd, (2 * A_W + B_W) // vd
    seq_spec = lambda col: pl.BlockSpec((1, seq, vd), lambda b, hp: (b, 0, col + hp))
    return pl.pallas_call(
        _band_attn_kernel,
        grid=(bsz, B_HEADS // 2),
        in_specs=[seq_spec(qcol), seq_spec(kcol),
                  pl.BlockSpec((1, 1, seq // t, vd + BF16_ROWS, t), lambda b, hp: (b, hp, 0, 0, 0)),
                  pl.BlockSpec((2, 1, t + BAND_WINDOW), lambda b, hp: (hp, 0, 0)),
                  pl.BlockSpec((BAND_WINDOW, t), lambda b, hp: (0, 0))],
        out_specs=pl.BlockSpec((1, seq, vd), lambda b, hp: (b, 0, hp)),
        out_shape=jax.ShapeDtypeStruct((bsz, seq, B_W), BF16),
        scratch_shapes=[pltpu.VMEM((2, BAND_WINDOW, t), F32),
                        pltpu.VMEM((2, BAND_WINDOW, t), F32), pltpu.VMEM((2, BAND_WINDOW, t), F32)],
        compiler_params=_params(("parallel", "parallel")),
        name="band_attn",
    )(hproj, hproj, vt, diag_vals, in_band.astype(F32))


def _retention_kernel(q_ref, k_ref, v_ref, gate_ref, dmask_ref, qdec_ref, kdec_ref, cdec_ref,
                      g_ref, beta_ref, o_ref, state_ref):
    ln = RET_BLOCK
    nb = q_ref.shape[1] // ln
    vd = 2 * HEAD_DIM
    scale = HEAD_DIM ** -0.5
    lane = lax.broadcasted_iota(jnp.int32, (ln, vd), 1)
    state_ref[...] = jnp.zeros(state_ref.shape, F32)

    def block(n, carry):
        rows = pl.ds(pl.multiple_of(n * ln, ln), ln)
        q = q_ref[0, rows, :]
        k = k_ref[0, rows, :]
        for hh in range(2):
            in_head = (lane < HEAD_DIM) if hh == 0 else (lane >= HEAD_DIM)
            qh = jnp.where(in_head, q, jnp.zeros_like(q))
            kh = jnp.where(in_head, k.astype(F32) * scale, 0.0)
            sl = slice(hh * vd, (hh + 1) * vd)
            v = v_ref[0, rows, sl]
            scores = lax.dot_general(qh, kh.astype(BF16), (((1,), (1,)), ((), ())),
                                     preferred_element_type=F32) * dmask_ref[hh]
            inner = jnp.dot(scores.astype(BF16), v, preferred_element_type=F32)
            state = state_ref[hh]
            cross = jnp.dot(qh, state.astype(BF16), preferred_element_type=F32) * qdec_ref[hh]
            kdt = (kh * kdec_ref[hh]).T.astype(BF16)
            kv = jnp.dot(kdt, v, preferred_element_type=F32)
            state_ref[hh] = cdec_ref[hh] * state + kv
            o = inner + cross
            mu = jnp.mean(o, axis=-1, keepdims=True)
            d = o - mu
            var = jnp.mean(d * d, axis=-1, keepdims=True)
            y = d * lax.rsqrt(var + LN_EPS) * g_ref[:, sl] + beta_ref[:, sl]
            gate = gate_ref[0, rows, sl]
            o_ref[0, rows, sl] = (y * (gate * jax.nn.sigmoid(gate))).astype(o_ref.dtype)
        return carry

    lax.fori_loop(0, nb, block, 0, unroll=8)


def _retention(qkv, gate, gn_g, gn_b):
    bsz, seq, _ = qkv.shape
    ln = RET_BLOCK
    vd = 2 * HEAD_DIM
    log_gamma = jnp.log(1.0 - 2.0 ** (-5.0 - jnp.arange(C_HEADS, dtype=F32)))
    idx = jnp.arange(ln, dtype=F32)
    diff = idx[:, None] - idx[None, :]
    dmask = jnp.where(diff >= 0, jnp.exp(log_gamma[:, None, None] * jnp.maximum(diff, 0.0)), 0.0)
    qdec = jnp.broadcast_to(jnp.exp(log_gamma[:, None] * (idx + 1.0))[:, :, None], (C_HEADS, ln, vd))
    kdec = jnp.broadcast_to(jnp.exp(log_gamma[:, None] * (ln - 1.0 - idx))[:, :, None], (C_HEADS, ln, vd))
    cdec = jnp.broadcast_to(jnp.exp(log_gamma * ln)[:, None, None], (C_HEADS, 1, vd))
    pair = lambda shape: pl.BlockSpec(shape, lambda b, hp: (hp,) + (0,) * (len(shape) - 1))
    return pl.pallas_call(
        _retention_kernel,
        grid=(bsz, C_HEADS // 2),
        in_specs=[pl.BlockSpec((1, seq, vd), lambda b, hp: (b, 0, hp)),
                  pl.BlockSpec((1, seq, vd), lambda b, hp: (b, 0, C_QK_W // vd + hp)),
                  pl.BlockSpec((1, seq, 2 * vd), lambda b, hp: (b, 0, (2 * C_QK_W) // (2 * vd) + hp)),
                  pl.BlockSpec((1, seq, 2 * vd), lambda b, hp: (b, 0, hp)),
                  pair((2, ln, ln)), pair((2, ln, vd)), pair((2, ln, vd)), pair((2, 1, vd)),
                  pl.BlockSpec((1, 2 * vd), lambda b, hp: (0, hp)),
                  pl.BlockSpec((1, 2 * vd), lambda b, hp: (0, hp))],
        out_specs=pl.BlockSpec((1, seq, 2 * vd), lambda b, hp: (b, 0, hp)),
        out_shape=jax.ShapeDtypeStruct((bsz, seq, C_W), BF16),
        scratch_shapes=[pltpu.VMEM((2, vd, vd), F32)],
        compiler_params=_params(("parallel", "parallel")),
        name="retention",
    )(qkv, qkv, qkv, gate, dmask, qdec, kdec, cdec, gn_g.reshape(1, C_W), gn_b.reshape(1, C_W))


def kernel(x, w_in_even, lam_q1, lam_k1, lam_q2, lam_k2, diff_subln_g, rel_bias, w_out_even,
           w_in_odd, ret_gn_g, ret_gn_b, sgu_ln_g, sgu_ln_b, sgu_w, sgu_b, w_out_odd,
           ln_mix_g, ln_mix_b, ffn_w_gate, ffn_w_up, ffn_w_down, ln_ffn_g, ln_ffn_b):
    bsz, seq, d = x.shape
    t = bsz * seq
    x2d = x.reshape(t, d)
    for l in range(DEPTH):
        j = l // 2
        if l % 2 == 0:
            hproj, qta, vta, vtb = _proj_even(x2d, w_in_even[j].astype(BF16), bsz, seq)
            hproj = hproj.reshape(bsz, seq, -1)
            lam_init = 0.8 - 0.6 * math.exp(-0.3 * l)
            mix_a = _diff_attention(hproj, qta, vta, lam_q1[j], lam_k1[j], lam_q2[j], lam_k2[j],
                                    diff_subln_g[j], lam_init)
            mix_b = _band_attention(hproj, vtb, rel_bias[j])
            w_out = w_out_even[j]
        else:
            qkv, gate, mix_b = _proj_odd(x2d, w_in_odd[j].astype(BF16), sgu_w[j], sgu_b[j],
                                         sgu_ln_g[j], sgu_ln_b[j])
            mix_a = _retention(qkv.reshape(bsz, seq, -1), gate.reshape(bsz, seq, -1),
                               ret_gn_g[j], ret_gn_b[j])
            w_out = w_out_odd[j]
        x2d = _mix_ffn(mix_a.reshape(t, -1), mix_b.reshape(t, -1), x2d, w_out.astype(BF16),
                       ln_mix_g[l], ln_mix_b[l], ffn_w_gate[l].astype(BF16), ffn_w_up[l].astype(BF16),
                       ffn_w_down[l].astype(BF16), ln_ffn_g[l], ln_ffn_b[l])
    return x2d.reshape(bsz, seq, d)
```

```python
import functools
import math

import jax
import jax.numpy as jnp
from jax import lax
from jax.experimental import pallas as pl
from jax.experimental.pallas import tpu as pltpu

F32 = jnp.float32
BF16 = jnp.bfloat16

D_MODEL = 1024
HEAD_DIM = 64
CHUNK = 64
A_HEADS = 4
A_W = 512
B_HEADS = 8
B_W = 512
B_LEFT_CHUNKS = 8
REL_CLIP = 128
C_HEADS = 4
C_QK_W = 256
C_W = 512
D_CHUNK = 128
D_GROUPS = 4
D_WIDTH = 512
DEPTH = 2
ALPHA = (2.0 * DEPTH) ** 0.25
LN_EPS = 1e-5
NEG_INF = -1e30
LOG2E = 1.4426950408889634
SKIP_LOG2 = 160.0
NORM_SLACK = 1.0 + 2.0 ** -6

VMEM_LIMIT_V7X = 56 * 1024 * 1024
BF16_ROWS = 16
POS_SPLIT = 256
POS_TERMS = 3

ROW_BLOCK = 512
ATT_BLOCK = 512
BAND_BLOCK = 256
BAND_WINDOW = BAND_BLOCK + B_LEFT_CHUNKS * CHUNK
BAND_UNROLL = 7
RET_BLOCK = 256


def _params(semantics):
    return pltpu.CompilerParams(dimension_semantics=semantics, vmem_limit_bytes=VMEM_LIMIT_V7X)


def _const_spec(shape):
    nd = len(shape)
    return pl.BlockSpec(shape, lambda *_: (0,) * nd, pipeline_mode=pl.Buffered(1))


def _layer_norm(y, g, b):
    mu = jnp.mean(y, axis=-1, keepdims=True)
    d = y - mu
    var = jnp.mean(d * d, axis=-1, keepdims=True)
    return d * lax.rsqrt(var + LN_EPS) * g + b


def _proj_odd_kernel(x_ref, w_ref, ws_ref, bt_ref, g_ref, beta_ref, qkv_ref, gate_ref, od_ref):
    x = x_ref[...].astype(BF16)
    dot = lambda c: jnp.dot(x, w_ref[:, c:c + 512], preferred_element_type=F32)
    off_d = 2 * C_QK_W + 2 * C_W
    u = dot(off_d)
    v = dot(off_d + D_WIDTH)
    qkv_ref[:, 0:512] = dot(0).astype(BF16)
    qkv_ref[:, 512:1024] = dot(512).astype(BF16)
    gate_ref[...] = dot(2 * C_QK_W + C_W)
    u = jax.nn.gelu(u)
    v = _layer_norm(jax.nn.gelu(v), g_ref[...], beta_ref[...])
    gw = D_WIDTH // D_GROUPS
    r = lax.broadcasted_iota(jnp.int32, (D_CHUNK, D_CHUNK), 0)
    c = lax.broadcasted_iota(jnp.int32, (D_CHUNK, D_CHUNK), 1)
    for g in range(D_GROUPS):
        w = jnp.where(r >= c, ws_ref[g], 0.0).astype(BF16)
        bias = bt_ref[:, g:g + 1]
        cols = slice(g * gw, (g + 1) * gw)
        vg = v[:, cols].astype(BF16)
        for n in range(ROW_BLOCK // D_CHUNK):
            rows = slice(n * D_CHUNK, (n + 1) * D_CHUNK)
            mixed = jnp.dot(w, vg[rows], preferred_element_type=F32) + bias
            od_ref[rows, cols] = (u[rows, cols] * mixed).astype(od_ref.dtype)


def _proj_odd(x2d, w_bf16, w_s, b_s, ln_g, ln_b):
    t, k = x2d.shape
    n = w_bf16.shape[1]
    assert ROW_BLOCK % D_CHUNK == 0 and n == 2 * C_QK_W + 2 * C_W + 2 * D_WIDTH
    row_spec = lambda wd: pl.BlockSpec((ROW_BLOCK, wd), lambda i: (i, 0))
    return pl.pallas_call(
        _proj_odd_kernel,
        grid=(t // ROW_BLOCK,),
        in_specs=[row_spec(k), _const_spec((k, n)), _const_spec((D_GROUPS, D_CHUNK, D_CHUNK)),
                  _const_spec((D_CHUNK, D_GROUPS)), _const_spec((1, D_WIDTH)), _const_spec((1, D_WIDTH))],
        out_specs=[row_spec(2 * C_QK_W + C_W), row_spec(C_W), row_spec(D_WIDTH)],
        out_shape=[jax.ShapeDtypeStruct((t, 2 * C_QK_W + C_W), BF16),
                   jax.ShapeDtypeStruct((t, C_W), F32),
                   jax.ShapeDtypeStruct((t, D_WIDTH), BF16)],
        compiler_params=_params(("parallel",)),
        name="in_proj_odd",
    )(x2d, w_bf16, w_s.astype(F32), b_s.T.astype(F32), ln_g.reshape(1, D_WIDTH), ln_b.reshape(1, D_WIDTH))


def _transposed_value_tiles(vals, block):
    ones_tile = (lax.broadcasted_iota(jnp.int32, (BF16_ROWS, block), 0) == 0).astype(BF16)
    return [jnp.concatenate([vals[r:r + block].T.astype(BF16), ones_tile], axis=0)
            for r in range(0, vals.shape[0], block)]


def _transposed_query_tiles(q, block):
    tiles = []
    for r in range(0, q.shape[0], block):
        qt = q[r:r + block].T * (HEAD_DIM ** -0.5 * LOG2E)
        row = lax.broadcasted_iota(jnp.int32, qt.shape, 0)
        tiles.append((jnp.where(row < HEAD_DIM, qt, 0.0).astype(BF16),
                      jnp.where(row >= HEAD_DIM, qt, 0.0).astype(BF16)))
    return tiles


def _proj_even_kernel(x_ref, w_ref, h_ref, qta_ref, vta_ref, vtb_ref):
    x = x_ref[...].astype(BF16)
    vd = 2 * HEAD_DIM
    dot = lambda c: jnp.dot(x, w_ref[:, c:c + 512], preferred_element_type=F32)
    vb = dot(3 * A_W + 2 * B_W)
    for hp in range(B_HEADS // 2):
        tiles = _transposed_value_tiles(vb[:, hp * vd:(hp + 1) * vd], BAND_BLOCK)
        for n, tile in enumerate(tiles):
            vtb_ref[0, hp, n] = tile
    va = dot(2 * A_W)
    for h in range(A_HEADS):
        vta_ref[0, h, 0] = _transposed_value_tiles(va[:, h * vd:(h + 1) * vd], ATT_BLOCK)[0]
    qa = dot(0)
    for h in range(A_HEADS):
        tile0, tile1 = _transposed_query_tiles(qa[:, h * vd:(h + 1) * vd], ATT_BLOCK)[0]
        qta_ref[0, h, 0, 0] = tile0
        qta_ref[0, h, 0, 1] = tile1
    ka = dot(A_W)
    r = lax.broadcasted_iota(jnp.int32, (ROW_BLOCK, vd), 0)
    col = lax.broadcasted_iota(jnp.int32, (ROW_BLOCK, vd), 1)
    pos = jnp.where(col < POS_TERMS, r & (POS_SPLIT - 1),
                    jnp.where(col < 2 * POS_TERMS, r & ~(POS_SPLIT - 1), 0)).astype(F32).astype(BF16)
    for h in range(A_HEADS):
        h_ref[:, 2 * vd * h:2 * vd * h + vd] = ka[:, h * vd:(h + 1) * vd].astype(BF16)
        h_ref[:, 2 * vd * h + vd:2 * vd * (h + 1)] = pos
    h_ref[:, 2 * A_W:2 * A_W + B_W] = dot(3 * A_W).astype(BF16)
    h_ref[:, 2 * A_W + B_W:2 * A_W + 2 * B_W] = dot(3 * A_W + B_W).astype(BF16)


def _proj_even(x2d, w_bf16, bsz, seq):
    t, k = x2d.shape
    n = w_bf16.shape[1]
    assert ROW_BLOCK == ATT_BLOCK and seq % ROW_BLOCK == 0
    per_seq = seq // ROW_BLOCK
    vd = 2 * HEAD_DIM
    vrows = vd + BF16_ROWS
    nba, nbb, per_step_b = seq // ATT_BLOCK, seq // BAND_BLOCK, ROW_BLOCK // BAND_BLOCK
    seq_block = lambda i: (i // per_seq, 0, i % per_seq, 0, 0)
    seq_block6 = lambda i: (i // per_seq, 0, i % per_seq, 0, 0, 0)
    return pl.pallas_call(
        _proj_even_kernel,
        grid=(t // ROW_BLOCK,),
        in_specs=[pl.BlockSpec((ROW_BLOCK, k), lambda i: (i, 0)), _const_spec((k, n))],
        out_specs=[pl.BlockSpec((ROW_BLOCK, 2 * A_W + 2 * B_W), lambda i: (i, 0)),
                   pl.BlockSpec((1, A_HEADS, 1, 2, vd, ATT_BLOCK), seq_block6),
                   pl.BlockSpec((1, A_HEADS, 1, vrows, ATT_BLOCK), seq_block),
                   pl.BlockSpec((1, B_HEADS // 2, per_step_b, vrows, BAND_BLOCK), seq_block)],
        out_shape=[jax.ShapeDtypeStruct((t, 2 * A_W + 2 * B_W), BF16),
                   jax.ShapeDtypeStruct((bsz, A_HEADS, nba, 2, vd, ATT_BLOCK), BF16),
                   jax.ShapeDtypeStruct((bsz, A_HEADS, nba, vrows, ATT_BLOCK), BF16),
                   jax.ShapeDtypeStruct((bsz, B_HEADS // 2, nbb, vrows, BAND_BLOCK), BF16)],
        compiler_params=_params(("parallel",)),
        name="in_proj_even",
    )(x2d, w_bf16)


def _mix_ffn_kernel(a_ref, b_ref, x_ref, wo_ref, g1_ref, b1_ref, wg_ref, wu_ref, wd_ref,
                    g2_ref, b2_ref, o_ref):
    ka = a_ref.shape[-1]
    half = x_ref.shape[0] // 2
    halves = (slice(0, half), slice(half, 2 * half))
    mix = [jnp.dot(a_ref[rows, :], wo_ref[0:ka, :], preferred_element_type=F32)
           + jnp.dot(b_ref[rows, :], wo_ref[ka:, :], preferred_element_type=F32) for rows in halves]
    x1 = [_layer_norm(ALPHA * x_ref[rows, :] + m, g1_ref[...], b1_ref[...]) for rows, m in zip(halves, mix)]
    xb = [v.astype(BF16) for v in x1]
    gate = [jnp.dot(v, wg_ref[...], preferred_element_type=F32) for v in xb]
    up = [jnp.dot(v, wu_ref[...], preferred_element_type=F32) for v in xb]
    h = [(g * jax.nn.sigmoid(g) * u).astype(BF16) for g, u in zip(gate, up)]
    y = [ALPHA * v + jnp.dot(hh, wd_ref[...], preferred_element_type=F32) for v, hh in zip(x1, h)]
    for rows, v in zip(halves, y):
        o_ref[rows, :] = _layer_norm(v, g2_ref[...], b2_ref[...])


def _mix_ffn(a, b, x2d, wo, g1, b1, wg, wu, wd, g2, b2):
    t, d = x2d.shape
    ka, kb = a.shape[1], b.shape[1]
    f = wg.shape[1]
    vec = lambda v: v.reshape(1, d)
    return pl.pallas_call(
        _mix_ffn_kernel,
        grid=(t // ROW_BLOCK,),
        in_specs=[pl.BlockSpec((ROW_BLOCK, ka), lambda i: (i, 0)),
                  pl.BlockSpec((ROW_BLOCK, kb), lambda i: (i, 0)),
                  pl.BlockSpec((ROW_BLOCK, d), lambda i: (i, 0)),
                  _const_spec((ka + kb, d)), _const_spec((1, d)), _const_spec((1, d)),
                  _const_spec((d, f)), _const_spec((d, f)), _const_spec((f, d)),
                  _const_spec((1, d)), _const_spec((1, d))],
        out_specs=pl.BlockSpec((ROW_BLOCK, d), lambda i: (i, 0)),
        out_shape=jax.ShapeDtypeStruct((t, d), F32),
        compiler_params=_params(("parallel",)),
        name="mix_ffn",
    )(a, b, x2d, wo, vec(g1), vec(b1), wg, wu, wd, vec(g2), vec(b2))


def _diff_attn_kernel(qt_ref, k_ref, vt_ref, slope_ref, diag_ref, diag_max_ref, step_ref,
                      lq1_ref, lk1_ref, lq2_ref, lk2_ref, g_ref, o_ref,
                      qa_ref, sa_ref, sb_ref, ra_ref, rb_ref, m_ref, acc_ref, knorm_ref, *, lam_init):
    t = ATT_BLOCK
    n_chunks = t // CHUNK
    i = pl.program_id(2)
    qt_ref = qt_ref.at[0, 0, 0]
    vd = 2 * HEAD_DIM
    for mp in range(2):
        qa_ref[mp, 0:vd] = qt_ref[mp]
        qa_ref[mp, vd:2 * vd] = slope_ref[0]

    @pl.when(i == 0)
    def _():
        lane = lax.broadcasted_iota(jnp.int32, (vd, vd), 0)
        col = lax.broadcasted_iota(jnp.int32, (vd, vd), 1)
        pick = ((lane // HEAD_DIM) == col).astype(BF16)

        def block_norms(j, carry):
            kb = k_ref[0, pl.ds(pl.multiple_of(j * t, t), t), 0:vd].astype(F32)
            sums = jnp.dot((kb * kb).astype(BF16), pick, preferred_element_type=F32)
            return jnp.maximum(carry, jnp.max(sums, axis=0, keepdims=True))

        n = lax.fori_loop(0, k_ref.shape[1] // t, block_norms, jnp.zeros((1, vd), F32), unroll=True)
        knorm_ref[0] = jnp.broadcast_to(jnp.sqrt(n[:, 0:1]), (1, t))
        knorm_ref[1] = jnp.broadcast_to(jnp.sqrt(n[:, 1:2]), (1, t))

    half = t // 2
    lo, hi = slice(0, half), slice(half, t)

    def chunk_maxima(s):
        return jnp.max(s.reshape(s.shape[0] // CHUNK, CHUNK // 8, 8, s.shape[1]), axis=1)

    def diagonal_scores():
        start = pl.multiple_of(i * t, t)
        for mp in range(2):
            top = jnp.dot(k_ref[0, pl.ds(start, half), 0:vd], qt_ref[mp], preferred_element_type=F32)
            bot = jnp.dot(k_ref[0, pl.ds(pl.multiple_of(start + half, half), half), 0:vd],
                          qt_ref[mp, :, hi], preferred_element_type=F32)
            sa_ref[mp, lo] = top
            sa_ref[mp, hi, hi] = bot
            ra_ref[mp, 0:n_chunks // 2] = chunk_maxima(top)
            ra_ref[mp, n_chunks // 2:, :, hi] = chunk_maxima(bot)
            ra_ref[mp, n_chunks // 2:, :, lo] = jnp.zeros((n_chunks // 2, 8, half), F32)

    def diagonal_softmax_pv():
        vt_blk = vt_ref[0, 0, i]
        for mp in range(2):
            bound = jnp.max(ra_ref[mp] + diag_max_ref[0][:, None, :], axis=0)
            m_new = jnp.max(bound, axis=0, keepdims=True)
            p_top = jnp.exp2(sa_ref[mp, lo] + diag_ref[0, lo] - m_new).astype(BF16)
            p_bot = jnp.exp2(sa_ref[mp, hi, hi] + diag_ref[0, hi, hi] - m_new[:, hi]).astype(BF16)
            acc = jnp.dot(vt_blk[:, lo], p_top, preferred_element_type=F32)
            acc_ref[mp, :, lo] = acc[:, lo]
            acc_ref[mp, :, hi] = acc[:, hi] + jnp.dot(vt_blk[:, hi], p_bot, preferred_element_type=F32)
            m_ref[mp] = m_new + step_ref[0]

    def scores(j, s_ref, r_ref):
        start = pl.multiple_of(j * t, t)
        for mp in range(2):
            s = jnp.dot(k_ref[0, pl.ds(start, t), :], qa_ref[mp], preferred_element_type=F32)
            s_ref[mp] = s
            r_ref[mp] = chunk_maxima(s)

    def softmax_pv(j, s_ref, r_ref):
        vt_blk = vt_ref[0, 0, j]
        step = step_ref[0]
        for mp in range(2):
            m_prev = m_ref[mp]
            bound = jnp.max(jnp.max(r_ref[mp], axis=0), axis=0, keepdims=True)
            m_new = jnp.maximum(m_prev, bound)
            alpha = jnp.exp2(m_prev - m_new)
            p = jnp.exp2(s_ref[mp] - m_new)
            acc_ref[mp] = alpha * acc_ref[mp] + jnp.dot(vt_blk, p.astype(BF16),
                                                        preferred_element_type=F32)
            m_ref[mp] = m_new + step

    def prev_block(r):
        return jnp.maximum(i - r, 0)

    diagonal_scores()
    scores(prev_block(1), sb_ref, rb_ref)
    diagonal_softmax_pv()

    tile_top = step_ref[0] * ((t - 1) / t)
    inv_step = 1.0 / step_ref[0]
    reach = jnp.zeros((1, t), F32)
    for mp in range(2):
        qb = qt_ref[mp].astype(F32)
        qnorm = jnp.sqrt(jnp.sum(qb * qb, axis=0, keepdims=True))
        gap = qnorm * knorm_ref[mp] * NORM_SLACK + tile_top - m_ref[mp]
        reach = jnp.maximum(reach, (gap + SKIP_LOG2) * inv_step + 1.0)
    reach = jnp.floor(jnp.minimum(reach, float(k_ref.shape[1] // t))).astype(jnp.int32)
    n_keep = jnp.minimum(jnp.max(reach), i)

    def past_pair(r0):
        scores(prev_block(r0 + 1), sa_ref, ra_ref)
        softmax_pv(i - r0, sb_ref, rb_ref)
        scores(prev_block(r0 + 2), sb_ref, rb_ref)
        softmax_pv(i - r0 - 1, sa_ref, ra_ref)

    def past_oct(jj, carry):
        for pr in range(4):
            past_pair(8 * jj + 2 * pr + 1)
        return carry

    n_octs = lax.shift_right_logical(n_keep, 3)
    lax.fori_loop(0, n_octs, past_oct, 0)
    done = 8 * n_octs

    @pl.when((n_keep & 4) == 4)
    def _():
        past_pair(done + 1)
        past_pair(done + 3)

    @pl.when((n_keep & 2) == 2)
    def _():
        past_pair(done + (n_keep & 4) + 1)

    @pl.when((n_keep & 1) == 1)
    def _():
        softmax_pv(i - n_keep, sb_ref, rb_ref)

    lam = (jnp.exp(jnp.sum(lq1_ref[...] * lk1_ref[...], axis=-1, keepdims=True))
           - jnp.exp(jnp.sum(lq2_ref[...] * lk2_ref[...], axis=-1, keepdims=True)) + lam_init)
    ot = (acc_ref[0, 0:vd] / acc_ref[0, vd:vd + 1]
          - lam * (acc_ref[1, 0:vd] / acc_ref[1, vd:vd + 1]))
    ms = jnp.mean(ot * ot, axis=0, keepdims=True)
    y = (ot * lax.rsqrt(ms + LN_EPS)).T * g_ref[...] * (1.0 - lam_init)
    o_ref[0] = y.astype(o_ref.dtype)


def _diff_attention(keys, qt, vt, lam_q1, lam_k1, lam_q2, lam_k2, subln_g, lam_init):
    bsz, seq, _ = keys.shape
    t = ATT_BLOCK
    nb = seq // t
    n_chunks = t // CHUNK
    vd = 2 * HEAD_DIM
    slopes = 2.0 ** (-8.0 * jnp.arange(1, A_HEADS + 1, dtype=F32) / A_HEADS)
    kk = jnp.arange(t)[:, None]
    qq = jnp.arange(t)[None, :]
    rel = (qq - kk).astype(F32)
    c = slopes * LOG2E
    allowed = (kk // CHUNK) <= (qq // CHUNK)
    diag = jnp.where(allowed[None], c[:, None, None] * (qq.astype(F32) - jnp.abs(rel))[None], NEG_INF)
    terms, rest = [], c
    for _ in range(POS_TERMS):
        terms.append(rest.astype(BF16))
        rest = rest - terms[-1].astype(F32)
    slope_rows = jnp.stack(terms + terms + [jnp.zeros_like(terms[0])] * (vd - 2 * POS_TERMS), axis=1)
    slope_tile = jnp.broadcast_to(slope_rows[:, :, None], (A_HEADS, vd, t))
    chunk_max = lambda tile: jnp.max(tile.reshape(A_HEADS, n_chunks, CHUNK, t), axis=2)
    step = jnp.broadcast_to((slopes * (LOG2E * t))[:, None, None], (A_HEADS, 1, t))
    vec = lambda v: v.reshape(1, HEAD_DIM).astype(F32)
    small = lambda shape: pl.BlockSpec(shape, lambda b, h, i: (0,) * len(shape))
    per_head = lambda shape: pl.BlockSpec(shape, lambda b, h, i: (h,) + (0,) * (len(shape) - 1))
    return pl.pallas_call(
        functools.partial(_diff_attn_kernel, lam_init=lam_init),
        grid=(bsz, A_HEADS, nb),
        in_specs=[pl.BlockSpec((1, 1, 1, 2, vd, t), lambda b, h, i: (b, h, i, 0, 0, 0)),
                  pl.BlockSpec((1, seq, 2 * vd), lambda b, h, i: (b, 0, h)),
                  pl.BlockSpec((1, 1, nb, vd + BF16_ROWS, t), lambda b, h, i: (b, h, 0, 0, 0)),
                  per_head((1, vd, t)), per_head((1, t, t)),
                  per_head((1, n_chunks, t)), per_head((1, 1, t)),
                  small((1, HEAD_DIM)), small((1, HEAD_DIM)), small((1, HEAD_DIM)), small((1, HEAD_DIM)),
                  small((1, vd))],
        out_specs=pl.BlockSpec((1, t, vd), lambda b, h, i: (b, i, h)),
        out_shape=jax.ShapeDtypeStruct((bsz, seq, A_W), BF16),
        scratch_shapes=[pltpu.VMEM((2, 2 * vd, t), BF16),
                        pltpu.VMEM((2, t, t), F32), pltpu.VMEM((2, t, t), F32),
                        pltpu.VMEM((2, n_chunks, 8, t), F32), pltpu.VMEM((2, n_chunks, 8, t), F32),
                        pltpu.VMEM((2, 1, t), F32),
                        pltpu.VMEM((2, vd + BF16_ROWS, t), F32),
                        pltpu.VMEM((2, 1, t), F32)],
        compiler_params=_params(("parallel", "parallel", "arbitrary")),
        name="diff_attn",
    )(qt, keys, vt, slope_tile, diag, chunk_max(diag), step,
      vec(lam_q1), vec(lam_k1), vec(lam_q2), vec(lam_k2), subln_g.reshape(1, vd).astype(F32))


def _band_attn_kernel(q_ref, k_ref, vt_ref, diag_vals_ref, band_ref, o_ref, bias_ref, sa_ref, sb_ref):
    t = BAND_BLOCK
    nb = k_ref.shape[1] // t
    n_win = BAND_WINDOW // t
    vd = 2 * HEAD_DIM
    row = lax.broadcasted_iota(jnp.int32, (vd, t), 0)
    n_vals = diag_vals_ref.shape[-1]

    for hh in range(2):
        vals = jnp.broadcast_to(diag_vals_ref[hh], (8, n_vals))
        for c0 in range(0, BAND_WINDOW, 8):
            rolled = pltpu.roll(vals, (c0 - (BAND_WINDOW - 1)) % n_vals, axis=1, stride=1, stride_axis=0)
            bias_ref[hh, c0:c0 + 8, :] = jnp.where(band_ref[c0:c0 + 8, :] > 0.0, rolled[:, 0:t], NEG_INF)

    def rows(n):
        return pl.ds(pl.multiple_of(n * t, t), t)

    def key_block(n, w):
        return jnp.maximum(n - (n_win - 1) + w, 0)

    def scores(n, s_ref):
        qt = q_ref[0, rows(n), :].astype(F32).T * (HEAD_DIM ** -0.5 * LOG2E)
        for hh in range(2):
            in_head = (row < HEAD_DIM) if hh == 0 else (row >= HEAD_DIM)
            qh = jnp.where(in_head, qt, 0.0).astype(BF16)
            for w in range(n_win):
                s_ref[hh, w * t:(w + 1) * t] = jnp.dot(k_ref[0, rows(key_block(n, w)), :], qh,
                                                       preferred_element_type=F32)

    def softmax_pv(n, s_ref, mask_keys):
        outs = []
        for hh in range(2):
            st = s_ref[hh] + bias_ref[hh]
            if mask_keys:
                key_pos = (n - (n_win - 1)) * t + lax.broadcasted_iota(jnp.int32, (BAND_WINDOW, 1), 0)
                st = jnp.where(key_pos >= 0, st, NEG_INF)
            p = jnp.exp2(st - jnp.max(st, axis=0, keepdims=True)).astype(BF16)
            ot = jnp.dot(vt_ref[0, 0, key_block(n, 0)], p[0:t], preferred_element_type=F32)
            for w in range(1, n_win):
                ot = ot + jnp.dot(vt_ref[0, 0, key_block(n, w)], p[w * t:(w + 1) * t],
                                  preferred_element_type=F32)
            outs.append(ot[0:vd] / ot[vd:vd + 1])
        o_ref[0, rows(n), :] = jnp.where(row < HEAD_DIM, outs[0], outs[1]).T.astype(o_ref.dtype)

    first = n_win - 1
    assert first % 2 == 0
    scores(0, sa_ref)
    for n in range(first):
        cur, nxt = (sa_ref, sb_ref) if n % 2 == 0 else (sb_ref, sa_ref)
        scores(n + 1, nxt)
        softmax_pv(n, cur, True)

    def pair(jj, carry):
        n0 = first + 2 * jj
        scores(n0 + 1, sb_ref)
        softmax_pv(n0, sa_ref, False)
        scores(n0 + 2, sa_ref)
        softmax_pv(n0 + 1, sb_ref, False)
        return carry

    n_pairs = (nb - first - 2) // 2
    lax.fori_loop(0, n_pairs, pair, 0, unroll=BAND_UNROLL)
    for n in range(first + 2 * n_pairs, nb):
        if n + 1 < nb:
            scores(n + 1, sb_ref)
        softmax_pv(n, sa_ref, False)
        sa_ref, sb_ref = sb_ref, sa_ref


def _band_attention(hproj, vt, rel_bias):
    bsz, seq, _ = hproj.shape
    t = BAND_BLOCK
    vd = 2 * HEAD_DIM
    c = jnp.arange(BAND_WINDOW)[:, None]
    r = jnp.arange(t)[None, :]
    in_band = ((c // CHUNK) >= (r // CHUNK)) & ((c // CHUNK) <= (r // CHUNK) + B_LEFT_CHUNKS)
    rb = rel_bias.astype(F32) * LOG2E
    n_lo = BAND_WINDOW - 1 - B_LEFT_CHUNKS * CHUNK - REL_CLIP
    n_hi = t + BAND_WINDOW - n_lo - rb.shape[1]
    diag_vals = jnp.concatenate([jnp.broadcast_to(rb[:, :1], (B_HEADS, n_lo)), rb,
                                 jnp.broadcast_to(rb[:, -1:], (B_HEADS, n_hi))], axis=1)[:, None, :]
    qcol, kcol = (2 * A_W) // vd, (2 * A_W + B_W) // vd
    seq_spec = lambda col: pl.BlockSpec((1, seq, vd), lambda b, hp: (b, 0, col + hp))
    return pl.pallas_call(
        _band_attn_kernel,
        grid=(bsz, B_HEADS // 2),
        in_specs=[seq_spec(qcol), seq_spec(kcol),
                  pl.BlockSpec((1, 1, seq // t, vd + BF16_ROWS, t), lambda b, hp: (b, hp, 0, 0, 0)),
                  pl.BlockSpec((2, 1, t + BAND_WINDOW), lambda b, hp: (hp, 0, 0)),
                  pl.BlockSpec((BAND_WINDOW, t), lambda b, hp: (0, 0))],
        out_specs=pl.BlockSpec((1, seq, vd), lambda b, hp: (b, 0, hp)),
        out_shape=jax.ShapeDtypeStruct((bsz, seq, B_W), BF16),
        scratch_shapes=[pltpu.VMEM((2, BAND_WINDOW, t), F32),
                        pltpu.VMEM((2, BAND_WINDOW, t), F32), pltpu.VMEM((2, BAND_WINDOW, t), F32)],
        compiler_params=_params(("parallel", "parallel")),
        name="band_attn",
    )(hproj, hproj, vt, diag_vals, in_band.astype(F32))


def _retention_kernel(q_ref, k_ref, v_ref, gate_ref, dmask_ref, qdec_ref, kdec_ref, cdec_ref,
                      g_ref, beta_ref, o_ref, state_ref):
    ln = RET_BLOCK
    nb = q_ref.shape[1] // ln
    vd = 2 * HEAD_DIM
    scale = HEAD_DIM ** -0.5
    lane = lax.broadcasted_iota(jnp.int32, (ln, vd), 1)
    state_ref[...] = jnp.zeros(state_ref.shape, F32)

    def block(n, carry):
        rows = pl.ds(pl.multiple_of(n * ln, ln), ln)
        q = q_ref[0, rows, :]
        k = k_ref[0, rows, :]
        for hh in range(2):
            in_head = (lane < HEAD_DIM) if hh == 0 else (lane >= HEAD_DIM)
            qh = jnp.where(in_head, q, jnp.zeros_like(q))
            kh = jnp.where(in_head, k.astype(F32) * scale, 0.0)
            sl = slice(hh * vd, (hh + 1) * vd)
            v = v_ref[0, rows, sl]
            scores = lax.dot_general(qh, kh.astype(BF16), (((1,), (1,)), ((), ())),
                                     preferred_element_type=F32) * dmask_ref[hh]
            inner = jnp.dot(scores.astype(BF16), v, preferred_element_type=F32)
            state = state_ref[hh]
            cross = jnp.dot(qh, state.astype(BF16), preferred_element_type=F32) * qdec_ref[hh]
            kdt = (kh * kdec_ref[hh]).T.astype(BF16)
            kv = jnp.dot(kdt, v, preferred_element_type=F32)
            state_ref[hh] = cdec_ref[hh] * state + kv
            o = inner + cross
            mu = jnp.mean(o, axis=-1, keepdims=True)
            d = o - mu
            var = jnp.mean(d * d, axis=-1, keepdims=True)
            y = d * lax.rsqrt(var + LN_EPS) * g_ref[:, sl] + beta_ref[:, sl]
            gate = gate_ref[0, rows, sl]
            o_ref[0, rows, sl] = (y * (gate * jax.nn.sigmoid(gate))).astype(o_ref.dtype)
        return carry

    lax.fori_loop(0, nb, block, 0, unroll=8)


def _retention(qkv, gate, gn_g, gn_b):
    bsz, seq, _ = qkv.shape
    ln = RET_BLOCK
    vd = 2 * HEAD_DIM
    log_gamma = jnp.log(1.0 - 2.0 ** (-5.0 - jnp.arange(C_HEADS, dtype=F32)))
    idx = jnp.arange(ln, dtype=F32)
    diff = idx[:, None] - idx[None, :]
    dmask = jnp.where(diff >= 0, jnp.exp(log_gamma[:, None, None] * jnp.maximum(diff, 0.0)), 0.0)
    qdec = jnp.broadcast_to(jnp.exp(log_gamma[:, None] * (idx + 1.0))[:, :, None], (C_HEADS, ln, vd))
    kdec = jnp.broadcast_to(jnp.exp(log_gamma[:, None] * (ln - 1.0 - idx))[:, :, None], (C_HEADS, ln, vd))
    cdec = jnp.broadcast_to(jnp.exp(log_gamma * ln)[:, None, None], (C_HEADS, 1, vd))
    pair = lambda shape: pl.BlockSpec(shape, lambda b, hp: (hp,) + (0,) * (len(shape) - 1))
    return pl.pallas_call(
        _retention_kernel,
        grid=(bsz, C_HEADS // 2),
        in_specs=[pl.BlockSpec((1, seq, vd), lambda b, hp: (b, 0, hp)),
                  pl.BlockSpec((1, seq, vd), lambda b, hp: (b, 0, C_QK_W // vd + hp)),
                  pl.BlockSpec((1, seq, 2 * vd), lambda b, hp: (b, 0, (2 * C_QK_W) // (2 * vd) + hp)),
                  pl.BlockSpec((1, seq, 2 * vd), lambda b, hp: (b, 0, hp)),
                  pair((2, ln, ln)), pair((2, ln, vd)), pair((2, ln, vd)), pair((2, 1, vd)),
                  pl.BlockSpec((1, 2 * vd), lambda b, hp: (0, hp)),
                  pl.BlockSpec((1, 2 * vd), lambda b, hp: (0, hp))],
        out_specs=pl.BlockSpec((1, seq, 2 * vd), lambda b, hp: (b, 0, hp)),
        out_shape=jax.ShapeDtypeStruct((bsz, seq, C_W), BF16),
        scratch_shapes=[pltpu.VMEM((2, vd, vd), F32)],
        compiler_params=_params(("parallel", "parallel")),
        name="retention",
    )(qkv, qkv, qkv, gate, dmask, qdec, kdec, cdec, gn_g.reshape(1, C_W), gn_b.reshape(1, C_W))


def kernel(x, w_in_even, lam_q1, lam_k1, lam_q2, lam_k2, diff_subln_g, rel_bias, w_out_even,
           w_in_odd, ret_gn_g, ret_gn_b, sgu_ln_g, sgu_ln_b, sgu_w, sgu_b, w_out_odd,
           ln_mix_g, ln_mix_b, ffn_w_gate, ffn_w_up, ffn_w_down, ln_ffn_g, ln_ffn_b):
    bsz, seq, d = x.shape
    t = bsz * seq
    x2d = x.reshape(t, d)
    for l in range(DEPTH):
        j = l // 2
        if l % 2 == 0:
            hproj, qta, vta, vtb = _proj_even(x2d, w_in_even[j].astype(BF16), bsz, seq)
            hproj = hproj.reshape(bsz, seq, -1)
            lam_init = 0.8 - 0.6 * math.exp(-0.3 * l)
            mix_a = _diff_attention(hproj, qta, vta, lam_q1[j], lam_k1[j], lam_q2[j], lam_k2[j],
                                    diff_subln_g[j], lam_init)
            mix_b = _band_attention(hproj, vtb, rel_bias[j])
            w_out = w_out_even[j]
        else:
            qkv, gate, mix_b = _proj_odd(x2d, w_in_odd[j].astype(BF16), sgu_w[j], sgu_b[j],
                                         sgu_ln_g[j], sgu_ln_b[j])
            mix_a = _retention(qkv.reshape(bsz, seq, -1), gate.reshape(bsz, seq, -1),
                               ret_gn_g[j], ret_gn_b[j])
            w_out = w_out_odd[j]
        x2d = _mix_ffn(mix_a.reshape(t, -1), mix_b.reshape(t, -1), x2d, w_out.astype(BF16),
                       ln_mix_g[l], ln_mix_b[l], ffn_w_gate[l].astype(BF16), ffn_w_up[l].astype(BF16),
                       ffn_w_down[l].astype(BF16), ln_ffn_g[l], ln_ffn_b[l])
    return x2d.reshape(bsz, seq, d)
```

```python
import functools
import math

import jax
import numpy as np
import jax.numpy as jnp
from jax import lax
from jax.experimental import pallas as pl
from jax.experimental.pallas import tpu as pltpu

F32 = jnp.float32
BF16 = jnp.bfloat16

D_MODEL = 1024
HEAD_DIM = 64
CHUNK = 64
A_HEADS = 4
A_W = 512
B_HEADS = 8
B_W = 512
B_LEFT_CHUNKS = 8
REL_CLIP = 128
C_HEADS = 4
C_QK_W = 256
C_W = 512
D_CHUNK = 128
D_GROUPS = 4
D_WIDTH = 512
DEPTH = 2
ALPHA = (2.0 * DEPTH) ** 0.25
LN_EPS = 1e-5
NEG_INF = -1e30
LOG2E = 1.4426950408889634
SKIP_LOG2 = 160.0
NORM_SLACK = 1.0 + 2.0 ** -6

VMEM_LIMIT_V7X = 56 * 1024 * 1024
BF16_ROWS = 16
POS_SPLIT = 256
POS_TERMS = 3

ROW_BLOCK = 512
ODD_ROW_BLOCK = 1024
ATT_BLOCK = 512
BAND_BLOCK = 256
BAND_WINDOW = BAND_BLOCK + B_LEFT_CHUNKS * CHUNK
BAND_UNROLL = 7
RET_BLOCK = 256


def _params(semantics):
    return pltpu.CompilerParams(dimension_semantics=semantics, vmem_limit_bytes=VMEM_LIMIT_V7X)


def _const_spec(shape):
    nd = len(shape)
    return pl.BlockSpec(shape, lambda *_: (0,) * nd, pipeline_mode=pl.Buffered(1))


def _layer_norm(y, g, b):
    mu = jnp.mean(y, axis=-1, keepdims=True)
    d = y - mu
    var = jnp.mean(d * d, axis=-1, keepdims=True)
    return d * lax.rsqrt(var + LN_EPS) * g + b


def _proj_odd_kernel(x_ref, w_ref, ws_ref, bt_ref, g_ref, beta_ref, qkv_ref, gate_ref, od_ref):
    x = x_ref[...].astype(BF16)
    dot = lambda c: jnp.dot(x, w_ref[:, c:c + 512], preferred_element_type=F32)
    off_d = 2 * C_QK_W + 2 * C_W
    u = dot(off_d)
    v = dot(off_d + D_WIDTH)
    qkv_ref[:, 0:512] = dot(0).astype(BF16)
    qkv_ref[:, 512:1024] = dot(512).astype(BF16)
    gate_ref[...] = dot(2 * C_QK_W + C_W)
    u = jax.nn.gelu(u)
    v = _layer_norm(jax.nn.gelu(v), g_ref[...], beta_ref[...])
    gw = D_WIDTH // D_GROUPS
    r = lax.broadcasted_iota(jnp.int32, (D_CHUNK, D_CHUNK), 0)
    c = lax.broadcasted_iota(jnp.int32, (D_CHUNK, D_CHUNK), 1)
    for g in range(D_GROUPS):
        w = jnp.where(r >= c, ws_ref[g], 0.0).astype(BF16)
        bias = bt_ref[:, g:g + 1]
        cols = slice(g * gw, (g + 1) * gw)
        vg = v[:, cols].astype(BF16)
        for n in range(x_ref.shape[0] // D_CHUNK):
            rows = slice(n * D_CHUNK, (n + 1) * D_CHUNK)
            mixed = jnp.dot(w, vg[rows], preferred_element_type=F32) + bias
            od_ref[rows, cols] = (u[rows, cols] * mixed).astype(od_ref.dtype)


def _proj_odd(x2d, w_bf16, w_s, b_s, ln_g, ln_b):
    t, k = x2d.shape
    n = w_bf16.shape[1]
    assert ODD_ROW_BLOCK % D_CHUNK == 0 and n == 2 * C_QK_W + 2 * C_W + 2 * D_WIDTH
    row_spec = lambda wd: pl.BlockSpec((ODD_ROW_BLOCK, wd), lambda i: (i, 0))
    return pl.pallas_call(
        _proj_odd_kernel,
        grid=(t // ODD_ROW_BLOCK,),
        in_specs=[row_spec(k), _const_spec((k, n)), _const_spec((D_GROUPS, D_CHUNK, D_CHUNK)),
                  _const_spec((D_CHUNK, D_GROUPS)), _const_spec((1, D_WIDTH)), _const_spec((1, D_WIDTH))],
        out_specs=[row_spec(2 * C_QK_W + C_W), row_spec(C_W), row_spec(D_WIDTH)],
        out_shape=[jax.ShapeDtypeStruct((t, 2 * C_QK_W + C_W), BF16),
                   jax.ShapeDtypeStruct((t, C_W), F32),
                   jax.ShapeDtypeStruct((t, D_WIDTH), BF16)],
        compiler_params=_params(("parallel",)),
        name="in_proj_odd",
    )(x2d, w_bf16, w_s.astype(F32), b_s.T.astype(F32), ln_g.reshape(1, D_WIDTH), ln_b.reshape(1, D_WIDTH))


def _transposed_value_tiles(vals, block):
    ones_tile = (lax.broadcasted_iota(jnp.int32, (BF16_ROWS, block), 0) == 0).astype(BF16)
    return [jnp.concatenate([vals[r:r + block].T.astype(BF16), ones_tile], axis=0)
            for r in range(0, vals.shape[0], block)]


def _bf16_terms(value, n_terms):
    terms, rest = [], np.float32(value)
    for _ in range(n_terms):
        bits = rest.view(np.uint32)
        kept = (bits + np.uint32(0x7FFF) + ((bits >> np.uint32(16)) & np.uint32(1))) & np.uint32(0xFFFF0000)
        terms.append(float(kept.view(np.float32)))
        rest = np.float32(rest - kept.view(np.float32))
    return terms


def _alibi_slope(h):
    return 2.0 ** (-8.0 * (h + 1) / A_HEADS)


def _transposed_query_tiles(q, block):
    tiles = []
    for r in range(0, q.shape[0], block):
        qt = q[r:r + block].T * (HEAD_DIM ** -0.5 * LOG2E)
        row = lax.broadcasted_iota(jnp.int32, qt.shape, 0)
        tiles.append((jnp.where(row < HEAD_DIM, qt, 0.0).astype(BF16),
                      jnp.where(row >= HEAD_DIM, qt, 0.0).astype(BF16)))
    return tiles


def _proj_even_kernel(x_ref, w_ref, h_ref, qta_ref, vta_ref, vtb_ref):
    x = x_ref[...].astype(BF16)
    vd = 2 * HEAD_DIM
    dot = lambda c: jnp.dot(x, w_ref[:, c:c + 512], preferred_element_type=F32)
    vb = dot(3 * A_W + 2 * B_W)
    for hp in range(B_HEADS // 2):
        tiles = _transposed_value_tiles(vb[:, hp * vd:(hp + 1) * vd], BAND_BLOCK)
        for n, tile in enumerate(tiles):
            vtb_ref[0, hp, n] = tile
    va = dot(2 * A_W)
    for h in range(A_HEADS):
        vta_ref[0, h, 0] = _transposed_value_tiles(va[:, h * vd:(h + 1) * vd], ATT_BLOCK)[0]
    qa = dot(0)
    srow = lax.broadcasted_iota(jnp.int32, (vd, ATT_BLOCK), 0)
    for h in range(A_HEADS):
        slope_rows = jnp.zeros((vd, ATT_BLOCK), F32)
        for k, term in enumerate(_bf16_terms(_alibi_slope(h) * LOG2E, POS_TERMS)):
            slope_rows = jnp.where((srow == k) | (srow == k + POS_TERMS), term, slope_rows)
        for mp, tile in enumerate(_transposed_query_tiles(qa[:, h * vd:(h + 1) * vd], ATT_BLOCK)[0]):
            qta_ref[0, h, 0, mp, 0:vd] = tile
            qta_ref[0, h, 0, mp, vd:2 * vd] = slope_rows.astype(BF16)
    ka = dot(A_W)
    r = lax.broadcasted_iota(jnp.int32, (ROW_BLOCK, vd), 0)
    col = lax.broadcasted_iota(jnp.int32, (ROW_BLOCK, vd), 1)
    pos = jnp.where(col < POS_TERMS, r & (POS_SPLIT - 1),
                    jnp.where(col < 2 * POS_TERMS, r & ~(POS_SPLIT - 1), 0)).astype(F32).astype(BF16)
    for h in range(A_HEADS):
        h_ref[:, 2 * vd * h:2 * vd * h + vd] = ka[:, h * vd:(h + 1) * vd].astype(BF16)
        h_ref[:, 2 * vd * h + vd:2 * vd * (h + 1)] = pos
    h_ref[:, 2 * A_W:2 * A_W + B_W] = dot(3 * A_W).astype(BF16)
    h_ref[:, 2 * A_W + B_W:2 * A_W + 2 * B_W] = dot(3 * A_W + B_W).astype(BF16)


def _proj_even(x2d, w_bf16, bsz, seq):
    t, k = x2d.shape
    n = w_bf16.shape[1]
    assert ROW_BLOCK == ATT_BLOCK and seq % ROW_BLOCK == 0
    per_seq = seq // ROW_BLOCK
    vd = 2 * HEAD_DIM
    vrows = vd + BF16_ROWS
    nba, nbb, per_step_b = seq // ATT_BLOCK, seq // BAND_BLOCK, ROW_BLOCK // BAND_BLOCK
    seq_block = lambda i: (i // per_seq, 0, i % per_seq, 0, 0)
    seq_block6 = lambda i: (i // per_seq, 0, i % per_seq, 0, 0, 0)
    return pl.pallas_call(
        _proj_even_kernel,
        grid=(t // ROW_BLOCK,),
        in_specs=[pl.BlockSpec((ROW_BLOCK, k), lambda i: (i, 0)), _const_spec((k, n))],
        out_specs=[pl.BlockSpec((ROW_BLOCK, 2 * A_W + 2 * B_W), lambda i: (i, 0)),
                   pl.BlockSpec((1, A_HEADS, 1, 2, 2 * vd, ATT_BLOCK), seq_block6),
                   pl.BlockSpec((1, A_HEADS, 1, vrows, ATT_BLOCK), seq_block),
                   pl.BlockSpec((1, B_HEADS // 2, per_step_b, vrows, BAND_BLOCK), seq_block)],
        out_shape=[jax.ShapeDtypeStruct((t, 2 * A_W + 2 * B_W), BF16),
                   jax.ShapeDtypeStruct((bsz, A_HEADS, nba, 2, 2 * vd, ATT_BLOCK), BF16),
                   jax.ShapeDtypeStruct((bsz, A_HEADS, nba, vrows, ATT_BLOCK), BF16),
                   jax.ShapeDtypeStruct((bsz, B_HEADS // 2, nbb, vrows, BAND_BLOCK), BF16)],
        compiler_params=_params(("parallel",)),
        name="in_proj_even",
    )(x2d, w_bf16)


def _mix_ffn_kernel(a_ref, b_ref, x_ref, wo_ref, g1_ref, b1_ref, wg_ref, wu_ref, wd_ref,
                    g2_ref, b2_ref, o_ref):
    ka = a_ref.shape[-1]
    half = x_ref.shape[0] // 2
    halves = (slice(0, half), slice(half, 2 * half))
    mix = [jnp.dot(a_ref[rows, :], wo_ref[0:ka, :], preferred_element_type=F32)
           + jnp.dot(b_ref[rows, :], wo_ref[ka:, :], preferred_element_type=F32) for rows in halves]
    x1 = [_layer_norm(ALPHA * x_ref[rows, :] + m, g1_ref[...], b1_ref[...]) for rows, m in zip(halves, mix)]
    xb = [v.astype(BF16) for v in x1]
    gate = [jnp.dot(v, wg_ref[...], preferred_element_type=F32) for v in xb]
    up = [jnp.dot(v, wu_ref[...], preferred_element_type=F32) for v in xb]
    h = [(g * jax.nn.sigmoid(g) * u).astype(BF16) for g, u in zip(gate, up)]
    y = [ALPHA * v + jnp.dot(hh, wd_ref[...], preferred_element_type=F32) for v, hh in zip(x1, h)]
    for rows, v in zip(halves, y):
        o_ref[rows, :] = _layer_norm(v, g2_ref[...], b2_ref[...])


def _mix_ffn(a, b, x2d, wo, g1, b1, wg, wu, wd, g2, b2):
    t, d = x2d.shape
    ka, kb = a.shape[1], b.shape[1]
    f = wg.shape[1]
    vec = lambda v: v.reshape(1, d)
    return pl.pallas_call(
        _mix_ffn_kernel,
        grid=(t // ROW_BLOCK,),
        in_specs=[pl.BlockSpec((ROW_BLOCK, ka), lambda i: (i, 0)),
                  pl.BlockSpec((ROW_BLOCK, kb), lambda i: (i, 0)),
                  pl.BlockSpec((ROW_BLOCK, d), lambda i: (i, 0)),
                  _const_spec((ka + kb, d)), _const_spec((1, d)), _const_spec((1, d)),
                  _const_spec((d, f)), _const_spec((d, f)), _const_spec((f, d)),
                  _const_spec((1, d)), _const_spec((1, d))],
        out_specs=pl.BlockSpec((ROW_BLOCK, d), lambda i: (i, 0)),
        out_shape=jax.ShapeDtypeStruct((t, d), F32),
        compiler_params=_params(("parallel",)),
        name="mix_ffn",
    )(a, b, x2d, wo, vec(g1), vec(b1), wg, wu, wd, vec(g2), vec(b2))


def _diff_attn_kernel(qt_ref, k_ref, vt_ref, diag_ref, diag_max_ref, step_ref,
                      lq1_ref, lk1_ref, lq2_ref, lk2_ref, g_ref, o_ref,
                      sa_ref, sb_ref, ra_ref, rb_ref, m_ref, acc_ref, knorm_ref, *, lam_init):
    t = ATT_BLOCK
    n_chunks = t // CHUNK
    i = pl.program_id(2)
    qt_ref = qt_ref.at[0, 0, 0]
    vd = 2 * HEAD_DIM

    @pl.when(i == 0)
    def _():
        lane = lax.broadcasted_iota(jnp.int32, (vd, vd), 0)
        col = lax.broadcasted_iota(jnp.int32, (vd, vd), 1)
        pick = ((lane // HEAD_DIM) == col).astype(BF16)

        def block_norms(j, carry):
            kb = k_ref[0, pl.ds(pl.multiple_of(j * t, t), t), 0:vd].astype(F32)
            sums = jnp.dot((kb * kb).astype(BF16), pick, preferred_element_type=F32)
            return jnp.maximum(carry, jnp.max(sums, axis=0, keepdims=True))

        n = lax.fori_loop(0, k_ref.shape[1] // t, block_norms, jnp.zeros((1, vd), F32), unroll=True)
        knorm_ref[0] = jnp.broadcast_to(jnp.sqrt(n[:, 0:1]), (1, t))
        knorm_ref[1] = jnp.broadcast_to(jnp.sqrt(n[:, 1:2]), (1, t))

    half = t // 2
    lo, hi = slice(0, half), slice(half, t)

    def chunk_maxima(s):
        return jnp.max(s.reshape(s.shape[0] // CHUNK, CHUNK // 8, 8, s.shape[1]), axis=1)

    def diagonal_scores():
        start = pl.multiple_of(i * t, t)
        for mp in range(2):
            top = jnp.dot(k_ref[0, pl.ds(start, half), 0:vd], qt_ref[mp, 0:vd], preferred_element_type=F32)
            bot = jnp.dot(k_ref[0, pl.ds(pl.multiple_of(start + half, half), half), 0:vd],
                          qt_ref[mp, 0:vd, hi], preferred_element_type=F32)
            sa_ref[mp, lo] = top
            sa_ref[mp, hi, hi] = bot
            ra_ref[mp, 0:n_chunks // 2] = chunk_maxima(top)
            ra_ref[mp, n_chunks // 2:, :, hi] = chunk_maxima(bot)
            ra_ref[mp, n_chunks // 2:, :, lo] = jnp.zeros((n_chunks // 2, 8, half), F32)

    def diagonal_softmax_pv():
        vt_blk = vt_ref[0, 0, i]
        for mp in range(2):
            bound = jnp.max(ra_ref[mp] + diag_max_ref[0][:, None, :], axis=0)
            m_new = jnp.max(bound, axis=0, keepdims=True)
            p_top = jnp.exp2(sa_ref[mp, lo] + diag_ref[0, lo] - m_new).astype(BF16)
            p_bot = jnp.exp2(sa_ref[mp, hi, hi] + diag_ref[0, hi, hi] - m_new[:, hi]).astype(BF16)
            acc = jnp.dot(vt_blk[:, lo], p_top, preferred_element_type=F32)
            acc_ref[mp, :, lo] = acc[:, lo]
            acc_ref[mp, :, hi] = acc[:, hi] + jnp.dot(vt_blk[:, hi], p_bot, preferred_element_type=F32)
            m_ref[mp] = m_new + step_ref[0]

    def scores(j, s_ref, r_ref):
        start = pl.multiple_of(j * t, t)
        for mp in range(2):
            s = jnp.dot(k_ref[0, pl.ds(start, t), :], qt_ref[mp], preferred_element_type=F32)
            s_ref[mp] = s
            r_ref[mp] = chunk_maxima(s)

    def softmax_pv(j, s_ref, r_ref):
        vt_blk = vt_ref[0, 0, j]
        step = step_ref[0]
        for mp in range(2):
            m_prev = m_ref[mp]
            bound = jnp.max(jnp.max(r_ref[mp], axis=0), axis=0, keepdims=True)
            m_new = jnp.maximum(m_prev, bound)
            alpha = jnp.exp2(m_prev - m_new)
            p = jnp.exp2(s_ref[mp] - m_new)
            acc_ref[mp] = alpha * acc_ref[mp] + jnp.dot(vt_blk, p.astype(BF16),
                                                        preferred_element_type=F32)
            m_ref[mp] = m_new + step

    def prev_block(r):
        return jnp.maximum(i - r, 0)

    diagonal_scores()
    scores(prev_block(1), sb_ref, rb_ref)
    diagonal_softmax_pv()

    tile_top = step_ref[0] * ((t - 1) / t)
    inv_step = 1.0 / step_ref[0]
    reach = jnp.zeros((1, t), F32)
    for mp in range(2):
        qb = qt_ref[mp, 0:vd].astype(F32)
        qnorm = jnp.sqrt(jnp.sum(qb * qb, axis=0, keepdims=True))
        gap = qnorm * knorm_ref[mp] * NORM_SLACK + tile_top - m_ref[mp]
        reach = jnp.maximum(reach, (gap + SKIP_LOG2) * inv_step + 1.0)
    reach = jnp.floor(jnp.minimum(reach, float(k_ref.shape[1] // t))).astype(jnp.int32)
    n_keep = jnp.minimum(jnp.max(reach), i)

    def past_pair(r0):
        scores(prev_block(r0 + 1), sa_ref, ra_ref)
        softmax_pv(i - r0, sb_ref, rb_ref)
        scores(prev_block(r0 + 2), sb_ref, rb_ref)
        softmax_pv(i - r0 - 1, sa_ref, ra_ref)

    def past_oct(jj, carry):
        for pr in range(4):
            past_pair(8 * jj + 2 * pr + 1)
        return carry

    n_octs = lax.shift_right_logical(n_keep, 3)
    lax.fori_loop(0, n_octs, past_oct, 0)
    done = 8 * n_octs

    @pl.when((n_keep & 4) == 4)
    def _():
        past_pair(done + 1)
        past_pair(done + 3)

    @pl.when((n_keep & 2) == 2)
    def _():
        past_pair(done + (n_keep & 4) + 1)

    @pl.when((n_keep & 1) == 1)
    def _():
        softmax_pv(i - n_keep, sb_ref, rb_ref)

    lam = (jnp.exp(jnp.sum(lq1_ref[...] * lk1_ref[...], axis=-1, keepdims=True))
           - jnp.exp(jnp.sum(lq2_ref[...] * lk2_ref[...], axis=-1, keepdims=True)) + lam_init)
    ot = (acc_ref[0, 0:vd] / acc_ref[0, vd:vd + 1]
          - lam * (acc_ref[1, 0:vd] / acc_ref[1, vd:vd + 1]))
    ms = jnp.mean(ot * ot, axis=0, keepdims=True)
    y = (ot * lax.rsqrt(ms + LN_EPS)).T * g_ref[...] * (1.0 - lam_init)
    o_ref[0] = y.astype(o_ref.dtype)


def _diff_attention(keys, qt, vt, lam_q1, lam_k1, lam_q2, lam_k2, subln_g, lam_init):
    bsz, seq, _ = keys.shape
    t = ATT_BLOCK
    nb = seq // t
    n_chunks = t // CHUNK
    vd = 2 * HEAD_DIM
    slopes = jnp.asarray([_alibi_slope(h) for h in range(A_HEADS)], F32)
    kk = jnp.arange(t)[:, None]
    qq = jnp.arange(t)[None, :]
    rel = (qq - kk).astype(F32)
    c = slopes * LOG2E
    allowed = (kk // CHUNK) <= (qq // CHUNK)
    diag = jnp.where(allowed[None], c[:, None, None] * (qq.astype(F32) - jnp.abs(rel))[None], NEG_INF)
    chunk_max = lambda tile: jnp.max(tile.reshape(A_HEADS, n_chunks, CHUNK, t), axis=2)
    step = jnp.broadcast_to((slopes * (LOG2E * t))[:, None, None], (A_HEADS, 1, t))
    vec = lambda v: v.reshape(1, HEAD_DIM).astype(F32)
    small = lambda shape: pl.BlockSpec(shape, lambda b, h, i: (0,) * len(shape))
    per_head = lambda shape: pl.BlockSpec(shape, lambda b, h, i: (h,) + (0,) * (len(shape) - 1))
    return pl.pallas_call(
        functools.partial(_diff_attn_kernel, lam_init=lam_init),
        grid=(bsz, A_HEADS, nb),
        in_specs=[pl.BlockSpec((1, 1, 1, 2, 2 * vd, t), lambda b, h, i: (b, h, i, 0, 0, 0)),
                  pl.BlockSpec((1, seq, 2 * vd), lambda b, h, i: (b, 0, h)),
                  pl.BlockSpec((1, 1, nb, vd + BF16_ROWS, t), lambda b, h, i: (b, h, 0, 0, 0)),
                  per_head((1, t, t)),
                  per_head((1, n_chunks, t)), per_head((1, 1, t)),
                  small((1, HEAD_DIM)), small((1, HEAD_DIM)), small((1, HEAD_DIM)), small((1, HEAD_DIM)),
                  small((1, vd))],
        out_specs=pl.BlockSpec((1, t, vd), lambda b, h, i: (b, i, h)),
        out_shape=jax.ShapeDtypeStruct((bsz, seq, A_W), BF16),
        scratch_shapes=[pltpu.VMEM((2, t, t), F32), pltpu.VMEM((2, t, t), F32),
                        pltpu.VMEM((2, n_chunks, 8, t), F32), pltpu.VMEM((2, n_chunks, 8, t), F32),
                        pltpu.VMEM((2, 1, t), F32),
                        pltpu.VMEM((2, vd + BF16_ROWS, t), F32),
                        pltpu.VMEM((2, 1, t), F32)],
        compiler_params=_params(("parallel", "parallel", "arbitrary")),
        name="diff_attn",
    )(qt, keys, vt, diag, chunk_max(diag), step,
      vec(lam_q1), vec(lam_k1), vec(lam_q2), vec(lam_k2), subln_g.reshape(1, vd).astype(F32))


def _band_attn_kernel(q_ref, k_ref, vt_ref, diag_vals_ref, band_ref, o_ref, bias_ref, sa_ref, sb_ref):
    t = BAND_BLOCK
    nb = k_ref.shape[1] // t
    n_win = BAND_WINDOW // t
    vd = 2 * HEAD_DIM
    row = lax.broadcasted_iota(jnp.int32, (vd, t), 0)
    n_vals = diag_vals_ref.shape[-1]

    for hh in range(2):
        vals = jnp.broadcast_to(diag_vals_ref[hh], (8, n_vals))
        for c0 in range(0, BAND_WINDOW, 8):
            rolled = pltpu.roll(vals, (c0 - (BAND_WINDOW - 1)) % n_vals, axis=1, stride=1, stride_axis=0)
            bias_ref[hh, c0:c0 + 8, :] = jnp.where(band_ref[c0:c0 + 8, :] > 0.0, rolled[:, 0:t], NEG_INF)

    def rows(n):
        return pl.ds(pl.multiple_of(n * t, t), t)

    def key_block(n, w):
        return jnp.maximum(n - (n_win - 1) + w, 0)

    def scores(n, s_ref):
        qt = q_ref[0, rows(n), :].astype(F32).T * (HEAD_DIM ** -0.5 * LOG2E)
        for hh in range(2):
            in_head = (row < HEAD_DIM) if hh == 0 else (row >= HEAD_DIM)
            qh = jnp.where(in_head, qt, 0.0).astype(BF16)
            for w in range(n_win):
                s_ref[hh, w * t:(w + 1) * t] = jnp.dot(k_ref[0, rows(key_block(n, w)), :], qh,
                                                       preferred_element_type=F32)

    def softmax_pv(n, s_ref, mask_keys):
        outs = []
        for hh in range(2):
            st = s_ref[hh] + bias_ref[hh]
            if mask_keys:
                key_pos = (n - (n_win - 1)) * t + lax.broadcasted_iota(jnp.int32, (BAND_WINDOW, 1), 0)
                st = jnp.where(key_pos >= 0, st, NEG_INF)
            p = jnp.exp2(st - jnp.max(st, axis=0, keepdims=True)).astype(BF16)
            ot = jnp.dot(vt_ref[0, 0, key_block(n, 0)], p[0:t], preferred_element_type=F32)
            for w in range(1, n_win):
                ot = ot + jnp.dot(vt_ref[0, 0, key_block(n, w)], p[w * t:(w + 1) * t],
                                  preferred_element_type=F32)
            outs.append(ot[0:vd] / ot[vd:vd + 1])
        o_ref[0, rows(n), :] = jnp.where(row < HEAD_DIM, outs[0], outs[1]).T.astype(o_ref.dtype)

    first = n_win - 1
    assert first % 2 == 0
    scores(0, sa_ref)
    for n in range(first):
        cur, nxt = (sa_ref, sb_ref) if n % 2 == 0 else (sb_ref, sa_ref)
        scores(n + 1, nxt)
        softmax_pv(n, cur, True)

    def pair(jj, carry):
        n0 = first + 2 * jj
        scores(n0 + 1, sb_ref)
        softmax_pv(n0, sa_ref, False)
        scores(n0 + 2, sa_ref)
        softmax_pv(n0 + 1, sb_ref, False)
        return carry

    n_pairs = (nb - first - 2) // 2
    lax.fori_loop(0, n_pairs, pair, 0, unroll=BAND_UNROLL)
    for n in range(first + 2 * n_pairs, nb):
        if n + 1 < nb:
            scores(n + 1, sb_ref)
        softmax_pv(n, sa_ref, False)
        sa_ref, sb_ref = sb_ref, sa_ref


def _band_attention(hproj, vt, rel_bias):
    bsz, seq, _ = hproj.shape
    t = BAND_BLOCK
    vd = 2 * HEAD_DIM
    c = jnp.arange(BAND_WINDOW)[:, None]
    r = jnp.arange(t)[None, :]
    in_band = ((c // CHUNK) >= (r // CHUNK)) & ((c // CHUNK) <= (r // CHUNK) + B_LEFT_CHUNKS)
    rb = rel_bias.astype(F32) * LOG2E
    n_lo = BAND_WINDOW - 1 - B_LEFT_CHUNKS * CHUNK - REL_CLIP
    n_hi = t + BAND_WINDOW - n_lo - rb.shape[1]
    diag_vals = jnp.concatenate([jnp.broadcast_to(rb[:, :1], (B_HEADS, n_lo)), rb,
                                 jnp.broadcast_to(rb[:, -1:], (B_HEADS, n_hi))], axis=1)[:, None, :]
    qcol, kcol = (2 * A_W) // vd, (2 * A_W + B_W) // vd
    seq_spec = lambda col: pl.BlockSpec((1, seq, vd), lambda b, hp: (b, 0, col + hp))
    return pl.pallas_call(
        _band_attn_kernel,
        grid=(bsz, B_HEADS // 2),
        in_specs=[seq_spec(qcol), seq_spec(kcol),
                  pl.BlockSpec((1, 1, seq // t, vd + BF16_ROWS, t), lambda b, hp: (b, hp, 0, 0, 0)),
                  pl.BlockSpec((2, 1, t + BAND_WINDOW), lambda b, hp: (hp, 0, 0)),
                  pl.BlockSpec((BAND_WINDOW, t), lambda b, hp: (0, 0))],
        out_specs=pl.BlockSpec((1, seq, vd), lambda b, hp: (b, 0, hp)),
        out_shape=jax.ShapeDtypeStruct((bsz, seq, B_W), BF16),
        scratch_shapes=[pltpu.VMEM((2, BAND_WINDOW, t), F32),
                        pltpu.VMEM((2, BAND_WINDOW, t), F32), pltpu.VMEM((2, BAND_WINDOW, t), F32)],
        compiler_params=_params(("parallel", "parallel")),
        name="band_attn",
    )(hproj, hproj, vt, diag_vals, in_band.astype(F32))


def _retention_kernel(q_ref, k_ref, v_ref, gate_ref, dmask_ref, qdec_ref, kdec_ref, cdec_ref,
                      g_ref, beta_ref, o_ref, state_ref):
    ln = RET_BLOCK
    nb = q_ref.shape[1] // ln
    vd = 2 * HEAD_DIM
    scale = HEAD_DIM ** -0.5
    lane = lax.broadcasted_iota(jnp.int32, (ln, vd), 1)
    state_ref[...] = jnp.zeros(state_ref.shape, F32)

    def block(n, carry):
        rows = pl.ds(pl.multiple_of(n * ln, ln), ln)
        q = q_ref[0, rows, :]
        k = k_ref[0, rows, :]
        for hh in range(2):
            in_head = (lane < HEAD_DIM) if hh == 0 else (lane >= HEAD_DIM)
            qh = jnp.where(in_head, q, jnp.zeros_like(q))
            kh = jnp.where(in_head, k.astype(F32) * scale, 0.0)
            sl = slice(hh * vd, (hh + 1) * vd)
            v = v_ref[0, rows, sl]
            scores = lax.dot_general(qh, kh.astype(BF16), (((1,), (1,)), ((), ())),
                                     preferred_element_type=F32) * dmask_ref[hh]
            inner = jnp.dot(scores.astype(BF16), v, preferred_element_type=F32)
            state = state_ref[hh]
            cross = jnp.dot(qh, state.astype(BF16), preferred_element_type=F32) * qdec_ref[hh]
            kdt = (kh * kdec_ref[hh]).T.astype(BF16)
            kv = jnp.dot(kdt, v, preferred_element_type=F32)
            state_ref[hh] = cdec_ref[hh] * state + kv
            o = inner + cross
            mu = jnp.mean(o, axis=-1, keepdims=True)
            d = o - mu
            var = jnp.mean(d * d, axis=-1, keepdims=True)
            y = d * lax.rsqrt(var + LN_EPS) * g_ref[:, sl] + beta_ref[:, sl]
            gate = gate_ref[0, rows, sl]
            o_ref[0, rows, sl] = (y * (gate * jax.nn.sigmoid(gate))).astype(o_ref.dtype)
        return carry

    lax.fori_loop(0, nb, block, 0, unroll=8)


def _retention(qkv, gate, gn_g, gn_b):
    bsz, seq, _ = qkv.shape
    ln = RET_BLOCK
    vd = 2 * HEAD_DIM
    log_gamma = jnp.log(1.0 - 2.0 ** (-5.0 - jnp.arange(C_HEADS, dtype=F32)))
    idx = jnp.arange(ln, dtype=F32)
    diff = idx[:, None] - idx[None, :]
    dmask = jnp.where(diff >= 0, jnp.exp(log_gamma[:, None, None] * jnp.maximum(diff, 0.0)), 0.0)
    qdec = jnp.broadcast_to(jnp.exp(log_gamma[:, None] * (idx + 1.0))[:, :, None], (C_HEADS, ln, vd))
    kdec = jnp.broadcast_to(jnp.exp(log_gamma[:, None] * (ln - 1.0 - idx))[:, :, None], (C_HEADS, ln, vd))
    cdec = jnp.broadcast_to(jnp.exp(log_gamma * ln)[:, None, None], (C_HEADS, 1, vd))
    pair = lambda shape: pl.BlockSpec(shape, lambda b, hp: (hp,) + (0,) * (len(shape) - 1))
    return pl.pallas_call(
        _retention_kernel,
        grid=(bsz, C_HEADS // 2),
        in_specs=[pl.BlockSpec((1, seq, vd), lambda b, hp: (b, 0, hp)),
                  pl.BlockSpec((1, seq, vd), lambda b, hp: (b, 0, C_QK_W // vd + hp)),
                  pl.BlockSpec((1, seq, 2 * vd), lambda b, hp: (b, 0, (2 * C_QK_W) // (2 * vd) + hp)),
                  pl.BlockSpec((1, seq, 2 * vd), lambda b, hp: (b, 0, hp)),
                  pair((2, ln, ln)), pair((2, ln, vd)), pair((2, ln, vd)), pair((2, 1, vd)),
                  pl.BlockSpec((1, 2 * vd), lambda b, hp: (0, hp)),
                  pl.BlockSpec((1, 2 * vd), lambda b, hp: (0, hp))],
        out_specs=pl.BlockSpec((1, seq, 2 * vd), lambda b, hp: (b, 0, hp)),
        out_shape=jax.ShapeDtypeStruct((bsz, seq, C_W), BF16),
        scratch_shapes=[pltpu.VMEM((2, vd, vd), F32)],
        compiler_params=_params(("parallel", "parallel")),
        name="retention",
    )(qkv, qkv, qkv, gate, dmask, qdec, kdec, cdec, gn_g.reshape(1, C_W), gn_b.reshape(1, C_W))


def kernel(x, w_in_even, lam_q1, lam_k1, lam_q2, lam_k2, diff_subln_g, rel_bias, w_out_even,
           w_in_odd, ret_gn_g, ret_gn_b, sgu_ln_g, sgu_ln_b, sgu_w, sgu_b, w_out_odd,
           ln_mix_g, ln_mix_b, ffn_w_gate, ffn_w_up, ffn_w_down, ln_ffn_g, ln_ffn_b):
    bsz, seq, d = x.shape
    t = bsz * seq
    x2d = x.reshape(t, d)
    for l in range(DEPTH):
        j = l // 2
        if l % 2 == 0:
            hproj, qta, vta, vtb = _proj_even(x2d, w_in_even[j].astype(BF16), bsz, seq)
            hproj = hproj.reshape(bsz, seq, -1)
            lam_init = 0.8 - 0.6 * math.exp(-0.3 * l)
            mix_a = _diff_attention(hproj, qta, vta, lam_q1[j], lam_k1[j], lam_q2[j], lam_k2[j],
                                    diff_subln_g[j], lam_init)
            mix_b = _band_attention(hproj, vtb, rel_bias[j])
            w_out = w_out_even[j]
        else:
            qkv, gate, mix_b = _proj_odd(x2d, w_in_odd[j].astype(BF16), sgu_w[j], sgu_b[j],
                                         sgu_ln_g[j], sgu_ln_b[j])
            mix_a = _retention(qkv.reshape(bsz, seq, -1), gate.reshape(bsz, seq, -1),
                               ret_gn_g[j], ret_gn_b[j])
            w_out = w_out_odd[j]
        x2d = _mix_ffn(mix_a.reshape(t, -1), mix_b.reshape(t, -1), x2d, w_out.astype(BF16),
                       ln_mix_g[l], ln_mix_b[l], ffn_w_gate[l].astype(BF16), ffn_w_up[l].astype(BF16),
                       ffn_w_down[l].astype(BF16), ln_ffn_g[l], ln_ffn_b[l])
    return x2d.reshape(bsz, seq, d)
```

```python
import functools
import math

import jax
import numpy as np
import jax.numpy as jnp
from jax import lax
from jax.experimental import pallas as pl
from jax.experimental.pallas import tpu as pltpu

F32 = jnp.float32
BF16 = jnp.bfloat16

D_MODEL = 1024
HEAD_DIM = 64
CHUNK = 64
A_HEADS = 4
A_W = 512
B_HEADS = 8
B_W = 512
B_LEFT_CHUNKS = 8
REL_CLIP = 128
C_HEADS = 4
C_QK_W = 256
C_W = 512
D_CHUNK = 128
D_GROUPS = 4
D_WIDTH = 512
DEPTH = 2
ALPHA = (2.0 * DEPTH) ** 0.25
LN_EPS = 1e-5
NEG_INF = -1e30
LOG2E = 1.4426950408889634
SKIP_LOG2 = 160.0
NORM_SLACK = 1.0 + 2.0 ** -6

VMEM_LIMIT_V7X = 56 * 1024 * 1024
BF16_ROWS = 16
POS_SPLIT = 256
POS_TERMS = 3

ROW_BLOCK = 512
ODD_ROW_BLOCK = 1024
ATT_BLOCK = 512
BAND_BLOCK = 256
BAND_WINDOW = BAND_BLOCK + B_LEFT_CHUNKS * CHUNK
BAND_UNROLL = 7
RET_BLOCK = 256


def _params(semantics):
    return pltpu.CompilerParams(dimension_semantics=semantics, vmem_limit_bytes=VMEM_LIMIT_V7X)


def _const_spec(shape):
    nd = len(shape)
    return pl.BlockSpec(shape, lambda *_: (0,) * nd, pipeline_mode=pl.Buffered(1))


def _layer_norm(y, g, b):
    mu = jnp.mean(y, axis=-1, keepdims=True)
    d = y - mu
    var = jnp.mean(d * d, axis=-1, keepdims=True)
    return d * lax.rsqrt(var + LN_EPS) * g + b


def _proj_odd_kernel(x_ref, w_ref, ws_ref, bt_ref, g_ref, beta_ref, qkv_ref, gate_ref, od_ref):
    x = x_ref[...].astype(BF16)
    dot = lambda c: jnp.dot(x, w_ref[:, c:c + 512], preferred_element_type=F32)
    off_d = 2 * C_QK_W + 2 * C_W
    u = dot(off_d)
    v = dot(off_d + D_WIDTH)
    qkv_ref[:, 0:512] = dot(0).astype(BF16)
    qkv_ref[:, 512:1024] = dot(512).astype(BF16)
    g_pre = dot(2 * C_QK_W + C_W)
    gate_ref[...] = g_pre * jax.nn.sigmoid(g_pre)
    u = jax.nn.gelu(u)
    v = _layer_norm(jax.nn.gelu(v), g_ref[...], beta_ref[...])
    gw = D_WIDTH // D_GROUPS
    r = lax.broadcasted_iota(jnp.int32, (D_CHUNK, D_CHUNK), 0)
    c = lax.broadcasted_iota(jnp.int32, (D_CHUNK, D_CHUNK), 1)
    for g in range(D_GROUPS):
        w = jnp.where(r >= c, ws_ref[g], 0.0).astype(BF16)
        bias = bt_ref[:, g:g + 1]
        cols = slice(g * gw, (g + 1) * gw)
        vg = v[:, cols].astype(BF16)
        for n in range(x_ref.shape[0] // D_CHUNK):
            rows = slice(n * D_CHUNK, (n + 1) * D_CHUNK)
            mixed = jnp.dot(w, vg[rows], preferred_element_type=F32) + bias
            od_ref[rows, cols] = (u[rows, cols] * mixed).astype(od_ref.dtype)


def _proj_odd(x2d, w_bf16, w_s, b_s, ln_g, ln_b):
    t, k = x2d.shape
    n = w_bf16.shape[1]
    assert ODD_ROW_BLOCK % D_CHUNK == 0 and n == 2 * C_QK_W + 2 * C_W + 2 * D_WIDTH
    row_spec = lambda wd: pl.BlockSpec((ODD_ROW_BLOCK, wd), lambda i: (i, 0))
    return pl.pallas_call(
        _proj_odd_kernel,
        grid=(t // ODD_ROW_BLOCK,),
        in_specs=[row_spec(k), _const_spec((k, n)), _const_spec((D_GROUPS, D_CHUNK, D_CHUNK)),
                  _const_spec((D_CHUNK, D_GROUPS)), _const_spec((1, D_WIDTH)), _const_spec((1, D_WIDTH))],
        out_specs=[row_spec(2 * C_QK_W + C_W), row_spec(C_W), row_spec(D_WIDTH)],
        out_shape=[jax.ShapeDtypeStruct((t, 2 * C_QK_W + C_W), BF16),
                   jax.ShapeDtypeStruct((t, C_W), F32),
                   jax.ShapeDtypeStruct((t, D_WIDTH), BF16)],
        compiler_params=_params(("parallel",)),
        name="in_proj_odd",
    )(x2d, w_bf16, w_s.astype(F32), b_s.T.astype(F32), ln_g.reshape(1, D_WIDTH), ln_b.reshape(1, D_WIDTH))


def _transposed_value_tiles(vals, block):
    ones_tile = (lax.broadcasted_iota(jnp.int32, (BF16_ROWS, block), 0) == 0).astype(BF16)
    return [jnp.concatenate([vals[r:r + block].T.astype(BF16), ones_tile], axis=0)
            for r in range(0, vals.shape[0], block)]


def _bf16_terms(value, n_terms):
    terms, rest = [], np.float32(value)
    for _ in range(n_terms):
        bits = rest.view(np.uint32)
        kept = (bits + np.uint32(0x7FFF) + ((bits >> np.uint32(16)) & np.uint32(1))) & np.uint32(0xFFFF0000)
        terms.append(float(kept.view(np.float32)))
        rest = np.float32(rest - kept.view(np.float32))
    return terms


def _alibi_slope(h):
    return 2.0 ** (-8.0 * (h + 1) / A_HEADS)


def _transposed_query_tiles(q, block):
    tiles = []
    for r in range(0, q.shape[0], block):
        qt = q[r:r + block].T * (HEAD_DIM ** -0.5 * LOG2E)
        row = lax.broadcasted_iota(jnp.int32, qt.shape, 0)
        tiles.append((jnp.where(row < HEAD_DIM, qt, 0.0).astype(BF16),
                      jnp.where(row >= HEAD_DIM, qt, 0.0).astype(BF16)))
    return tiles


def _proj_even_kernel(x_ref, w_ref, h_ref, qta_ref, vta_ref, vtb_ref):
    x = x_ref[...].astype(BF16)
    vd = 2 * HEAD_DIM
    dot = lambda c: jnp.dot(x, w_ref[:, c:c + 512], preferred_element_type=F32)
    vb = dot(3 * A_W + 2 * B_W)
    for hp in range(B_HEADS // 2):
        tiles = _transposed_value_tiles(vb[:, hp * vd:(hp + 1) * vd], BAND_BLOCK)
        for n, tile in enumerate(tiles):
            vtb_ref[0, hp, n] = tile
    va = dot(2 * A_W)
    for h in range(A_HEADS):
        vta_ref[0, h, 0] = _transposed_value_tiles(va[:, h * vd:(h + 1) * vd], ATT_BLOCK)[0]
    qa = dot(0)
    srow = lax.broadcasted_iota(jnp.int32, (vd, ATT_BLOCK), 0)
    for h in range(A_HEADS):
        slope_rows = jnp.zeros((vd, ATT_BLOCK), F32)
        for k, term in enumerate(_bf16_terms(_alibi_slope(h) * LOG2E, POS_TERMS)):
            slope_rows = jnp.where((srow == k) | (srow == k + POS_TERMS), term, slope_rows)
        for mp, tile in enumerate(_transposed_query_tiles(qa[:, h * vd:(h + 1) * vd], ATT_BLOCK)[0]):
            qta_ref[0, h, 0, mp, 0:vd] = tile
            qta_ref[0, h, 0, mp, vd:2 * vd] = slope_rows.astype(BF16)
    ka = dot(A_W)
    r = lax.broadcasted_iota(jnp.int32, (ROW_BLOCK, vd), 0)
    col = lax.broadcasted_iota(jnp.int32, (ROW_BLOCK, vd), 1)
    pos = jnp.where(col < POS_TERMS, r & (POS_SPLIT - 1),
                    jnp.where(col < 2 * POS_TERMS, r & ~(POS_SPLIT - 1), 0)).astype(F32).astype(BF16)
    for h in range(A_HEADS):
        h_ref[:, 2 * vd * h:2 * vd * h + vd] = ka[:, h * vd:(h + 1) * vd].astype(BF16)
        h_ref[:, 2 * vd * h + vd:2 * vd * (h + 1)] = pos
    h_ref[:, 2 * A_W:2 * A_W + B_W] = dot(3 * A_W).astype(BF16)
    h_ref[:, 2 * A_W + B_W:2 * A_W + 2 * B_W] = dot(3 * A_W + B_W).astype(BF16)


def _proj_even(x2d, w_bf16, bsz, seq):
    t, k = x2d.shape
    n = w_bf16.shape[1]
    assert ROW_BLOCK == ATT_BLOCK and seq % ROW_BLOCK == 0
    per_seq = seq // ROW_BLOCK
    vd = 2 * HEAD_DIM
    vrows = vd + BF16_ROWS
    nba, nbb, per_step_b = seq // ATT_BLOCK, seq // BAND_BLOCK, ROW_BLOCK // BAND_BLOCK
    seq_block = lambda i: (i // per_seq, 0, i % per_seq, 0, 0)
    seq_block6 = lambda i: (i // per_seq, 0, i % per_seq, 0, 0, 0)
    return pl.pallas_call(
        _proj_even_kernel,
        grid=(t // ROW_BLOCK,),
        in_specs=[pl.BlockSpec((ROW_BLOCK, k), lambda i: (i, 0)), _const_spec((k, n))],
        out_specs=[pl.BlockSpec((ROW_BLOCK, 2 * A_W + 2 * B_W), lambda i: (i, 0)),
                   pl.BlockSpec((1, A_HEADS, 1, 2, 2 * vd, ATT_BLOCK), seq_block6),
                   pl.BlockSpec((1, A_HEADS, 1, vrows, ATT_BLOCK), seq_block),
                   pl.BlockSpec((1, B_HEADS // 2, per_step_b, vrows, BAND_BLOCK), seq_block)],
        out_shape=[jax.ShapeDtypeStruct((t, 2 * A_W + 2 * B_W), BF16),
                   jax.ShapeDtypeStruct((bsz, A_HEADS, nba, 2, 2 * vd, ATT_BLOCK), BF16),
                   jax.ShapeDtypeStruct((bsz, A_HEADS, nba, vrows, ATT_BLOCK), BF16),
                   jax.ShapeDtypeStruct((bsz, B_HEADS // 2, nbb, vrows, BAND_BLOCK), BF16)],
        compiler_params=_params(("parallel",)),
        name="in_proj_even",
    )(x2d, w_bf16)


def _mix_ffn_kernel(a_ref, b_ref, x_ref, wo_ref, g1_ref, b1_ref, wg_ref, wu_ref, wd_ref,
                    g2_ref, b2_ref, o_ref):
    ka = a_ref.shape[-1]
    half = x_ref.shape[0] // 2
    halves = (slice(0, half), slice(half, 2 * half))
    mix = [jnp.dot(a_ref[rows, :], wo_ref[0:ka, :], preferred_element_type=F32)
           + jnp.dot(b_ref[rows, :], wo_ref[ka:, :], preferred_element_type=F32) for rows in halves]
    x1 = [_layer_norm(ALPHA * x_ref[rows, :] + m, g1_ref[...], b1_ref[...]) for rows, m in zip(halves, mix)]
    xb = [v.astype(BF16) for v in x1]
    gate = [jnp.dot(v, wg_ref[...], preferred_element_type=F32) for v in xb]
    up = [jnp.dot(v, wu_ref[...], preferred_element_type=F32) for v in xb]
    h = [(g * jax.nn.sigmoid(g) * u).astype(BF16) for g, u in zip(gate, up)]
    y = [ALPHA * v + jnp.dot(hh, wd_ref[...], preferred_element_type=F32) for v, hh in zip(x1, h)]
    for rows, v in zip(halves, y):
        o_ref[rows, :] = _layer_norm(v, g2_ref[...], b2_ref[...])


def _mix_ffn(a, b, x2d, wo, g1, b1, wg, wu, wd, g2, b2):
    t, d = x2d.shape
    ka, kb = a.shape[1], b.shape[1]
    f = wg.shape[1]
    vec = lambda v: v.reshape(1, d)
    return pl.pallas_call(
        _mix_ffn_kernel,
        grid=(t // ROW_BLOCK,),
        in_specs=[pl.BlockSpec((ROW_BLOCK, ka), lambda i: (i, 0)),
                  pl.BlockSpec((ROW_BLOCK, kb), lambda i: (i, 0)),
                  pl.BlockSpec((ROW_BLOCK, d), lambda i: (i, 0)),
                  _const_spec((ka + kb, d)), _const_spec((1, d)), _const_spec((1, d)),
                  _const_spec((d, f)), _const_spec((d, f)), _const_spec((f, d)),
                  _const_spec((1, d)), _const_spec((1, d))],
        out_specs=pl.BlockSpec((ROW_BLOCK, d), lambda i: (i, 0)),
        out_shape=jax.ShapeDtypeStruct((t, d), F32),
        compiler_params=_params(("parallel",)),
        name="mix_ffn",
    )(a, b, x2d, wo, vec(g1), vec(b1), wg, wu, wd, vec(g2), vec(b2))


def _diff_attn_kernel(qt_ref, k_ref, vt_ref, diag_ref, diag_max_ref, step_ref,
                      lq1_ref, lk1_ref, lq2_ref, lk2_ref, g_ref, o_ref,
                      sa_ref, sb_ref, ra_ref, rb_ref, m_ref, acc_ref, knorm_ref, *, lam_init):
    t = ATT_BLOCK
    n_chunks = t // CHUNK
    i = pl.program_id(2)
    qt_ref = qt_ref.at[0, 0, 0]
    vd = 2 * HEAD_DIM

    @pl.when(i == 0)
    def _():
        lane = lax.broadcasted_iota(jnp.int32, (vd, vd), 0)
        col = lax.broadcasted_iota(jnp.int32, (vd, vd), 1)
        pick = ((lane // HEAD_DIM) == col).astype(BF16)

        def block_norms(j, carry):
            kb = k_ref[0, pl.ds(pl.multiple_of(j * t, t), t), 0:vd].astype(F32)
            sums = jnp.dot((kb * kb).astype(BF16), pick, preferred_element_type=F32)
            return jnp.maximum(carry, jnp.max(sums, axis=0, keepdims=True))

        n = lax.fori_loop(0, k_ref.shape[1] // t, block_norms, jnp.zeros((1, vd), F32), unroll=True)
        knorm_ref[0] = jnp.broadcast_to(jnp.sqrt(n[:, 0:1]), (1, t))
        knorm_ref[1] = jnp.broadcast_to(jnp.sqrt(n[:, 1:2]), (1, t))

    half = t // 2
    lo, hi = slice(0, half), slice(half, t)

    def chunk_maxima(s):
        return jnp.max(s.reshape(s.shape[0] // CHUNK, CHUNK // 8, 8, s.shape[1]), axis=1)

    def diagonal_scores():
        start = pl.multiple_of(i * t, t)
        for mp in range(2):
            top = jnp.dot(k_ref[0, pl.ds(start, half), 0:vd], qt_ref[mp, 0:vd], preferred_element_type=F32)
            bot = jnp.dot(k_ref[0, pl.ds(pl.multiple_of(start + half, half), half), 0:vd],
                          qt_ref[mp, 0:vd, hi], preferred_element_type=F32)
            sa_ref[mp, lo] = top
            sa_ref[mp, hi, hi] = bot
            ra_ref[mp, 0:n_chunks // 2] = chunk_maxima(top)
            ra_ref[mp, n_chunks // 2:, :, hi] = chunk_maxima(bot)
            ra_ref[mp, n_chunks // 2:, :, lo] = jnp.zeros((n_chunks // 2, 8, half), F32)

    def diagonal_softmax_pv():
        vt_blk = vt_ref[0, 0, i]
        for mp in range(2):
            bound = jnp.max(ra_ref[mp] + diag_max_ref[0][:, None, :], axis=0)
            m_new = jnp.max(bound, axis=0, keepdims=True)
            p_top = jnp.exp2(sa_ref[mp, lo] + diag_ref[0, lo] - m_new).astype(BF16)
            p_bot = jnp.exp2(sa_ref[mp, hi, hi] + diag_ref[0, hi, hi] - m_new[:, hi]).astype(BF16)
            acc = jnp.dot(vt_blk[:, lo], p_top, preferred_element_type=F32)
            acc_ref[mp, :, lo] = acc[:, lo]
            acc_ref[mp, :, hi] = acc[:, hi] + jnp.dot(vt_blk[:, hi], p_bot, preferred_element_type=F32)
            m_ref[mp] = m_new + step_ref[0]

    def scores(j, s_ref, r_ref):
        start = pl.multiple_of(j * t, t)
        for mp in range(2):
            s = jnp.dot(k_ref[0, pl.ds(start, t), :], qt_ref[mp], preferred_element_type=F32)
            s_ref[mp] = s
            r_ref[mp] = chunk_maxima(s)

    def softmax_pv(j, s_ref, r_ref):
        vt_blk = vt_ref[0, 0, j]
        step = step_ref[0]
        for mp in range(2):
            m_prev = m_ref[mp]
            bound = jnp.max(jnp.max(r_ref[mp], axis=0), axis=0, keepdims=True)
            m_new = jnp.maximum(m_prev, bound)
            alpha = jnp.exp2(m_prev - m_new)
            p = jnp.exp2(s_ref[mp] - m_new)
            acc_ref[mp] = alpha * acc_ref[mp] + jnp.dot(vt_blk, p.astype(BF16),
                                                        preferred_element_type=F32)
            m_ref[mp] = m_new + step

    def prev_block(r):
        return jnp.maximum(i - r, 0)

    diagonal_scores()
    scores(prev_block(1), sb_ref, rb_ref)
    diagonal_softmax_pv()

    tile_top = step_ref[0] * ((t - 1) / t)
    inv_step = 1.0 / step_ref[0]
    reach = jnp.zeros((1, t), F32)
    for mp in range(2):
        qb = qt_ref[mp, 0:vd].astype(F32)
        qnorm = jnp.sqrt(jnp.sum(qb * qb, axis=0, keepdims=True))
        gap = qnorm * knorm_ref[mp] * NORM_SLACK + tile_top - m_ref[mp]
        reach = jnp.maximum(reach, (gap + SKIP_LOG2) * inv_step + 1.0)
    reach = jnp.floor(jnp.minimum(reach, float(k_ref.shape[1] // t))).astype(jnp.int32)
    n_keep = jnp.minimum(jnp.max(reach), i)

    def past_pair(r0):
        scores(prev_block(r0 + 1), sa_ref, ra_ref)
        softmax_pv(i - r0, sb_ref, rb_ref)
        scores(prev_block(r0 + 2), sb_ref, rb_ref)
        softmax_pv(i - r0 - 1, sa_ref, ra_ref)

    def past_oct(jj, carry):
        for pr in range(4):
            past_pair(8 * jj + 2 * pr + 1)
        return carry

    n_octs = lax.shift_right_logical(n_keep, 3)
    lax.fori_loop(0, n_octs, past_oct, 0)
    done = 8 * n_octs

    @pl.when((n_keep & 4) == 4)
    def _():
        past_pair(done + 1)
        past_pair(done + 3)

    @pl.when((n_keep & 2) == 2)
    def _():
        past_pair(done + (n_keep & 4) + 1)

    @pl.when((n_keep & 1) == 1)
    def _():
        softmax_pv(i - n_keep, sb_ref, rb_ref)

    lam = (jnp.exp(jnp.sum(lq1_ref[...] * lk1_ref[...], axis=-1, keepdims=True))
           - jnp.exp(jnp.sum(lq2_ref[...] * lk2_ref[...], axis=-1, keepdims=True)) + lam_init)
    ot = (acc_ref[0, 0:vd] / acc_ref[0, vd:vd + 1]
          - lam * (acc_ref[1, 0:vd] / acc_ref[1, vd:vd + 1]))
    ms = jnp.mean(ot * ot, axis=0, keepdims=True)
    y = (ot * lax.rsqrt(ms + LN_EPS)).T * g_ref[...] * (1.0 - lam_init)
    o_ref[0] = y.astype(o_ref.dtype)


def _diff_attention(keys, qt, vt, lam_q1, lam_k1, lam_q2, lam_k2, subln_g, lam_init):
    bsz, seq, _ = keys.shape
    t = ATT_BLOCK
    nb = seq // t
    n_chunks = t // CHUNK
    vd = 2 * HEAD_DIM
    slopes = jnp.asarray([_alibi_slope(h) for h in range(A_HEADS)], F32)
    kk = jnp.arange(t)[:, None]
    qq = jnp.arange(t)[None, :]
    rel = (qq - kk).astype(F32)
    c = slopes * LOG2E
    allowed = (kk // CHUNK) <= (qq // CHUNK)
    diag = jnp.where(allowed[None], c[:, None, None] * (qq.astype(F32) - jnp.abs(rel))[None], NEG_INF)
    chunk_max = lambda tile: jnp.max(tile.reshape(A_HEADS, n_chunks, CHUNK, t), axis=2)
    step = jnp.broadcast_to((slopes * (LOG2E * t))[:, None, None], (A_HEADS, 1, t))
    vec = lambda v: v.reshape(1, HEAD_DIM).astype(F32)
    small = lambda shape: pl.BlockSpec(shape, lambda b, h, i: (0,) * len(shape))
    per_head = lambda shape: pl.BlockSpec(shape, lambda b, h, i: (h,) + (0,) * (len(shape) - 1))
    return pl.pallas_call(
        functools.partial(_diff_attn_kernel, lam_init=lam_init),
        grid=(bsz, A_HEADS, nb),
        in_specs=[pl.BlockSpec((1, 1, 1, 2, 2 * vd, t), lambda b, h, i: (b, h, i, 0, 0, 0)),
                  pl.BlockSpec((1, seq, 2 * vd), lambda b, h, i: (b, 0, h)),
                  pl.BlockSpec((1, 1, nb, vd + BF16_ROWS, t), lambda b, h, i: (b, h, 0, 0, 0)),
                  per_head((1, t, t)),
                  per_head((1, n_chunks, t)), per_head((1, 1, t)),
                  small((1, HEAD_DIM)), small((1, HEAD_DIM)), small((1, HEAD_DIM)), small((1, HEAD_DIM)),
                  small((1, vd))],
        out_specs=pl.BlockSpec((1, t, vd), lambda b, h, i: (b, i, h)),
        out_shape=jax.ShapeDtypeStruct((bsz, seq, A_W), BF16),
        scratch_shapes=[pltpu.VMEM((2, t, t), F32), pltpu.VMEM((2, t, t), F32),
                        pltpu.VMEM((2, n_chunks, 8, t), F32), pltpu.VMEM((2, n_chunks, 8, t), F32),
                        pltpu.VMEM((2, 1, t), F32),
                        pltpu.VMEM((2, vd + BF16_ROWS, t), F32),
                        pltpu.VMEM((2, 1, t), F32)],
        compiler_params=_params(("parallel", "parallel", "arbitrary")),
        name="diff_attn",
    )(qt, keys, vt, diag, chunk_max(diag), step,
      vec(lam_q1), vec(lam_k1), vec(lam_q2), vec(lam_k2), subln_g.reshape(1, vd).astype(F32))


def _band_attn_kernel(q_ref, k_ref, vt_ref, diag_vals_ref, band_ref, o_ref, bias_ref, sa_ref, sb_ref):
    t = BAND_BLOCK
    nb = k_ref.shape[1] // t
    n_win = BAND_WINDOW // t
    vd = 2 * HEAD_DIM
    row = lax.broadcasted_iota(jnp.int32, (vd, t), 0)
    n_vals = diag_vals_ref.shape[-1]

    for hh in range(2):
        vals = jnp.broadcast_to(diag_vals_ref[hh], (8, n_vals))
        for c0 in range(0, BAND_WINDOW, 8):
            rolled = pltpu.roll(vals, (c0 - (BAND_WINDOW - 1)) % n_vals, axis=1, stride=1, stride_axis=0)
            bias_ref[hh, c0:c0 + 8, :] = jnp.where(band_ref[c0:c0 + 8, :] > 0.0, rolled[:, 0:t], NEG_INF)

    def rows(n):
        return pl.ds(pl.multiple_of(n * t, t), t)

    def key_block(n, w):
        return jnp.maximum(n - (n_win - 1) + w, 0)

    def scores(n, s_ref):
        qt = q_ref[0, rows(n), :].astype(F32).T * (HEAD_DIM ** -0.5 * LOG2E)
        for hh in range(2):
            in_head = (row < HEAD_DIM) if hh == 0 else (row >= HEAD_DIM)
            qh = jnp.where(in_head, qt, 0.0).astype(BF16)
            for w in range(n_win):
                s_ref[hh, w * t:(w + 1) * t] = jnp.dot(k_ref[0, rows(key_block(n, w)), :], qh,
                                                       preferred_element_type=F32)

    def softmax_pv(n, s_ref, mask_keys):
        outs = []
        for hh in range(2):
            st = s_ref[hh] + bias_ref[hh]
            if mask_keys:
                key_pos = (n - (n_win - 1)) * t + lax.broadcasted_iota(jnp.int32, (BAND_WINDOW, 1), 0)
                st = jnp.where(key_pos >= 0, st, NEG_INF)
            p = jnp.exp2(st - jnp.max(st, axis=0, keepdims=True)).astype(BF16)
            ot = jnp.dot(vt_ref[0, 0, key_block(n, 0)], p[0:t], preferred_element_type=F32)
            for w in range(1, n_win):
                ot = ot + jnp.dot(vt_ref[0, 0, key_block(n, w)], p[w * t:(w + 1) * t],
                                  preferred_element_type=F32)
            outs.append(ot[0:vd] / ot[vd:vd + 1])
        o_ref[0, rows(n), :] = jnp.where(row < HEAD_DIM, outs[0], outs[1]).T.astype(o_ref.dtype)

    first = n_win - 1
    assert first % 2 == 0
    scores(0, sa_ref)
    for n in range(first):
        cur, nxt = (sa_ref, sb_ref) if n % 2 == 0 else (sb_ref, sa_ref)
        scores(n + 1, nxt)
        softmax_pv(n, cur, True)

    def pair(jj, carry):
        n0 = first + 2 * jj
        scores(n0 + 1, sb_ref)
        softmax_pv(n0, sa_ref, False)
        scores(n0 + 2, sa_ref)
        softmax_pv(n0 + 1, sb_ref, False)
        return carry

    n_pairs = (nb - first - 2) // 2
    lax.fori_loop(0, n_pairs, pair, 0, unroll=BAND_UNROLL)
    for n in range(first + 2 * n_pairs, nb):
        if n + 1 < nb:
            scores(n + 1, sb_ref)
        softmax_pv(n, sa_ref, False)
        sa_ref, sb_ref = sb_ref, sa_ref


def _band_attention(hproj, vt, rel_bias):
    bsz, seq, _ = hproj.shape
    t = BAND_BLOCK
    vd = 2 * HEAD_DIM
    c = jnp.arange(BAND_WINDOW)[:, None]
    r = jnp.arange(t)[None, :]
    in_band = ((c // CHUNK) >= (r // CHUNK)) & ((c // CHUNK) <= (r // CHUNK) + B_LEFT_CHUNKS)
    rb = rel_bias.astype(F32) * LOG2E
    n_lo = BAND_WINDOW - 1 - B_LEFT_CHUNKS * CHUNK - REL_CLIP
    n_hi = t + BAND_WINDOW - n_lo - rb.shape[1]
    diag_vals = jnp.concatenate([jnp.broadcast_to(rb[:, :1], (B_HEADS, n_lo)), rb,
                                 jnp.broadcast_to(rb[:, -1:], (B_HEADS, n_hi))], axis=1)[:, None, :]
    qcol, kcol = (2 * A_W) // vd, (2 * A_W + B_W) // vd
    seq_spec = lambda col: pl.BlockSpec((1, seq, vd), lambda b, hp: (b, 0, col + hp))
    return pl.pallas_call(
        _band_attn_kernel,
        grid=(bsz, B_HEADS // 2),
        in_specs=[seq_spec(qcol), seq_spec(kcol),
                  pl.BlockSpec((1, 1, seq // t, vd + BF16_ROWS, t), lambda b, hp: (b, hp, 0, 0, 0)),
                  pl.BlockSpec((2, 1, t + BAND_WINDOW), lambda b, hp: (hp, 0, 0)),
                  pl.BlockSpec((BAND_WINDOW, t), lambda b, hp: (0, 0))],
        out_specs=pl.BlockSpec((1, seq, vd), lambda b, hp: (b, 0, hp)),
        out_shape=jax.ShapeDtypeStruct((bsz, seq, B_W), BF16),
        scratch_shapes=[pltpu.VMEM((2, BAND_WINDOW, t), F32),
                        pltpu.VMEM((2, BAND_WINDOW, t), F32), pltpu.VMEM((2, BAND_WINDOW, t), F32)],
        compiler_params=_params(("parallel", "parallel")),
        name="band_attn",
    )(hproj, hproj, vt, diag_vals, in_band.astype(F32))


def _retention_kernel(q_ref, k_ref, v_ref, gate_ref, dmask_ref, qdec_ref, kdec_ref, cdec_ref,
                      g_ref, beta_ref, o_ref, state_ref):
    ln = RET_BLOCK
    nb = q_ref.shape[1] // ln
    vd = 2 * HEAD_DIM
    scale = HEAD_DIM ** -0.5
    lane = lax.broadcasted_iota(jnp.int32, (ln, vd), 1)
    state_ref[...] = jnp.zeros(state_ref.shape, F32)

    def block(n, carry):
        rows = pl.ds(pl.multiple_of(n * ln, ln), ln)
        q = q_ref[0, rows, :]
        k = k_ref[0, rows, :]
        for hh in range(2):
            in_head = (lane < HEAD_DIM) if hh == 0 else (lane >= HEAD_DIM)
            qh = jnp.where(in_head, q, jnp.zeros_like(q))
            kh = jnp.where(in_head, k.astype(F32) * scale, 0.0)
            sl = slice(hh * vd, (hh + 1) * vd)
            v = v_ref[0, rows, sl]
            scores = lax.dot_general(qh, kh.astype(BF16), (((1,), (1,)), ((), ())),
                                     preferred_element_type=F32) * dmask_ref[hh]
            inner = jnp.dot(scores.astype(BF16), v, preferred_element_type=F32)
            state = state_ref[hh]
            cross = jnp.dot(qh, state.astype(BF16), preferred_element_type=F32) * qdec_ref[hh]
            kdt = (kh * kdec_ref[hh]).T.astype(BF16)
            kv = jnp.dot(kdt, v, preferred_element_type=F32)
            state_ref[hh] = cdec_ref[hh] * state + kv
            o = inner + cross
            mu = jnp.mean(o, axis=-1, keepdims=True)
            d = o - mu
            var = jnp.mean(d * d, axis=-1, keepdims=True)
            y = d * lax.rsqrt(var + LN_EPS) * g_ref[:, sl] + beta_ref[:, sl]
            o_ref[0, rows, sl] = (y * gate_ref[0, rows, sl]).astype(o_ref.dtype)
        return carry

    lax.fori_loop(0, nb, block, 0, unroll=8)


def _retention(qkv, gate, gn_g, gn_b):
    bsz, seq, _ = qkv.shape
    ln = RET_BLOCK
    vd = 2 * HEAD_DIM
    log_gamma = jnp.log(1.0 - 2.0 ** (-5.0 - jnp.arange(C_HEADS, dtype=F32)))
    idx = jnp.arange(ln, dtype=F32)
    diff = idx[:, None] - idx[None, :]
    dmask = jnp.where(diff >= 0, jnp.exp(log_gamma[:, None, None] * jnp.maximum(diff, 0.0)), 0.0)
    qdec = jnp.broadcast_to(jnp.exp(log_gamma[:, None] * (idx + 1.0))[:, :, None], (C_HEADS, ln, vd))
    kdec = jnp.broadcast_to(jnp.exp(log_gamma[:, None] * (ln - 1.0 - idx))[:, :, None], (C_HEADS, ln, vd))
    cdec = jnp.broadcast_to(jnp.exp(log_gamma * ln)[:, None, None], (C_HEADS, 1, vd))
    pair = lambda shape: pl.BlockSpec(shape, lambda b, hp: (hp,) + (0,) * (len(shape) - 1))
    return pl.pallas_call(
        _retention_kernel,
        grid=(bsz, C_HEADS // 2),
        in_specs=[pl.BlockSpec((1, seq, vd), lambda b, hp: (b, 0, hp)),
                  pl.BlockSpec((1, seq, vd), lambda b, hp: (b, 0, C_QK_W // vd + hp)),
                  pl.BlockSpec((1, seq, 2 * vd), lambda b, hp: (b, 0, (2 * C_QK_W) // (2 * vd) + hp)),
                  pl.BlockSpec((1, seq, 2 * vd), lambda b, hp: (b, 0, hp)),
                  pair((2, ln, ln)), pair((2, ln, vd)), pair((2, ln, vd)), pair((2, 1, vd)),
                  pl.BlockSpec((1, 2 * vd), lambda b, hp: (0, hp)),
                  pl.BlockSpec((1, 2 * vd), lambda b, hp: (0, hp))],
        out_specs=pl.BlockSpec((1, seq, 2 * vd), lambda b, hp: (b, 0, hp)),
        out_shape=jax.ShapeDtypeStruct((bsz, seq, C_W), BF16),
        scratch_shapes=[pltpu.VMEM((2, vd, vd), F32)],
        compiler_params=_params(("parallel", "parallel")),
        name="retention",
    )(qkv, qkv, qkv, gate, dmask, qdec, kdec, cdec, gn_g.reshape(1, C_W), gn_b.reshape(1, C_W))


def kernel(x, w_in_even, lam_q1, lam_k1, lam_q2, lam_k2, diff_subln_g, rel_bias, w_out_even,
           w_in_odd, ret_gn_g, ret_gn_b, sgu_ln_g, sgu_ln_b, sgu_w, sgu_b, w_out_odd,
           ln_mix_g, ln_mix_b, ffn_w_gate, ffn_w_up, ffn_w_down, ln_ffn_g, ln_ffn_b):
    bsz, seq, d = x.shape
    t = bsz * seq
    x2d = x.reshape(t, d)
    for l in range(DEPTH):
        j = l // 2
        if l % 2 == 0:
            hproj, qta, vta, vtb = _proj_even(x2d, w_in_even[j].astype(BF16), bsz, seq)
            hproj = hproj.reshape(bsz, seq, -1)
            lam_init = 0.8 - 0.6 * math.exp(-0.3 * l)
            mix_a = _diff_attention(hproj, qta, vta, lam_q1[j], lam_k1[j], lam_q2[j], lam_k2[j],
                                    diff_subln_g[j], lam_init)
            mix_b = _band_attention(hproj, vtb, rel_bias[j])
            w_out = w_out_even[j]
        else:
            qkv, gate, mix_b = _proj_odd(x2d, w_in_odd[j].astype(BF16), sgu_w[j], sgu_b[j],
                                         sgu_ln_g[j], sgu_ln_b[j])
            mix_a = _retention(qkv.reshape(bsz, seq, -1), gate.reshape(bsz, seq, -1),
                               ret_gn_g[j], ret_gn_b[j])
            w_out = w_out_odd[j]
        x2d = _mix_ffn(mix_a.reshape(t, -1), mix_b.reshape(t, -1), x2d, w_out.astype(BF16),
                       ln_mix_g[l], ln_mix_b[l], ffn_w_gate[l].astype(BF16), ffn_w_up[l].astype(BF16),
                       ffn_w_down[l].astype(BF16), ln_ffn_g[l], ln_ffn_b[l])
    return x2d.reshape(bsz, seq, d)
```
